```python
import jax
import jax.numpy as jnp
from jax import lax
import numpy as np

D_MODEL = 1024
BATCH = 2
SEQ = 16384
DEPTH = 2

GRID_W = 64
CTX_LEN = 256
EPS = 1e-6
NEG_INF = -1e30

HEAD_DIM = 64
N_HEADS = (D_MODEL // 2) // HEAD_DIM
KV_HEADS = N_HEADS // 4
GROUP = N_HEADS // KV_HEADS
WINDOW = 128
ATTN_BLOCK = 128
ROPE_BASE = 10000.0
ROPE_FREQS = HEAD_DIM // 4

LRU_WIDTH = D_MODEL // 4
LRU_BLOCKS = 4
LRU_BLOCK_W = LRU_WIDTH // LRU_BLOCKS
LRU_C = 8.0
CONV_W = 4
CONV_PAD = (CONV_W // 2, CONV_W - 1 - CONV_W // 2)
SQRT_FLOOR = 1e-12

HG_HEADS = 4
HG_DK = (D_MODEL // 4) // HG_HEADS
HG_DV = (D_MODEL // 4) // HG_HEADS
HG_CHUNK = 32

ATTN_Q_W = N_HEADS * HEAD_DIM
ATTN_KV_W = KV_HEADS * HEAD_DIM
HG_K_W = HG_HEADS * HG_DK
HG_V_W = HG_HEADS * HG_DV
MIX_W = ATTN_Q_W + LRU_WIDTH + HG_V_W
IN_SPLITS = (ATTN_Q_W, ATTN_KV_W, ATTN_KV_W, LRU_WIDTH, LRU_WIDTH, HG_K_W, HG_K_W, HG_K_W, HG_V_W, HG_V_W)
IN_W = ATTN_Q_W + 2 * ATTN_KV_W + 2 * LRU_WIDTH + 3 * HG_K_W + 2 * HG_V_W

N_EXPERTS = 32
TOP_K = 4
D_FF = D_MODEL
SWIGLU_LIMIT = 7.0
SWIGLU_ALPHA = 1.702
MOE_BLOCK = 128

kernel_name = 'hybrid_dit_swa_rglru_hgrn2_moe'


def rms_norm(x, g):
    xf = x.astype(jnp.float32)
    y = xf * lax.rsqrt(jnp.mean(xf * xf, axis=-1, keepdims=True) + EPS)
    return (y * g.astype(jnp.float32)).astype(x.dtype)


def modulate(h, shift, scale):
    return h * (1.0 + scale) + shift


def axial_rope_tables(n_tokens, dtype):
    n_rows = n_tokens // GRID_W
    rows = jnp.repeat(jnp.arange(n_rows, dtype=jnp.float32), GRID_W)
    cols = jnp.tile(jnp.arange(GRID_W, dtype=jnp.float32), n_rows)
    inv_freq = ROPE_BASE ** (-jnp.arange(ROPE_FREQS, dtype=jnp.float32) / ROPE_FREQS)
    ang = jnp.stack([rows[:, None] * inv_freq, cols[:, None] * inv_freq], axis=1)
    return jnp.cos(ang).astype(dtype), jnp.sin(ang).astype(dtype)


def apply_axial_rope(x, cos, sin):
    b, t, h, _ = x.shape
    xr = x.reshape(b, t, h, 2, 2, ROPE_FREQS)
    x1, x2 = xr[..., 0, :], xr[..., 1, :]
    cs, sn = cos[:, None], sin[:, None]
    out = jnp.stack([x1 * cs - x2 * sn, x1 * sn + x2 * cs], axis=-2)
    return out.reshape(b, t, h, HEAD_DIM)


def sink_softmax(parts, sink):
    snk = jnp.broadcast_to(sink[None, :, :, None, None], parts[0].shape[:-1] + (1,))
    p = jax.nn.softmax(jnp.concatenate(parts + [snk], axis=-1), axis=-1)
    return p[..., :-1]


def window_attention(q_l, k_l, v_l, q_c, k_c, v_c, sink, cos, sin, with_ctx_out):
    bsz, seq, _ = q_l.shape
    ctx_len = k_c.shape[1]
    nb = seq // ATTN_BLOCK
    scale = HEAD_DIM ** -0.5

    def heads(t, n):
        return t.reshape(t.shape[0], t.shape[1], n, HEAD_DIM)

    ql = apply_axial_rope(heads(q_l, N_HEADS), cos, sin) * scale
    kl = apply_axial_rope(heads(k_l, KV_HEADS), cos, sin)
    vl = heads(v_l, KV_HEADS)
    kc = heads(k_c, KV_HEADS)
    vc = heads(v_c, KV_HEADS)
    snk = sink.astype(jnp.float32).reshape(KV_HEADS, GROUP)

    def band(t):
        tp = jnp.pad(t, ((0, 0), (ATTN_BLOCK, ATTN_BLOCK), (0, 0), (0, 0)))
        tp = tp.reshape(bsz, nb + 2, ATTN_BLOCK, KV_HEADS, HEAD_DIM)
        return jnp.concatenate([tp[:, :-2], tp[:, 1:-1], tp[:, 2:]], axis=2)

    start = jnp.arange(nb)[:, None, None] * ATTN_BLOCK
    qpos = start + jnp.arange(ATTN_BLOCK)[None, :, None]
    kpos = start - ATTN_BLOCK + jnp.arange(3 * ATTN_BLOCK)[None, None, :]
    valid = (jnp.abs(qpos - kpos) <= WINDOW) & (kpos >= 0) & (kpos < seq)
    n_loc = 3 * ATTN_BLOCK

    def block(args):
        qb, kb, vb, mb = args
        s_loc = jnp.where(mb, jnp.einsum('bqhgd,bkhd->bhgqk', qb, kb).astype(jnp.float32), NEG_INF)
        s_ctx = jnp.einsum('bqhgd,bchd->bhgqc', qb, kc).astype(jnp.float32)
        p = sink_softmax([s_loc, s_ctx], snk).astype(vb.dtype)
        return (jnp.einsum('bhgqk,bkhd->bqhgd', p[..., :n_loc], vb)
                + jnp.einsum('bhgqc,bchd->bqhgd', p[..., n_loc:], vc))

    qb = jnp.moveaxis(ql.reshape(bsz, nb, ATTN_BLOCK, KV_HEADS, GROUP, HEAD_DIM), 1, 0)
    o = lax.map(block, (qb, jnp.moveaxis(band(kl), 1, 0), jnp.moveaxis(band(vl), 1, 0), valid))
    out_l = jnp.moveaxis(o, 0, 1).reshape(bsz, seq, ATTN_Q_W)
    out_c = None
    if with_ctx_out:
        qc = heads(q_c, N_HEADS).reshape(bsz, ctx_len, KV_HEADS, GROUP, HEAD_DIM) * scale
        s = jnp.einsum('bqhgd,bkhd->bhgqk', qc, kc).astype(jnp.float32)
        p = sink_softmax([s], snk).astype(vc.dtype)
        out_c = jnp.einsum('bhgqk,bkhd->bqhgd', p, vc).reshape(bsz, ctx_len, ATTN_Q_W)
    return out_l, out_c


def short_conv(x, w, b):
    y = lax.conv_general_dilated(x, w[:, None, :].astype(x.dtype), window_strides=(1,),
                                 padding=[CONV_PAD], dimension_numbers=('NWC', 'WIO', 'NWC'),
                                 feature_group_count=x.shape[-1])
    return y + b


def linear_scan(a, b, h0):
    def combine(l, r):
        return l[0] * r[0], r[0] * l[1] + r[1]
    a_cum, h = lax.associative_scan(combine, (a, b), axis=1)
    h = h + a_cum * h0[:, None, :]
    return h, h[:, -1]


def prefix_scan(scan_fn, ctx_args, lat_args, init, reverse):
    flip = (lambda t: jnp.flip(t, axis=1)) if reverse else (lambda t: t)
    o_c, s_c = scan_fn(*[flip(t) for t in ctx_args], init)
    o_l, _ = scan_fn(*[flip(t) for t in lat_args], s_c)
    return flip(o_c), flip(o_l)


def rglru_gates(u, wr, br, wi, bi, lam):
    bsz, t, ch = u.shape
    ub = u.reshape(bsz, t, LRU_BLOCKS, LRU_BLOCK_W)
    r = jax.nn.sigmoid(jnp.einsum('btnc,ncd->btnd', ub, wr).reshape(bsz, t, ch) + br)
    i = jax.nn.sigmoid(jnp.einsum('btnc,ncd->btnd', ub, wi).reshape(bsz, t, ch) + bi)
    log_a = -LRU_C * r.astype(jnp.float32) * jax.nn.softplus(-lam.astype(jnp.float32))
    mult = jnp.sqrt(jnp.maximum(-jnp.expm1(2.0 * log_a), SQRT_FLOOR))
    b = mult * (i * u).astype(jnp.float32)
    return jnp.exp(log_a), b


def rglru_mixer(xl, gl, xc, gc, conv_w, conv_b, wr, br, wi, bi, lam, with_ctx_out):
    ul = short_conv(xl, conv_w, conv_b)
    uc = short_conv(xc, conv_w, conv_b)
    h0 = jnp.zeros((xl.shape[0], LRU_WIDTH), jnp.float32)
    outs = [prefix_scan(linear_scan,
                        rglru_gates(uc, wr[d], br[d], wi[d], bi[d], lam[d]),
                        rglru_gates(ul, wr[d], br[d], wi[d], bi[d], lam[d]),
                        h0, reverse=(d == 1)) for d in range(2)]
    yl = (outs[0][1] + outs[1][1]).astype(xl.dtype) * jax.nn.gelu(gl)
    yc = (outs[0][0] + outs[1][0]).astype(xc.dtype) * jax.nn.gelu(gc) if with_ctx_out else None
    return yl, yc


def gla_chunked(q, k, v, log_f, s0):
    bsz, t, h, _ = q.shape
    n = t // HG_CHUNK

    def r(a):
        return a.reshape(bsz, n, HG_CHUNK, h, a.shape[-1])

    q, k, v = r(q).astype(jnp.float32), r(k).astype(jnp.float32), r(v)
    b = jnp.cumsum(r(log_f).astype(jnp.float32), axis=2)
    b_last = b[:, :, -1:]
    b_ref = 0.5 * b_last
    q_i = q * jnp.exp(b - b_ref)
    k_i = k * jnp.exp(b_ref - b)
    q_s = q * jnp.exp(b)
    k_e = k * jnp.exp(b_last - b)
    causal = jnp.tril(jnp.ones((HG_CHUNK, HG_CHUNK), dtype=bool))
    att = jnp.where(causal, jnp.einsum('bnthk,bnshk->bnhts', q_i, k_i), 0.0)
    vf = v.astype(jnp.float32)
    o = jnp.einsum('bnhts,bnshv->bnthv', att, vf)
    kv = jnp.einsum('bnshk,bnshv->nbhkv', k_e, vf)
    decay = jnp.moveaxis(jnp.exp(b_last[:, :, 0]), 1, 0)

    def step(s, xs):
        dcy, kvc = xs
        return dcy[..., None] * s + kvc, s

    s_fin, s_start = lax.scan(step, s0.astype(jnp.float32), (decay, kv))
    o = o + jnp.einsum('bnthk,nbhkv->bnthv', q_s, s_start)
    return o.reshape(bsz, t, h, v.shape[-1]).astype(v.dtype), s_fin


def hgrn2_mixer(pl, pc, lb, norm_g, with_ctx_out):
    def prep(p, d):
        q, zf, zb, v, _ = p
        bsz, t, _ = q.shape
        z = (zf, zb)[d].reshape(bsz, t, HG_HEADS, HG_DK).astype(jnp.float32)
        lbd = lb[d].reshape(HG_HEADS, HG_DK)
        log_f = jnp.log(lbd + (1.0 - lbd) * jax.nn.sigmoid(z))
        k = (1.0 - lbd) * jax.nn.sigmoid(-z)
        return (q.reshape(bsz, t, HG_HEADS, HG_DK), k,
                v.reshape(bsz, t, HG_HEADS, HG_DV), log_f)

    s0 = jnp.zeros((pl[0].shape[0], HG_HEADS, HG_DK, HG_DV), jnp.float32)
    outs = [prefix_scan(gla_chunked, prep(pc, d), prep(pl, d), s0, reverse=(d == 1)) for d in range(2)]

    def readout(o, g):
        bsz, t = g.shape[:2]
        return rms_norm(o, norm_g.reshape(HG_HEADS, HG_DV)).reshape(bsz, t, HG_V_W) * jax.nn.silu(g)

    yl = readout(outs[0][1] + outs[1][1], pl[4])
    yc = readout(outs[0][0] + outs[1][0], pc[4]) if with_ctx_out else None
    return yl, yc


def clamped_swiglu(h, w_gu, b_gu, w_down, b_down):
    gu = h @ w_gu + b_gu
    gate = jnp.minimum(gu[:, :D_FF], SWIGLU_LIMIT)
    up = jnp.clip(gu[:, D_FF:], -SWIGLU_LIMIT, SWIGLU_LIMIT)
    glu = gate * jax.nn.sigmoid(SWIGLU_ALPHA * gate)
    return ((up + 1.0) * glu) @ w_down + b_down


def moe_ffn(h, router_w, router_b, w_gu, b_gu, w_down, b_down):
    n_tok, d = h.shape
    logits = (h @ router_w + router_b).astype(jnp.float32)
    top_val, top_idx = lax.top_k(logits, TOP_K)
    gates = jax.nn.softmax(top_val, axis=-1).astype(h.dtype)
    m = n_tok * TOP_K
    flat_e = top_idx.reshape(m)
    order = jnp.argsort(flat_e)
    sorted_e = flat_e[order]
    sorted_tok = (order // TOP_K).astype(jnp.int32)
    sorted_g = gates.reshape(m)[order]
    counts = jnp.bincount(flat_e, length=N_EXPERTS)
    starts = jnp.cumsum(counts) - counts
    padded = (counts + MOE_BLOCK - 1) // MOE_BLOCK * MOE_BLOCK
    pends = jnp.cumsum(padded)
    pstarts = pends - padded
    dest = pstarts[sorted_e] + jnp.arange(m) - starts[sorted_e]
    n_blocks = -(-m // MOE_BLOCK) + N_EXPERTS
    p_rows = n_blocks * MOE_BLOCK
    tok_buf = jnp.full((p_rows,), n_tok, jnp.int32).at[dest].set(sorted_tok)
    gate_buf = jnp.zeros((p_rows,), h.dtype).at[dest].set(sorted_g)
    blk_e = jnp.minimum(jnp.searchsorted(pends, jnp.arange(n_blocks) * MOE_BLOCK, side='right'), N_EXPERTS - 1)
    h_pad = jnp.concatenate([h, jnp.zeros((1, d), h.dtype)], axis=0)

    def run(args):
        tok, g, e = args
        return clamped_swiglu(h_pad[tok], w_gu[e], b_gu[e], w_down[e], b_down[e]) * g[:, None]

    y = lax.map(run, (tok_buf.reshape(n_blocks, MOE_BLOCK), gate_buf.reshape(n_blocks, MOE_BLOCK), blk_e))
    return jax.ops.segment_sum(y.reshape(p_rows, d), tok_buf, num_segments=n_tok + 1)[:n_tok]


def setup_inputs(seed: int = 0) -> dict:
    key = jax.random.key(seed)
    keys = iter(jax.random.split(key, 32))
    f32 = jnp.float32

    def nrm(shape, scale):
        return jax.random.normal(next(keys), shape, f32) * scale

    def gain(shape):
        return 1.0 + nrm(shape, 0.02)

    L, D = DEPTH, D_MODEL
    a0 = jax.random.uniform(next(keys), (L, 2, LRU_WIDTH), f32, 0.9, 0.999)
    root = a0 ** (1.0 / LRU_C)
    return {
        'x': nrm((BATCH, SEQ, D), 1.0),
        'c': nrm((BATCH, D), 1.0),
        'ctx': nrm((BATCH, CTX_LEN, D), 1.0),
        'c_ctx': nrm((D,), 1.0),
        'ada_w': nrm((L, D, 6 * D), 0.5 * D ** -0.5),
        'ada_b': nrm((L, 6 * D), 0.01),
        'norm1_g': gain((L, D)),
        'w_in': nrm((L, D, IN_W), D ** -0.5),
        'attn_sink': nrm((L, N_HEADS), 0.5),
        'conv_w': nrm((L, CONV_W, LRU_WIDTH), CONV_W ** -0.5),
        'conv_b': nrm((L, LRU_WIDTH), 0.01),
        'lru_wr': nrm((L, 2, LRU_BLOCKS, LRU_BLOCK_W, LRU_BLOCK_W), LRU_BLOCK_W ** -0.5),
        'lru_br': nrm((L, 2, LRU_WIDTH), 0.01),
        'lru_wi': nrm((L, 2, LRU_BLOCKS, LRU_BLOCK_W, LRU_BLOCK_W), LRU_BLOCK_W ** -0.5),
        'lru_bi': nrm((L, 2, LRU_WIDTH), 0.01),
        'lru_lambda': jnp.log(root) - jnp.log1p(-root),
        'hgrn_lb_logits': nrm((L, 2, HG_K_W), 0.5),
        'hgrn_norm_g': gain((L, HG_V_W)),
        'w_out': nrm((L, MIX_W, D), MIX_W ** -0.5),
        'norm2_g': gain((L, D)),
        'router_w': nrm((L, D, N_EXPERTS), D ** -0.5),
        'router_b': nrm((L, N_EXPERTS), 0.01),
        'moe_w_gu': nrm((L, N_EXPERTS, D, 2 * D_FF), D ** -0.5),
        'moe_b_gu': nrm((L, N_EXPERTS, 2 * D_FF), 0.01),
        'moe_w_down': nrm((L, N_EXPERTS, D_FF, D), D_FF ** -0.5),
        'moe_b_down': nrm((L, N_EXPERTS, D), 0.01),
        'final_g': gain((D,)),
    }


def reference(x, c, ctx, c_ctx, ada_w, ada_b, norm1_g, w_in, attn_sink, conv_w, conv_b,
              lru_wr, lru_br, lru_wi, lru_bi, lru_lambda, hgrn_lb_logits, hgrn_norm_g, w_out,
              norm2_g, router_w, router_b, moe_w_gu, moe_b_gu, moe_w_down, moe_b_down, final_g):
    bsz, seq, d = x.shape
    ctx_len = ctx.shape[1]
    cos, sin = axial_rope_tables(seq, x.dtype)
    lb_p = jax.nn.softmax(hgrn_lb_logits.astype(jnp.float32), axis=0)
    lower_bounds = jnp.cumsum(lb_p, axis=0) - lb_p[0]
    cond_l = jax.nn.silu(c)
    cond_c = jax.nn.silu(c_ctx)
    cuts = np.cumsum(IN_SPLITS)[:-1].tolist()
    xl, xc = x, ctx
    for layer in range(DEPTH):
        ctx_out = layer < DEPTH - 1
        mod_l = jnp.split((cond_l @ ada_w[layer] + ada_b[layer])[:, None, :], 6, axis=-1)
        mod_c = jnp.split(cond_c @ ada_w[layer] + ada_b[layer], 6, axis=-1)
        hl = modulate(rms_norm(xl, norm1_g[layer]), mod_l[0], mod_l[1])
        hc = modulate(rms_norm(xc, norm1_g[layer]), mod_c[0], mod_c[1])
        pl = jnp.split(hl @ w_in[layer], cuts, axis=-1)
        pc = jnp.split(hc @ w_in[layer], cuts, axis=-1)
        att_l, att_c = window_attention(pl[0], pl[1], pl[2], pc[0], pc[1], pc[2],
                                        attn_sink[layer], cos, sin, ctx_out)
        lru_l, lru_c = rglru_mixer(pl[3], pl[4], pc[3], pc[4], conv_w[layer], conv_b[layer],
                                   lru_wr[layer], lru_br[layer], lru_wi[layer], lru_bi[layer],
                                   lru_lambda[layer], ctx_out)
        hg_l, hg_c = hgrn2_mixer(pl[5:], pc[5:], lower_bounds[layer], hgrn_norm_g[layer], ctx_out)
        xl = xl + mod_l[2] * (jnp.concatenate([att_l, lru_l, hg_l], axis=-1) @ w_out[layer])
        hl = modulate(rms_norm(xl, norm2_g[layer]), mod_l[3], mod_l[4])
        tokens = hl.reshape(bsz * seq, d)
        if ctx_out:
            xc = xc + mod_c[2] * (jnp.concatenate([att_c, lru_c, hg_c], axis=-1) @ w_out[layer])
            hc = modulate(rms_norm(xc, norm2_g[layer]), mod_c[3], mod_c[4])
            tokens = jnp.concatenate([tokens, hc.reshape(bsz * ctx_len, d)], axis=0)
        ffn = moe_ffn(tokens, router_w[layer], router_b[layer], moe_w_gu[layer], moe_b_gu[layer],
                      moe_w_down[layer], moe_b_down[layer])
        xl = xl + mod_l[5] * ffn[:bsz * seq].reshape(bsz, seq, d)
        if ctx_out:
            xc = xc + mod_c[5] * ffn[bsz * seq:].reshape(bsz, ctx_len, d)
    return rms_norm(xl, final_g)
```

```python
import functools

import jax
import jax.numpy as jnp
from jax import lax
from jax.experimental import pallas as pl
from jax.experimental.pallas import tpu as pltpu

F32 = jnp.float32
BF16 = jnp.bfloat16
I32 = jnp.int32

EPS = 1e-6
NEG_INF = -1e30
GRID_W = 64
ROPE_BASE = 10000.0
HEAD_DIM = 64
N_HEADS = 8
WINDOW = 128
LRU_C = 8.0
SQRT_FLOOR = 1e-12
HG_CHUNK = 32
N_EXPERTS = 32
TOP_K = 4
SWIGLU_LIMIT = 7.0
SWIGLU_ALPHA = 1.702

TM = 512
TSCAN = 256
TB = 256
LANES = 128
VMEM_LIMIT = 56 * 1024 * 1024

NT_DIMS = (((1,), (1,)), ((), ()))


def _cparams(*sem):
    return pltpu.CompilerParams(dimension_semantics=sem, vmem_limit_bytes=VMEM_LIMIT)


def _rms(x):
    return x * lax.rsqrt(jnp.mean(x * x, axis=-1, keepdims=True) + EPS)


def _sigmoid(x):
    return 1.0 / (1.0 + jnp.exp(-x))


def _adaln_body(c_ref, w_ref, b_ref, o_ref):
    c = c_ref[...]
    cond = c * _sigmoid(c)
    o_ref[...] = jnp.dot(cond, w_ref[...], preferred_element_type=F32,
                         precision=lax.Precision.HIGHEST) + b_ref[...]


def adaln(cond8, ada_w, ada_b):
    depth, d, n = ada_w.shape
    tn = 1536
    return pl.pallas_call(
        _adaln_body,
        grid=(depth, n // tn),
        in_specs=[pl.BlockSpec((8, d), lambda l, j: (0, 0)),
                  pl.BlockSpec((None, d, tn), lambda l, j: (l, 0, j)),
                  pl.BlockSpec((None, 1, tn), lambda l, j: (l, 0, j))],
        out_specs=pl.BlockSpec((None, 8, tn), lambda l, j: (l, 0, j)),
        out_shape=jax.ShapeDtypeStruct((depth, 8, n), F32),
        compiler_params=_cparams("parallel", "parallel"),
        name="adaln",
    )(cond8, ada_w, ada_b.reshape(depth, 1, n))


def _inproj_body(x_ref, g_ref, sh_ref, sc_ref, w_ref, cs_ref, sa_ref, sb_ref,
                 q_ref, kv_ref, lru_ref, hg_ref):
    h = _rms(x_ref[...]) * g_ref[...]
    h = h * (1.0 + sc_ref[...]) + sh_ref[...]
    p = jnp.dot(h.astype(BF16), w_ref[...], preferred_element_type=F32)
    cs, sa, sb = cs_ref[...], sa_ref[...], sb_ref[...]

    def rope(t):
        return t * cs + pltpu.roll(t, LANES - 16, 1) * sa + pltpu.roll(t, 16, 1) * sb

    for j in range(4):
        q_ref[:, j * LANES:(j + 1) * LANES] = (
            rope(p[:, j * LANES:(j + 1) * LANES]) * (HEAD_DIM ** -0.5)).astype(BF16)
    k = rope(p[:, 512:640])
    v = p[:, 640:768]
    lo = lax.broadcasted_iota(I32, k.shape, 1) < HEAD_DIM
    kr = pltpu.roll(k, HEAD_DIM, 1)
    vr = pltpu.roll(v, HEAD_DIM, 1)
    kv_ref[:, 0:128] = jnp.where(lo, k, kr).astype(BF16)
    kv_ref[:, 128:256] = jnp.where(lo, kr, k).astype(BF16)
    kv_ref[:, 256:384] = jnp.where(lo, v, vr).astype(BF16)
    kv_ref[:, 384:512] = jnp.where(lo, vr, v).astype(BF16)
    lru_ref[...] = p[:, 768:1280]
    hg_ref[...] = p[:, 1280:2560]


def inproj(xs, g, shift, scale, w_bf, rope_c, rope_a, rope_b, *, tiles_per_batch, n_seg):
    r, d = xs.shape
    n_tiles = r // TM
    n_lat = tiles_per_batch * (n_seg - 1)

    def seg(i):
        return (jnp.minimum(i // tiles_per_batch, n_seg - 1), 0, 0)

    def rope_idx(i):
        return (jnp.where(i < n_lat, i % tiles_per_batch, tiles_per_batch), 0)

    row = lambda w: pl.BlockSpec((TM, w), lambda i: (i, 0))
    return pl.pallas_call(
        _inproj_body,
        grid=(n_tiles,),
        in_specs=[row(d),
                  pl.BlockSpec((1, d), lambda i: (0, 0)),
                  pl.BlockSpec((None, 1, d), seg),
                  pl.BlockSpec((None, 1, d), seg),
                  pl.BlockSpec(w_bf.shape, lambda i: (0, 0)),
                  pl.BlockSpec((TM, LANES), rope_idx),
                  pl.BlockSpec((TM, LANES), rope_idx),
                  pl.BlockSpec((TM, LANES), rope_idx)],
        out_specs=[row(512), row(512), row(512), row(1280)],
        out_shape=[jax.ShapeDtypeStruct((r, 512), BF16),
                   jax.ShapeDtypeStruct((r, 512), BF16),
                   jax.ShapeDtypeStruct((r, 512), F32),
                   jax.ShapeDtypeStruct((r, 1280), F32)],
        compiler_params=_cparams("parallel"),
        name="inproj",
    )(xs, g, shift, scale, w_bf, rope_c, rope_a, rope_b)


def _softmax_pv(s_parts, v_parts, sink_col):
    m = sink_col
    for s in s_parts:
        m = jnp.maximum(m, jnp.max(s, axis=-1, keepdims=True))
    l = jnp.exp(sink_col - m)
    o = None
    for s, v in zip(s_parts, v_parts):
        p = jnp.exp(s - m)
        l = l + jnp.sum(p, axis=-1, keepdims=True)
        pv = jnp.dot(p.astype(BF16), v, preferred_element_type=F32)
        o = pv if o is None else o + pv
    return o / l


def _attn_body(sink_ref, q_ref, kvm_ref, kvp_ref, kvn_ref, kvc_ref, o_ref, *,
               tiles_per_batch, n_lat, ctx_len, n_batch):
    i = pl.program_id(0)
    blk = WINDOW
    nsub = TM // blk

    def stacked_q(q_sub, g):
        rows = q_sub.shape[0]
        lo = lax.broadcasted_iota(I32, (rows, LANES), 1) < HEAD_DIM
        parts = []
        for jq in (2 * g, 2 * g + 1):
            qg = q_sub[:, jq * LANES:(jq + 1) * LANES]
            parts.append(jnp.where(lo, qg, jnp.zeros_like(qg)))
            parts.append(jnp.where(lo, jnp.zeros_like(qg), qg))
        return jnp.concatenate(parts, axis=0)

    def sink_column(g, rows):
        blk_id = lax.broadcasted_iota(I32, (4 * rows, 1), 0) // rows
        col = jnp.full((4 * rows, 1), sink_ref[4 * g + 3], F32)
        for b in range(3):
            col = jnp.where(blk_id == b, sink_ref[4 * g + b], col)
        return col

    def write_heads(o, g, rows, row0):
        lo = lax.broadcasted_iota(I32, (rows, LANES), 1) < HEAD_DIM
        for t, jq in enumerate((2 * g, 2 * g + 1)):
            o_e0 = o[(2 * t) * rows:(2 * t + 1) * rows]
            o_e1 = o[(2 * t + 1) * rows:(2 * t + 2) * rows]
            o_ref[row0:row0 + rows, jq * LANES:(jq + 1) * LANES] = (
                jnp.where(lo, o_e0, o_e1).astype(BF16))

    @pl.when(i < n_lat)
    def _latent():
        j = i % tiles_per_batch
        r_io = lax.broadcasted_iota(I32, (4 * blk, blk), 0) % blk
        c_io = lax.broadcasted_iota(I32, (4 * blk, blk), 1)
        for sb in range(nsub):
            q_sub = q_ref[sb * blk:(sb + 1) * blk, :]
            has_prev = jnp.logical_or(j > 0, sb > 0)
            has_next = jnp.logical_or(j < tiles_per_batch - 1, sb < nsub - 1)
            m_prev = jnp.logical_and(c_io >= r_io, has_prev)
            m_next = jnp.logical_and(c_io <= r_io, has_next)
            for g in range(2):
                kc, vc = g * LANES, (2 + g) * LANES

                def kv_block(b, col):
                    if b < 0:
                        return kvp_ref[:, col:col + LANES]
                    if b >= nsub:
                        return kvn_ref[:, col:col + LANES]
                    return kvm_ref[b * blk:(b + 1) * blk, col:col + LANES]

                qs = stacked_q(q_sub, g)
                s_prev = lax.dot_general(qs, kv_block(sb - 1, kc), NT_DIMS, preferred_element_type=F32)
                s_own = lax.dot_general(qs, kv_block(sb, kc), NT_DIMS, preferred_element_type=F32)
                s_next = lax.dot_general(qs, kv_block(sb + 1, kc), NT_DIMS, preferred_element_type=F32)
                s_ctx = lax.dot_general(qs, kvc_ref[:, kc:kc + LANES], NT_DIMS, preferred_element_type=F32)
                s_prev = jnp.where(m_prev, s_prev, NEG_INF)
                s_next = jnp.where(m_next, s_next, NEG_INF)
                o = _softmax_pv(
                    [s_prev, s_own, s_next, s_ctx],
                    [kv_block(sb - 1, vc), kv_block(sb, vc), kv_block(sb + 1, vc),
                     kvc_ref[:, vc:vc + LANES]],
                    sink_column(g, blk))
                write_heads(o, g, blk, sb * blk)

    @pl.when(i >= n_lat)
    def _context():
        for bb in range(n_batch):
            r0 = bb * ctx_len
            q_sub = q_ref[r0:r0 + ctx_len, :]
            for g in range(2):
                kc, vc = g * LANES, (2 + g) * LANES
                qs = stacked_q(q_sub, g)
                s = lax.dot_general(qs, kvm_ref[r0:r0 + ctx_len, kc:kc + LANES], NT_DIMS,
                                    preferred_element_type=F32)
                o = _softmax_pv([s], [kvm_ref[r0:r0 + ctx_len, vc:vc + LANES]],
                                sink_column(g, ctx_len))
                write_heads(o, g, ctx_len, r0)


def attention(q, kv, sink, *, tiles_per_batch, n_batch, ctx_len):
    r = q.shape[0]
    n_tiles = r // TM
    n_lat = tiles_per_batch * n_batch
    assert n_batch * ctx_len == TM and n_tiles == n_lat + 1
    sub = TM // WINDOW
    n_blk128 = r // WINDOW
    ctx_blk0 = (n_lat * TM) // ctx_len

    def prev_idx(i, s):
        return (jnp.maximum(i * sub - 1, 0), 0)

    def next_idx(i, s):
        return (jnp.minimum(i * sub + sub, n_blk128 - 1), 0)

    def ctx_idx(i, s):
        return (ctx_blk0 + jnp.minimum(i // tiles_per_batch, n_batch - 1), 0)

    body = functools.partial(_attn_body, tiles_per_batch=tiles_per_batch, n_lat=n_lat,
                             ctx_len=ctx_len, n_batch=n_batch)
    return pl.pallas_call(
        body,
        grid_spec=pltpu.PrefetchScalarGridSpec(
            num_scalar_prefetch=1,
            grid=(n_tiles,),
            in_specs=[pl.BlockSpec((TM, 512), lambda i, s: (i, 0)),
                      pl.BlockSpec((TM, 512), lambda i, s: (i, 0)),
                      pl.BlockSpec((WINDOW, 512), prev_idx),
                      pl.BlockSpec((WINDOW, 512), next_idx),
                      pl.BlockSpec((ctx_len, 512), ctx_idx)],
            out_specs=pl.BlockSpec((TM, 512), lambda i, s: (i, 0))),
        out_shape=jax.ShapeDtypeStruct((r, 512), BF16),
        compiler_params=_cparams("parallel"),
        name="attention",
    )(sink, q, kv, kv, kv, kv)


def _scan_maps(n_batch, seq, ctx_len):
    assert ctx_len == TSCAN
    nl = seq // TSCAN
    ctx0 = (n_batch * seq) // TSCAN

    def fwd(b, c):
        return jnp.where(c == 0, ctx0 + b, b * nl + c - 1)

    def bwd(b, c):
        return jnp.where(c == 0, ctx0 + b, b * nl + nl - c)

    return nl, fwd, bwd


def _lru_dir(x_ref, xp_ref, xn_ref, has_prev, has_next, cw_ref, cb_ref, w_ref, br_ref, bi_ref,
             lam_ref, h_ref, a_scr, b_scr, out_ref, reverse):
    t = TSCAN
    xp = jnp.where(has_prev, xp_ref[...], 0.0)
    xn = jnp.where(has_next, xn_ref[...], 0.0)
    xpad = jnp.concatenate([xp, x_ref[...], xn], axis=0)
    n = t + 16
    cw = cw_ref[...]
    u = cb_ref[...] + jnp.zeros((t, xpad.shape[1]), F32)
    for j in range(4):
        sh = (2 - j) % n
        rolled = xpad if sh == 0 else pltpu.roll(xpad, sh, 0)
        u = u + rolled[8:8 + t] * cw[j:j + 1, :]
    gates = jnp.dot(u.astype(BF16), w_ref[...], preferred_element_type=F32)
    c = u.shape[1]
    r = _sigmoid(gates[:, :c] + br_ref[...])
    ig = _sigmoid(gates[:, c:] + bi_ref[...])
    nl = -lam_ref[...]
    softplus = jnp.maximum(nl, 0.0) + jnp.log1p(jnp.exp(-jnp.abs(nl)))
    log_a = (-LRU_C) * r * softplus
    a = jnp.exp(log_a)
    mult = jnp.sqrt(jnp.maximum(1.0 - a * a, SQRT_FLOOR))
    bv = mult * (ig * u)

    g8 = t // 8
    a3 = a.reshape(g8, 8, c)
    b3 = bv.reshape(g8, 8, c)
    r8 = lax.broadcasted_iota(I32, (g8, 8, c), 1)
    for s in (1, 2, 4):
        if reverse:
            a_sh, b_sh, ok = pltpu.roll(a3, 8 - s, 1), pltpu.roll(b3, 8 - s, 1), r8 < 8 - s
        else:
            a_sh, b_sh, ok = pltpu.roll(a3, s, 1), pltpu.roll(b3, s, 1), r8 >= s
        b3 = jnp.where(ok, a3 * b_sh + b3, b3)
        a3 = jnp.where(ok, a3 * a_sh, a3)
    a_scr[...] = a3.reshape(t, c)
    b_scr[...] = b3.reshape(t, c)
    edge = 0 if reverse else 7

    def group(gi, carry):
        g = (g8 - 1 - gi) if reverse else gi
        rows = pl.ds(pl.multiple_of(g * 8, 8), 8)
        hg = b_scr[rows, :] + a_scr[rows, :] * carry
        out_ref[rows, :] = hg
        return jnp.broadcast_to(hg[edge:edge + 1, :], hg.shape)

    h_ref[...] = lax.fori_loop(0, g8, group, h_ref[...], unroll=4)


def _lru_body(xf_ref, xfp_ref, xfn_ref, xb_ref, xbp_ref, xbn_ref, cw_ref, cb_ref, w_ref, br_ref,
              bi_ref, lam_ref, hf_ref, hb_ref, st_ref, a_scr, b_scr, *, nl):
    c = pl.program_id(1)

    @pl.when(c == 0)
    def _init():
        st_ref[...] = jnp.zeros_like(st_ref)

    lat = c > 0
    _lru_dir(xf_ref, xfp_ref, xfn_ref, jnp.logical_and(lat, c > 1), jnp.logical_and(lat, c < nl),
             cw_ref, cb_ref, w_ref.at[0], br_ref.at[0], bi_ref.at[0], lam_ref.at[0],
             st_ref.at[0], a_scr, b_scr, hf_ref, False)
    _lru_dir(xb_ref, xbp_ref, xbn_ref, jnp.logical_and(lat, c < nl), jnp.logical_and(lat, c > 1),
             cw_ref, cb_ref, w_ref.at[1], br_ref.at[1], bi_ref.at[1], lam_ref.at[1],
             st_ref.at[1], a_scr, b_scr, hb_ref, True)


def lru_scan(lru, conv_w, conv_b, w_gates, br, bi, lam, *, n_batch, seq, ctx_len):
    r = lru.shape[0]
    c = lru.shape[1] // 2
    nl, fwd, bwd = _scan_maps(n_batch, seq, ctx_len)
    per = TSCAN // 8
    n8 = r // 8

    def main(m):
        return pl.BlockSpec((TSCAN, c), lambda b, s: (m(b, s), 0))

    def prev(m):
        return pl.BlockSpec((8, c), lambda b, s: (jnp.maximum(m(b, s) * per - 1, 0), 0))

    def nxt(m):
        return pl.BlockSpec((8, c), lambda b, s: (jnp.minimum(m(b, s) * per + per, n8 - 1), 0))

    full = lambda a: pl.BlockSpec(a.shape, lambda b, s: (0,) * a.ndim)
    return pl.pallas_call(
        functools.partial(_lru_body, nl=nl),
        grid=(n_batch, nl + 1),
        in_specs=[main(fwd), prev(fwd), nxt(fwd), main(bwd), prev(bwd), nxt(bwd),
                  full(conv_w), full(conv_b), full(w_gates), full(br), full(bi), full(lam)],
        out_specs=[main(fwd), main(bwd)],
        out_shape=[jax.ShapeDtypeStruct((r, c), F32)] * 2,
        scratch_shapes=[pltpu.VMEM((2, 8, c), F32), pltpu.VMEM((TSCAN, c), F32),
                        pltpu.VMEM((TSCAN, c), F32)],
        compiler_params=_cparams("parallel", "arbitrary"),
        name="lru_scan",
    )(lru, lru, lru, lru, lru, lru, conv_w, conv_b, w_gates, br, bi, lam)


def _hgrn_dir(q_ref, z_ref, v_ref, lb_ref, st_ref, out_ref, reverse):
    t = TSCAN
    nch = t // HG_CHUNK
    q = q_ref[...]
    z = z_ref[...]
    v = v_ref[...]
    lb = lb_ref[...]
    w = q.shape[1]
    log_f = jnp.log(lb + (1.0 - lb) * _sigmoid(z))
    k = (1.0 - lb) * _sigmoid(-z)

    rc = lax.broadcasted_iota(I32, (t, w), 0) % HG_CHUNK
    b = log_f
    s = 1
    while s < HG_CHUNK:
        if reverse:
            b = b + jnp.where(rc < HG_CHUNK - s, pltpu.roll(b, t - s, 0), 0.0)
        else:
            b = b + jnp.where(rc >= s, pltpu.roll(b, s, 0), 0.0)
        s *= 2
    tot = jnp.sum(log_f.reshape(nch, HG_CHUNK, w), axis=1, keepdims=True)
    b_last = jnp.broadcast_to(tot, (nch, HG_CHUNK, w)).reshape(t, w)
    b_half = 0.5 * b_last
    q_i = (q * jnp.exp(b - b_half)).astype(BF16)
    k_i = (k * jnp.exp(b_half - b)).astype(BF16)
    q_s = (q * jnp.exp(b)).astype(BF16)
    k_e = (k * jnp.exp(b_last - b)).astype(BF16)
    decay = jnp.exp(tot.reshape(nch, w))
    v_bf = v.astype(BF16)

    ri = lax.broadcasted_iota(I32, (t, t), 0)
    ci = lax.broadcasted_iota(I32, (t, t), 1)
    same = (ri // HG_CHUNK) == (ci // HG_CHUNK)
    causal = jnp.logical_and(same, (ci >= ri) if reverse else (ci <= ri))
    lo = lax.broadcasted_iota(I32, (t, LANES), 1) < HEAD_DIM
    chunk_of_row = lax.broadcasted_iota(I32, (t, LANES), 0) // HG_CHUNK
    bd_r = lax.broadcasted_iota(I32, (LANES, LANES), 0) // HEAD_DIM
    bd_c = lax.broadcasted_iota(I32, (LANES, LANES), 1) // HEAD_DIM
    block_diag = bd_r == bd_c

    for p in range(w // LANES):
        cols = slice(p * LANES, (p + 1) * LANES)
        qi_p, ki_p, qs_p, ke_p, v_p = q_i[:, cols], k_i[:, cols], q_s[:, cols], k_e[:, cols], v_bf[:, cols]
        o_half = []
        for e in range(2):
            qm = jnp.where(lo if e == 0 else jnp.logical_not(lo), qi_p, jnp.zeros_like(qi_p))
            att = lax.dot_general(qm, ki_p, NT_DIMS, preferred_element_type=F32)
            att = jnp.where(causal, att, 0.0).astype(BF16)
            o_half.append(jnp.dot(att, v_p, preferred_element_type=F32))
        out_ref[:, cols] = jnp.where(lo, o_half[0], o_half[1])

        v_t = v[:, cols].T.astype(BF16)
        st = st_ref[p]
        order = range(nch - 1, -1, -1) if reverse else range(nch)
        for n in order:
            rows = slice(n * HG_CHUNK, (n + 1) * HG_CHUNK)
            o_inter = lax.dot_general(qs_p[rows], st.astype(BF16), NT_DIMS, preferred_element_type=F32)
            out_ref[rows, cols] = out_ref[rows, cols] + o_inter
            ke_n = jnp.where(chunk_of_row == n, ke_p, jnp.zeros_like(ke_p))
            kv_t = jnp.dot(v_t, ke_n, preferred_element_type=F32)
            st = decay[n:n + 1, cols] * st + jnp.where(block_diag, kv_t, 0.0)
        st_ref[p] = st


def _hgrn_body(qf_ref, zf_ref, vf_ref, qb_ref, zb_ref, vb_ref, lb_ref, of_ref, ob_ref, st_ref):
    c = pl.program_id(1)

    @pl.when(c == 0)
    def _init():
        st_ref[...] = jnp.zeros_like(st_ref)

    _hgrn_dir(qf_ref, zf_ref, vf_ref, lb_ref.at[0], st_ref.at[0], of_ref, False)
    _hgrn_dir(qb_ref, zb_ref, vb_ref, lb_ref.at[1], st_ref.at[1], ob_ref, True)


def hgrn_scan(hg, lower_bounds, *, n_batch, seq, ctx_len):
    r = hg.shape[0]
    w = hg.shape[1] // 5
    nl, fwd, bwd = _scan_maps(n_batch, seq, ctx_len)

    def col(m, j):
        return pl.BlockSpec((TSCAN, w), lambda b, s: (m(b, s), j))

    lb3 = lower_bounds.reshape(2, 1, w)
    return pl.pallas_call(
        _hgrn_body,
        grid=(n_batch, nl + 1),
        in_specs=[col(fwd, 0), col(fwd, 1), col(fwd, 3), col(bwd, 0), col(bwd, 2), col(bwd, 3),
                  pl.BlockSpec(lb3.shape, lambda b, s: (0, 0, 0))],
        out_specs=[col(fwd, 0), col(bwd, 0)],
        out_shape=[jax.ShapeDtypeStruct((r, w), F32)] * 2,
        scratch_shapes=[pltpu.VMEM((2, w // LANES, LANES, LANES), F32)],
        compiler_params=_cparams("parallel", "arbitrary"),
        name="hgrn_scan",
    )(hg, hg, hg, hg, hg, hg, lb3)


def _outproj_router_body(x_ref, att_ref, hf_ref, hb_ref, lg_ref, of_ref, ob_ref, gg_ref, hn_ref,
                         wo_ref, gm_ref, sh_ref, sc_ref, n2_ref, rw_ref, rb_ref,
                         xo_ref, h2_ref, idx_ref, gate_ref, rank_ref, cnt_ref, carry_ref):
    i = pl.program_id(0)

    @pl.when(i == 0)
    def _init():
        carry_ref[...] = jnp.zeros_like(carry_ref)

    lru_y = (hf_ref[...] + hb_ref[...]) * jax.nn.gelu(lg_ref[...], approximate=True)
    o = of_ref[...] + ob_ref[...]
    w = o.shape[1]
    gr = lax.broadcasted_iota(I32, (w, w), 0) // HEAD_DIM
    gc = lax.broadcasted_iota(I32, (w, w), 1) // HEAD_DIM
    head_mean = jnp.where(gr == gc, 1.0 / HEAD_DIM, 0.0).astype(F32)
    ms = jnp.dot(o * o, head_mean, preferred_element_type=F32, precision=lax.Precision.HIGHEST)
    gg = gg_ref[...]
    hg_y = o * lax.rsqrt(ms + EPS) * hn_ref[...] * (gg * _sigmoid(gg))
    y = jnp.dot(att_ref[...], wo_ref[0:512, :], preferred_element_type=F32)
    y = y + jnp.dot(lru_y.astype(BF16), wo_ref[512:768, :], preferred_element_type=F32)
    y = y + jnp.dot(hg_y.astype(BF16), wo_ref[768:1024, :], preferred_element_type=F32)
    x = x_ref[...] + gm_ref[...] * y
    xo_ref[...] = x
    h2 = _rms(x) * n2_ref[...]
    h2 = h2 * (1.0 + sc_ref[...]) + sh_ref[...]
    h2_ref[...] = h2

    logits = lax.dot_general(rw_ref[...], h2, NT_DIMS, preferred_element_type=F32,
                             precision=lax.Precision.HIGHEST) + rb_ref[...]
    ne, tm = logits.shape
    e_io = lax.broadcasted_iota(I32, (ne, tm), 0)
    work = logits
    vals, hots = [], []
    for k in range(TOP_K):
        m = jnp.max(work, axis=0, keepdims=True)
        idx = jnp.min(jnp.where(work == m, e_io, ne), axis=0, keepdims=True)
        hot = e_io == idx
        vals.append(m)
        hots.append(hot)
        idx_ref[k:k + 1, :] = idx
        work = jnp.where(hot, -jnp.inf, work)
    exps = [jnp.exp(v - vals[0]) for v in vals]
    denom = exps[0] + exps[1] + exps[2] + exps[3]
    for k in range(TOP_K):
        gate_ref[k:k + 1, :] = exps[k] / denom
    z4 = jnp.zeros((8 - TOP_K, tm), F32)
    gate_ref[TOP_K:8, :] = z4
    idx_ref[TOP_K:8, :] = z4.astype(I32)
    rank_ref[TOP_K:8, :] = z4.astype(I32)

    chosen = jnp.logical_or(jnp.logical_or(hots[0], hots[1]), jnp.logical_or(hots[2], hots[3]))
    sr = lax.broadcasted_iota(I32, (tm, tm), 0)
    sc = lax.broadcasted_iota(I32, (tm, tm), 1)
    before = jnp.where(sr < sc, 1.0, 0.0).astype(BF16)
    chosen_f = jnp.where(chosen, 1.0, 0.0)
    prefix = jnp.dot(chosen_f.astype(BF16), before, preferred_element_type=F32) + carry_ref[:, 0:1]
    for k in range(TOP_K):
        rk = jnp.sum(jnp.where(hots[k], prefix, 0.0), axis=0, keepdims=True)
        rank_ref[k:k + 1, :] = rk.astype(I32)
    carry_ref[...] = carry_ref[...] + jnp.sum(chosen_f, axis=1, keepdims=True)
    cnt_ref[...] = carry_ref[...]


def outproj_router(xs, att, hf, hb, lru, of, ob, hg, hn_g, wo_bf, gate_msa, shift_mlp, scale_mlp,
                   n2_g, rw_t, rb, *, tiles_per_batch, n_seg):
    r, d = xs.shape
    n_tiles = r // TM

    def seg(i):
        return (jnp.minimum(i // tiles_per_batch, n_seg - 1), 0, 0)

    row = lambda w, j=0: pl.BlockSpec((TM, w), lambda i: (i, j))
    full = lambda a: pl.BlockSpec(a.shape, lambda i: (0,) * a.ndim)
    modspec = pl.BlockSpec((None, 1, d), seg)
    lane_out = pl.BlockSpec((8, TM), lambda i: (0, i))
    return pl.pallas_call(
        _outproj_router_body,
        grid=(n_tiles,),
        in_specs=[row(d), row(512), row(256), row(256), row(256, 1), row(256), row(256),
                  row(256, 4), full(hn_g), full(wo_bf), modspec, modspec, modspec, full(n2_g),
                  full(rw_t), full(rb)],
        out_specs=[row(d), row(d), lane_out, lane_out, lane_out,
                   pl.BlockSpec((N_EXPERTS, LANES), lambda i: (0, 0))],
        out_shape=[jax.ShapeDtypeStruct((r, d), F32), jax.ShapeDtypeStruct((r, d), F32),
                   jax.ShapeDtypeStruct((8, r), I32), jax.ShapeDtypeStruct((8, r), F32),
                   jax.ShapeDtypeStruct((8, r), I32),
                   jax.ShapeDtypeStruct((N_EXPERTS, LANES), F32)],
        scratch_shapes=[pltpu.VMEM((N_EXPERTS, LANES), F32)],
        compiler_params=_cparams("arbitrary"),
        name="outproj_router",
    )(xs, att, hf, hb, lru, of, ob, hg, hn_g, wo_bf, gate_msa, shift_mlp, scale_mlp, n2_g, rw_t, rb)


def _row_copy(src, s_row, dst, d_row, sem):
    return pltpu.make_async_copy(src.at[pl.ds(s_row, 1), :], dst.at[pl.ds(d_row, 1), :], sem)


def _dispatch_body(dest_ref, h_ref, xs_ref, sem):
    n = TOP_K * TM

    def issue(a, carry):
        _row_copy(h_ref, a % TM, xs_ref, dest_ref[0, 0, a], sem).start()
        return carry

    lax.fori_loop(0, n, issue, 0, unroll=8)

    def drain(a, carry):
        _row_copy(h_ref, 0, xs_ref, 0, sem).wait()
        return carry

    lax.fori_loop(0, n, drain, 0, unroll=8)


def moe_dispatch(h2, dest_tiles, p_rows):
    r, d = h2.shape
    n_tiles = r // TM
    return pl.pallas_call(
        _dispatch_body,
        grid=(n_tiles,),
        in_specs=[pl.BlockSpec((1, 1, TOP_K * TM), lambda i: (i, 0, 0), memory_space=pltpu.SMEM),
                  pl.BlockSpec((TM, d), lambda i: (i, 0))],
        out_specs=pl.BlockSpec(memory_space=pl.ANY),
        out_shape=jax.ShapeDtypeStruct((p_rows, d), F32),
        scratch_shapes=[pltpu.SemaphoreType.DMA(())],
        compiler_params=_cparams("arbitrary"),
        name="moe_dispatch",
    )(dest_tiles, h2)


def _expert_body(be_ref, nu_ref, x_ref, wgu_ref, bgu_ref, wd_ref, bd_ref, y_ref):
    @pl.when(pl.program_id(0) < nu_ref[0])
    def _run():
        dff = wd_ref.shape[0]
        gu = jnp.dot(x_ref[...].astype(BF16), wgu_ref[...], preferred_element_type=F32) + bgu_ref[...]
        gate = jnp.minimum(gu[:, :dff], SWIGLU_LIMIT)
        up = jnp.clip(gu[:, dff:], -SWIGLU_LIMIT, SWIGLU_LIMIT)
        glu = gate * _sigmoid(SWIGLU_ALPHA * gate)
        act = ((up + 1.0) * glu).astype(BF16)
        y_ref[...] = jnp.dot(act, wd_ref[...], preferred_element_type=F32) + bd_ref[...]


def moe_experts(xs_sorted, blk_e, n_used, wgu_bf, bgu, wd_bf, bd):
    p_rows, d = xs_sorted.shape
    ne, _, dgu = wgu_bf.shape
    dff = wd_bf.shape[1]
    return pl.pallas_call(
        _expert_body,
        grid_spec=pltpu.PrefetchScalarGridSpec(
            num_scalar_prefetch=2,
            grid=(p_rows // TB,),
            in_specs=[pl.BlockSpec((TB, d), lambda i, be, nu: (i, 0)),
                      pl.BlockSpec((None, d, dgu), lambda i, be, nu: (be[i], 0, 0)),
                      pl.BlockSpec((None, 1, dgu), lambda i, be, nu: (be[i], 0, 0)),
                      pl.BlockSpec((None, dff, d), lambda i, be, nu: (be[i], 0, 0)),
                      pl.BlockSpec((None, 1, d), lambda i, be, nu: (be[i], 0, 0))],
            out_specs=pl.BlockSpec((TB, d), lambda i, be, nu: (i, 0))),
        out_shape=jax.ShapeDtypeStruct((p_rows, d), F32),
        compiler_params=_cparams("arbitrary"),
        name="moe_experts",
    )(blk_e, n_used, xs_sorted, wgu_bf, bgu.reshape(ne, 1, dgu), wd_bf, bd.reshape(ne, 1, d))


def _combine_body(dest_ref, x_ref, gate_ref, gm_ref, fg_ref, ys_ref, o_ref, buf, sem, *, final):
    n = TOP_K * TM

    def issue(a, carry):
        _row_copy(ys_ref, dest_ref[0, 0, a], buf, a, sem).start()
        return carry

    lax.fori_loop(0, n, issue, 0, unroll=8)

    def drain(a, carry):
        _row_copy(ys_ref, 0, buf, 0, sem).wait()
        return carry

    lax.fori_loop(0, n, drain, 0, unroll=8)
    gt = gate_ref[...].T
    acc = buf[0:TM, :] * gt[:, 0:1]
    for k in range(1, TOP_K):
        acc = acc + buf[k * TM:(k + 1) * TM, :] * gt[:, k:k + 1]
    x = x_ref[...] + gm_ref[...] * acc
    if final:
        x = _rms(x) * fg_ref[...]
    o_ref[...] = x


def moe_combine(xs, gates, dest_tiles, ys, gate_mlp, final_g, *, tiles_per_batch, n_seg, n_tiles, final):
    d = xs.shape[1]

    def seg(i):
        return (jnp.minimum(i // tiles_per_batch, n_seg - 1), 0, 0)

    return pl.pallas_call(
        functools.partial(_combine_body, final=final),
        grid=(n_tiles,),
        in_specs=[pl.BlockSpec((1, 1, TOP_K * TM), lambda i: (i, 0, 0), memory_space=pltpu.SMEM),
                  pl.BlockSpec((TM, d), lambda i: (i, 0)),
                  pl.BlockSpec((8, TM), lambda i: (0, i)),
                  pl.BlockSpec((None, 1, d), seg),
                  pl.BlockSpec((1, d), lambda i: (0, 0)),
                  pl.BlockSpec(memory_space=pl.ANY)],
        out_specs=pl.BlockSpec((TM, d), lambda i: (i, 0)),
        out_shape=jax.ShapeDtypeStruct((n_tiles * TM, d), F32),
        scratch_shapes=[pltpu.VMEM((TOP_K * TM, d), F32), pltpu.SemaphoreType.DMA(())],
        compiler_params=_cparams("arbitrary"),
        name="moe_combine",
    )(dest_tiles, xs, gates, gate_mlp, final_g, ys)


def _rope_tables(seq):
    nf = HEAD_DIM // 4
    pos = jnp.arange(seq)
    rows = (pos // GRID_W).astype(F32)
    cols = (pos % GRID_W).astype(F32)
    inv_freq = ROPE_BASE ** (-jnp.arange(nf, dtype=F32) / nf)
    d = jnp.arange(LANES) % HEAD_DIM
    axis = d // (2 * nf)
    half = (d // nf) % 2
    f = d % nf
    ang = jnp.where(axis[None, :] == 0, rows[:, None], cols[:, None]) * inv_freq[f][None, :]
    cs, sn = jnp.cos(ang), jnp.sin(ang)
    ca = jnp.where(half[None, :] == 0, -sn, 0.0)
    cb = jnp.where(half[None, :] == 1, sn, 0.0)
    pad1 = jnp.ones((TM, LANES), F32)
    pad0 = jnp.zeros((TM, LANES), F32)
    return (jnp.concatenate([cs, pad1]), jnp.concatenate([ca, pad0]), jnp.concatenate([cb, pad0]))


def _block_diag(w):
    n, c, _ = w.shape
    eye = jnp.eye(n, dtype=w.dtype)
    return (eye[:, None, :, None] * w[:, :, None, :]).reshape(n * c, n * c)


def _moe_layout(idx, rank, counts, n_blocks):
    cnt = counts.astype(I32)
    padded = (cnt + TB - 1) // TB * TB
    pends = jnp.cumsum(padded)
    pstarts = pends - padded
    dest = jnp.take(pstarts, idx[:TOP_K], axis=0) + rank[:TOP_K]
    r = dest.shape[1]
    dest_tiles = dest.reshape(TOP_K, r // TM, TM).transpose(1, 0, 2).reshape(r // TM, 1, TOP_K * TM)
    blk_start = jnp.arange(n_blocks, dtype=I32) * TB
    blk_e = jnp.minimum(jnp.searchsorted(pends, blk_start, side="right"), N_EXPERTS - 1).astype(I32)
    n_used = (pends[-1:] // TB).astype(I32)
    return dest_tiles, blk_e, n_used


def kernel(x, c, ctx, c_ctx, ada_w, ada_b, norm1_g, w_in, attn_sink, conv_w, conv_b, lru_wr, lru_br,
           lru_wi, lru_bi, lru_lambda, hgrn_lb_logits, hgrn_norm_g, w_out, norm2_g, router_w,
           router_b, moe_w_gu, moe_b_gu, moe_w_down, moe_b_down, final_g):
    n_batch, seq, d = x.shape
    ctx_len = ctx.shape[1]
    depth = ada_w.shape[0]
    assert n_batch * ctx_len == TM and seq % TM == 0 and ctx_len == TSCAN
    tiles_per_batch = seq // TM
    n_seg = n_batch + 1
    n_lat = n_batch * tiles_per_batch
    r = n_batch * seq + n_batch * ctx_len
    n_blocks = (r * TOP_K) // TB + N_EXPERTS
    p_rows = n_blocks * TB

    xs = jnp.concatenate([x.reshape(n_batch * seq, d), ctx.reshape(n_batch * ctx_len, d)], axis=0)
    cond8 = jnp.zeros((8, d), F32).at[:n_batch].set(c).at[n_batch].set(c_ctx)
    mods = adaln(cond8, ada_w, ada_b)
    rope_c, rope_a, rope_b = _rope_tables(seq)
    lb_p = jax.nn.softmax(hgrn_lb_logits.astype(F32), axis=0)
    lower_bounds = jnp.cumsum(lb_p, axis=0) - lb_p[0]

    out = None
    for layer in range(depth):
        mod = [mods[layer, :, j * d:(j + 1) * d].reshape(8, 1, d) for j in range(6)]
        q, kv, lru, hg = inproj(xs, norm1_g[layer].reshape(1, d), mod[0], mod[1],
                                w_in[layer].astype(BF16), rope_c, rope_a, rope_b,
                                tiles_per_batch=tiles_per_batch, n_seg=n_seg)
        att = attention(q, kv, attn_sink[layer], tiles_per_batch=tiles_per_batch,
                        n_batch=n_batch, ctx_len=ctx_len)
        w_gates = jnp.stack([jnp.concatenate([_block_diag(lru_wr[layer, dd]),
                                              _block_diag(lru_wi[layer, dd])], axis=1)
                             for dd in range(2)]).astype(BF16)
        cw = lru_br.shape[-1]
        hf, hb = lru_scan(lru, conv_w[layer], conv_b[layer].reshape(1, cw), w_gates,
                          lru_br[layer].reshape(2, 1, cw), lru_bi[layer].reshape(2, 1, cw),
                          lru_lambda[layer].reshape(2, 1, cw),
                          n_batch=n_batch, seq=seq, ctx_len=ctx_len)
        of, ob = hgrn_scan(hg, lower_bounds[layer], n_batch=n_batch, seq=seq, ctx_len=ctx_len)
        xs, h2, idx, gates, rank, counts = outproj_router(
            xs, att, hf, hb, lru, of, ob, hg, hgrn_norm_g[layer].reshape(1, -1),
            w_out[layer].astype(BF16), mod[2], mod[3], mod[4], norm2_g[layer].reshape(1, d),
            router_w[layer].T, router_b[layer].reshape(N_EXPERTS, 1),
            tiles_per_batch=tiles_per_batch, n_seg=n_seg)
        dest_tiles, blk_e, n_used = _moe_layout(idx, rank, counts[:, 0], n_blocks)
        xs_sorted = moe_dispatch(h2, dest_tiles, p_rows)
        ys = moe_experts(xs_sorted, blk_e, n_used, moe_w_gu[layer].astype(BF16), moe_b_gu[layer],
                         moe_w_down[layer].astype(BF16), moe_b_down[layer])
        final = layer == depth - 1
        res = moe_combine(xs, gates, dest_tiles, ys, mod[5], final_g.reshape(1, d),
                          tiles_per_batch=tiles_per_batch, n_seg=n_seg,
                          n_tiles=n_lat if final else n_lat + 1, final=final)
        if final:
            out = res.reshape(n_batch, seq, d)
        else:
            xs = res
    return out
```

```python
import functools

import jax
import jax.numpy as jnp
from jax import lax
from jax.experimental import pallas as pl
from jax.experimental.pallas import tpu as pltpu

F32 = jnp.float32
BF16 = jnp.bfloat16
I32 = jnp.int32

EPS = 1e-6
NEG_INF = -1e30
GRID_W = 64
ROPE_BASE = 10000.0
HEAD_DIM = 64
N_HEADS = 8
WINDOW = 128
LRU_C = 8.0
SQRT_FLOOR = 1e-12
HG_CHUNK = 32
N_EXPERTS = 32
TOP_K = 4
SWIGLU_LIMIT = 7.0
SWIGLU_ALPHA = 1.702

TM = 512
TSCAN = 256
TB = 256
RUN_BITS = 7
TAIL_BITS = 5
ZR = TOP_K * 512 + N_EXPERTS * 8
LANES = 128
VMEM_LIMIT = 56 * 1024 * 1024

NT_DIMS = (((1,), (1,)), ((), ()))


def _cparams(*sem):
    return pltpu.CompilerParams(dimension_semantics=sem, vmem_limit_bytes=VMEM_LIMIT)


def _rms(x):
    return x * lax.rsqrt(jnp.mean(x * x, axis=-1, keepdims=True) + EPS)


def _sigmoid(x):
    return 1.0 / (1.0 + jnp.exp(-x))


def _adaln_body(c_ref, w_ref, b_ref, o_ref):
    c = c_ref[...]
    cond = c * _sigmoid(c)
    o_ref[...] = jnp.dot(cond, w_ref[...], preferred_element_type=F32,
                         precision=lax.Precision.HIGHEST) + b_ref[...]


def adaln(cond8, ada_w, ada_b):
    depth, d, n = ada_w.shape
    tn = 1536
    return pl.pallas_call(
        _adaln_body,
        grid=(depth, n // tn),
        in_specs=[pl.BlockSpec((8, d), lambda l, j: (0, 0)),
                  pl.BlockSpec((None, d, tn), lambda l, j: (l, 0, j)),
                  pl.BlockSpec((None, 1, tn), lambda l, j: (l, 0, j))],
        out_specs=pl.BlockSpec((None, 8, tn), lambda l, j: (l, 0, j)),
        out_shape=jax.ShapeDtypeStruct((depth, 8, n), F32),
        compiler_params=_cparams("parallel", "parallel"),
        name="adaln",
    )(cond8, ada_w, ada_b.reshape(depth, 1, n))


def _inproj_body(x_ref, g_ref, sh_ref, sc_ref, w_ref, cs_ref, sa_ref, sb_ref,
                 q_ref, kv_ref, lru_ref, hg_ref):
    h = _rms(x_ref[...]) * g_ref[...]
    h = h * (1.0 + sc_ref[...]) + sh_ref[...]
    p = jnp.dot(h.astype(BF16), w_ref[...], preferred_element_type=F32)
    cs, sa, sb = cs_ref[...], sa_ref[...], sb_ref[...]

    def rope(t):
        return t * cs + pltpu.roll(t, LANES - 16, 1) * sa + pltpu.roll(t, 16, 1) * sb

    for j in range(4):
        q_ref[:, j * LANES:(j + 1) * LANES] = (
            rope(p[:, j * LANES:(j + 1) * LANES]) * (HEAD_DIM ** -0.5)).astype(BF16)
    k = rope(p[:, 512:640])
    v = p[:, 640:768]
    lo = lax.broadcasted_iota(I32, k.shape, 1) < HEAD_DIM
    kr = pltpu.roll(k, HEAD_DIM, 1)
    vr = pltpu.roll(v, HEAD_DIM, 1)
    kv_ref[:, 0:128] = jnp.where(lo, k, kr).astype(BF16)
    kv_ref[:, 128:256] = jnp.where(lo, kr, k).astype(BF16)
    kv_ref[:, 256:384] = jnp.where(lo, v, vr).astype(BF16)
    kv_ref[:, 384:512] = jnp.where(lo, vr, v).astype(BF16)
    lru_ref[...] = p[:, 768:1280]
    hg_ref[...] = p[:, 1280:2560]


def inproj(xs, g, shift, scale, w_bf, rope_c, rope_a, rope_b, *, tiles_per_batch, n_seg):
    r, d = xs.shape
    n_tiles = r // TM
    n_lat = tiles_per_batch * (n_seg - 1)

    def seg(i):
        return (jnp.minimum(i // tiles_per_batch, n_seg - 1), 0, 0)

    def rope_idx(i):
        return (jnp.where(i < n_lat, i % tiles_per_batch, tiles_per_batch), 0)

    row = lambda w: pl.BlockSpec((TM, w), lambda i: (i, 0))
    return pl.pallas_call(
        _inproj_body,
        grid=(n_tiles,),
        in_specs=[row(d),
                  pl.BlockSpec((1, d), lambda i: (0, 0)),
                  pl.BlockSpec((None, 1, d), seg),
                  pl.BlockSpec((None, 1, d), seg),
                  pl.BlockSpec(w_bf.shape, lambda i: (0, 0)),
                  pl.BlockSpec((TM, LANES), rope_idx),
                  pl.BlockSpec((TM, LANES), rope_idx),
                  pl.BlockSpec((TM, LANES), rope_idx)],
        out_specs=[row(512), row(512), row(512), row(1280)],
        out_shape=[jax.ShapeDtypeStruct((r, 512), BF16),
                   jax.ShapeDtypeStruct((r, 512), BF16),
                   jax.ShapeDtypeStruct((r, 512), F32),
                   jax.ShapeDtypeStruct((r, 1280), F32)],
        compiler_params=_cparams("parallel"),
        name="inproj",
    )(xs, g, shift, scale, w_bf, rope_c, rope_a, rope_b)


def _softmax_pv(s_parts, v_parts, sink_col):
    m = sink_col
    for s in s_parts:
        m = jnp.maximum(m, jnp.max(s, axis=-1, keepdims=True))
    l = jnp.exp(sink_col - m)
    o = None
    for s, v in zip(s_parts, v_parts):
        p = jnp.exp(s - m)
        l = l + jnp.sum(p, axis=-1, keepdims=True)
        pv = jnp.dot(p.astype(BF16), v, preferred_element_type=F32)
        o = pv if o is None else o + pv
    return o / l


def _attn_body(sink_ref, q_ref, kvm_ref, kvp_ref, kvn_ref, kvc_ref, o_ref, *,
               tiles_per_batch, n_lat, ctx_len, n_batch):
    i = pl.program_id(0)
    blk = WINDOW
    nsub = TM // blk

    def stacked_q(q_sub, g):
        rows = q_sub.shape[0]
        lo = lax.broadcasted_iota(I32, (rows, LANES), 1) < HEAD_DIM
        parts = []
        for jq in (2 * g, 2 * g + 1):
            qg = q_sub[:, jq * LANES:(jq + 1) * LANES]
            parts.append(jnp.where(lo, qg, jnp.zeros_like(qg)))
            parts.append(jnp.where(lo, jnp.zeros_like(qg), qg))
        return jnp.concatenate(parts, axis=0)

    def sink_column(g, rows):
        blk_id = lax.broadcasted_iota(I32, (4 * rows, 1), 0) // rows
        col = jnp.full((4 * rows, 1), sink_ref[4 * g + 3], F32)
        for b in range(3):
            col = jnp.where(blk_id == b, sink_ref[4 * g + b], col)
        return col

    def write_heads(o, g, rows, row0):
        lo = lax.broadcasted_iota(I32, (rows, LANES), 1) < HEAD_DIM
        for t, jq in enumerate((2 * g, 2 * g + 1)):
            o_e0 = o[(2 * t) * rows:(2 * t + 1) * rows]
            o_e1 = o[(2 * t + 1) * rows:(2 * t + 2) * rows]
            o_ref[row0:row0 + rows, jq * LANES:(jq + 1) * LANES] = (
                jnp.where(lo, o_e0, o_e1).astype(BF16))

    @pl.when(i < n_lat)
    def _latent():
        j = i % tiles_per_batch
        r_io = lax.broadcasted_iota(I32, (4 * blk, blk), 0) % blk
        c_io = lax.broadcasted_iota(I32, (4 * blk, blk), 1)
        for sb in range(nsub):
            q_sub = q_ref[sb * blk:(sb + 1) * blk, :]
            has_prev = jnp.logical_or(j > 0, sb > 0)
            has_next = jnp.logical_or(j < tiles_per_batch - 1, sb < nsub - 1)
            m_prev = jnp.logical_and(c_io >= r_io, has_prev)
            m_next = jnp.logical_and(c_io <= r_io, has_next)
            for g in range(2):
                kc, vc = g * LANES, (2 + g) * LANES

                def kv_block(b, col):
                    if b < 0:
                        return kvp_ref[:, col:col + LANES]
                    if b >= nsub:
                        return kvn_ref[:, col:col + LANES]
                    return kvm_ref[b * blk:(b + 1) * blk, col:col + LANES]

                qs = stacked_q(q_sub, g)
                s_prev = lax.dot_general(qs, kv_block(sb - 1, kc), NT_DIMS, preferred_element_type=F32)
                s_own = lax.dot_general(qs, kv_block(sb, kc), NT_DIMS, preferred_element_type=F32)
                s_next = lax.dot_general(qs, kv_block(sb + 1, kc), NT_DIMS, preferred_element_type=F32)
                s_ctx = lax.dot_general(qs, kvc_ref[:, kc:kc + LANES], NT_DIMS, preferred_element_type=F32)
                s_prev = jnp.where(m_prev, s_prev, NEG_INF)
                s_next = jnp.where(m_next, s_next, NEG_INF)
                o = _softmax_pv(
                    [s_prev, s_own, s_next, s_ctx],
                    [kv_block(sb - 1, vc), kv_block(sb, vc), kv_block(sb + 1, vc),
                     kvc_ref[:, vc:vc + LANES]],
                    sink_column(g, blk))
                write_heads(o, g, blk, sb * blk)

    @pl.when(i >= n_lat)
    def _context():
        for bb in range(n_batch):
            r0 = bb * ctx_len
            q_sub = q_ref[r0:r0 + ctx_len, :]
            for g in range(2):
                kc, vc = g * LANES, (2 + g) * LANES
                qs = stacked_q(q_sub, g)
                s = lax.dot_general(qs, kvm_ref[r0:r0 + ctx_len, kc:kc + LANES], NT_DIMS,
                                    preferred_element_type=F32)
                o = _softmax_pv([s], [kvm_ref[r0:r0 + ctx_len, vc:vc + LANES]],
                                sink_column(g, ctx_len))
                write_heads(o, g, ctx_len, r0)


def attention(q, kv, sink, *, tiles_per_batch, n_batch, ctx_len):
    r = q.shape[0]
    n_tiles = r // TM
    n_lat = tiles_per_batch * n_batch
    assert n_batch * ctx_len == TM and n_tiles == n_lat + 1
    sub = TM // WINDOW
    n_blk128 = r // WINDOW
    ctx_blk0 = (n_lat * TM) // ctx_len

    def prev_idx(i, s):
        return (jnp.maximum(i * sub - 1, 0), 0)

    def next_idx(i, s):
        return (jnp.minimum(i * sub + sub, n_blk128 - 1), 0)

    def ctx_idx(i, s):
        return (ctx_blk0 + jnp.minimum(i // tiles_per_batch, n_batch - 1), 0)

    body = functools.partial(_attn_body, tiles_per_batch=tiles_per_batch, n_lat=n_lat,
                             ctx_len=ctx_len, n_batch=n_batch)
    return pl.pallas_call(
        body,
        grid_spec=pltpu.PrefetchScalarGridSpec(
            num_scalar_prefetch=1,
            grid=(n_tiles,),
            in_specs=[pl.BlockSpec((TM, 512), lambda i, s: (i, 0)),
                      pl.BlockSpec((TM, 512), lambda i, s: (i, 0)),
                      pl.BlockSpec((WINDOW, 512), prev_idx),
                      pl.BlockSpec((WINDOW, 512), next_idx),
                      pl.BlockSpec((ctx_len, 512), ctx_idx)],
            out_specs=pl.BlockSpec((TM, 512), lambda i, s: (i, 0))),
        out_shape=jax.ShapeDtypeStruct((r, 512), BF16),
        compiler_params=_cparams("parallel"),
        name="attention",
    )(sink, q, kv, kv, kv, kv)


def _scan_maps(n_batch, seq, ctx_len):
    assert ctx_len == TSCAN
    nl = seq // TSCAN
    ctx0 = (n_batch * seq) // TSCAN

    def fwd(b, c):
        return jnp.where(c == 0, ctx0 + b, b * nl + c - 1)

    def bwd(b, c):
        return jnp.where(c == 0, ctx0 + b, b * nl + nl - c)

    return nl, fwd, bwd


def _lru_dir(x_ref, xp_ref, xn_ref, has_prev, has_next, cw_ref, cb_ref, w_ref, br_ref, bi_ref,
             lam_ref, h_ref, a_scr, b_scr, out_ref, reverse):
    t = TSCAN
    xp = jnp.where(has_prev, xp_ref[...], 0.0)
    xn = jnp.where(has_next, xn_ref[...], 0.0)
    xpad = jnp.concatenate([xp, x_ref[...], xn], axis=0)
    n = t + 16
    cw = cw_ref[...]
    u = cb_ref[...] + jnp.zeros((t, xpad.shape[1]), F32)
    for j in range(4):
        sh = (2 - j) % n
        rolled = xpad if sh == 0 else pltpu.roll(xpad, sh, 0)
        u = u + rolled[8:8 + t] * cw[j:j + 1, :]
    gates = jnp.dot(u.astype(BF16), w_ref[...], preferred_element_type=F32)
    c = u.shape[1]
    r = _sigmoid(gates[:, :c] + br_ref[...])
    ig = _sigmoid(gates[:, c:] + bi_ref[...])
    nl = -lam_ref[...]
    softplus = jnp.maximum(nl, 0.0) + jnp.log1p(jnp.exp(-jnp.abs(nl)))
    log_a = (-LRU_C) * r * softplus
    a = jnp.exp(log_a)
    mult = jnp.sqrt(jnp.maximum(1.0 - a * a, SQRT_FLOOR))
    bv = mult * (ig * u)

    g8 = t // 8
    a3 = a.reshape(g8, 8, c)
    b3 = bv.reshape(g8, 8, c)
    r8 = lax.broadcasted_iota(I32, (g8, 8, c), 1)
    for s in (1, 2, 4):
        if reverse:
            a_sh, b_sh, ok = pltpu.roll(a3, 8 - s, 1), pltpu.roll(b3, 8 - s, 1), r8 < 8 - s
        else:
            a_sh, b_sh, ok = pltpu.roll(a3, s, 1), pltpu.roll(b3, s, 1), r8 >= s
        b3 = jnp.where(ok, a3 * b_sh + b3, b3)
        a3 = jnp.where(ok, a3 * a_sh, a3)
    a_scr[...] = a3.reshape(t, c)
    b_scr[...] = b3.reshape(t, c)
    edge = 0 if reverse else 7

    def group(gi, carry):
        g = (g8 - 1 - gi) if reverse else gi
        rows = pl.ds(pl.multiple_of(g * 8, 8), 8)
        hg = b_scr[rows, :] + a_scr[rows, :] * carry
        out_ref[rows, :] = hg
        return jnp.broadcast_to(hg[edge:edge + 1, :], hg.shape)

    h_ref[...] = lax.fori_loop(0, g8, group, h_ref[...], unroll=4)


def _lru_body(xf_ref, xfp_ref, xfn_ref, xb_ref, xbp_ref, xbn_ref, cw_ref, cb_ref, w_ref, br_ref,
              bi_ref, lam_ref, hf_ref, hb_ref, st_ref, a_scr, b_scr, *, nl):
    c = pl.program_id(1)

    @pl.when(c == 0)
    def _init():
        st_ref[...] = jnp.zeros_like(st_ref)

    lat = c > 0
    _lru_dir(xf_ref, xfp_ref, xfn_ref, jnp.logical_and(lat, c > 1), jnp.logical_and(lat, c < nl),
             cw_ref, cb_ref, w_ref.at[0], br_ref.at[0], bi_ref.at[0], lam_ref.at[0],
             st_ref.at[0], a_scr, b_scr, hf_ref, False)
    _lru_dir(xb_ref, xbp_ref, xbn_ref, jnp.logical_and(lat, c < nl), jnp.logical_and(lat, c > 1),
             cw_ref, cb_ref, w_ref.at[1], br_ref.at[1], bi_ref.at[1], lam_ref.at[1],
             st_ref.at[1], a_scr, b_scr, hb_ref, True)


def lru_scan(lru, conv_w, conv_b, w_gates, br, bi, lam, *, n_batch, seq, ctx_len):
    r = lru.shape[0]
    c = lru.shape[1] // 2
    nl, fwd, bwd = _scan_maps(n_batch, seq, ctx_len)
    per = TSCAN // 8
    n8 = r // 8

    def main(m):
        return pl.BlockSpec((TSCAN, c), lambda b, s: (m(b, s), 0))

    def prev(m):
        return pl.BlockSpec((8, c), lambda b, s: (jnp.maximum(m(b, s) * per - 1, 0), 0))

    def nxt(m):
        return pl.BlockSpec((8, c), lambda b, s: (jnp.minimum(m(b, s) * per + per, n8 - 1), 0))

    full = lambda a: pl.BlockSpec(a.shape, lambda b, s: (0,) * a.ndim)
    return pl.pallas_call(
        functools.partial(_lru_body, nl=nl),
        grid=(n_batch, nl + 1),
        in_specs=[main(fwd), prev(fwd), nxt(fwd), main(bwd), prev(bwd), nxt(bwd),
                  full(conv_w), full(conv_b), full(w_gates), full(br), full(bi), full(lam)],
        out_specs=[main(fwd), main(bwd)],
        out_shape=[jax.ShapeDtypeStruct((r, c), F32)] * 2,
        scratch_shapes=[pltpu.VMEM((2, 8, c), F32), pltpu.VMEM((TSCAN, c), F32),
                        pltpu.VMEM((TSCAN, c), F32)],
        compiler_params=_cparams("parallel", "arbitrary"),
        name="lru_scan",
    )(lru, lru, lru, lru, lru, lru, conv_w, conv_b, w_gates, br, bi, lam)


def _hgrn_dir(q_ref, z_ref, v_ref, lb_ref, st_ref, out_ref, reverse):
    t = TSCAN
    nch = t // HG_CHUNK
    q = q_ref[...]
    z = z_ref[...]
    v = v_ref[...]
    lb = lb_ref[...]
    w = q.shape[1]
    log_f = jnp.log(lb + (1.0 - lb) * _sigmoid(z))
    k = (1.0 - lb) * _sigmoid(-z)

    rc = lax.broadcasted_iota(I32, (t, w), 0) % HG_CHUNK
    b = log_f
    s = 1
    while s < HG_CHUNK:
        if reverse:
            b = b + jnp.where(rc < HG_CHUNK - s, pltpu.roll(b, t - s, 0), 0.0)
        else:
            b = b + jnp.where(rc >= s, pltpu.roll(b, s, 0), 0.0)
        s *= 2
    tot = jnp.sum(log_f.reshape(nch, HG_CHUNK, w), axis=1, keepdims=True)
    b_last = jnp.broadcast_to(tot, (nch, HG_CHUNK, w)).reshape(t, w)
    b_half = 0.5 * b_last
    q_i = (q * jnp.exp(b - b_half)).astype(BF16)
    k_i = (k * jnp.exp(b_half - b)).astype(BF16)
    q_s = (q * jnp.exp(b)).astype(BF16)
    k_e = (k * jnp.exp(b_last - b)).astype(BF16)
    decay = jnp.exp(tot.reshape(nch, w))
    v_bf = v.astype(BF16)

    ri = lax.broadcasted_iota(I32, (t, t), 0)
    ci = lax.broadcasted_iota(I32, (t, t), 1)
    same = (ri // HG_CHUNK) == (ci // HG_CHUNK)
    causal = jnp.logical_and(same, (ci >= ri) if reverse else (ci <= ri))
    lo = lax.broadcasted_iota(I32, (t, LANES), 1) < HEAD_DIM
    chunk_of_row = lax.broadcasted_iota(I32, (t, LANES), 0) // HG_CHUNK
    bd_r = lax.broadcasted_iota(I32, (LANES, LANES), 0) // HEAD_DIM
    bd_c = lax.broadcasted_iota(I32, (LANES, LANES), 1) // HEAD_DIM
    block_diag = bd_r == bd_c

    for p in range(w // LANES):
        cols = slice(p * LANES, (p + 1) * LANES)
        qi_p, ki_p, qs_p, ke_p, v_p = q_i[:, cols], k_i[:, cols], q_s[:, cols], k_e[:, cols], v_bf[:, cols]
        o_half = []
        for e in range(2):
            qm = jnp.where(lo if e == 0 else jnp.logical_not(lo), qi_p, jnp.zeros_like(qi_p))
            att = lax.dot_general(qm, ki_p, NT_DIMS, preferred_element_type=F32)
            att = jnp.where(causal, att, 0.0).astype(BF16)
            o_half.append(jnp.dot(att, v_p, preferred_element_type=F32))
        out_ref[:, cols] = jnp.where(lo, o_half[0], o_half[1])

        v_t = v[:, cols].T.astype(BF16)
        st = st_ref[p]
        order = range(nch - 1, -1, -1) if reverse else range(nch)
        for n in order:
            rows = slice(n * HG_CHUNK, (n + 1) * HG_CHUNK)
            o_inter = lax.dot_general(qs_p[rows], st.astype(BF16), NT_DIMS, preferred_element_type=F32)
            out_ref[rows, cols] = out_ref[rows, cols] + o_inter
            ke_n = jnp.where(chunk_of_row == n, ke_p, jnp.zeros_like(ke_p))
            kv_t = jnp.dot(v_t, ke_n, preferred_element_type=F32)
            st = decay[n:n + 1, cols] * st + jnp.where(block_diag, kv_t, 0.0)
        st_ref[p] = st


def _hgrn_body(qf_ref, zf_ref, vf_ref, qb_ref, zb_ref, vb_ref, lb_ref, of_ref, ob_ref, st_ref):
    c = pl.program_id(1)

    @pl.when(c == 0)
    def _init():
        st_ref[...] = jnp.zeros_like(st_ref)

    _hgrn_dir(qf_ref, zf_ref, vf_ref, lb_ref.at[0], st_ref.at[0], of_ref, False)
    _hgrn_dir(qb_ref, zb_ref, vb_ref, lb_ref.at[1], st_ref.at[1], ob_ref, True)


def hgrn_scan(hg, lower_bounds, *, n_batch, seq, ctx_len):
    r = hg.shape[0]
    w = hg.shape[1] // 5
    nl, fwd, bwd = _scan_maps(n_batch, seq, ctx_len)

    def col(m, j):
        return pl.BlockSpec((TSCAN, w), lambda b, s: (m(b, s), j))

    lb3 = lower_bounds.reshape(2, 1, w)
    return pl.pallas_call(
        _hgrn_body,
        grid=(n_batch, nl + 1),
        in_specs=[col(fwd, 0), col(fwd, 1), col(fwd, 3), col(bwd, 0), col(bwd, 2), col(bwd, 3),
                  pl.BlockSpec(lb3.shape, lambda b, s: (0, 0, 0))],
        out_specs=[col(fwd, 0), col(bwd, 0)],
        out_shape=[jax.ShapeDtypeStruct((r, w), F32)] * 2,
        scratch_shapes=[pltpu.VMEM((2, w // LANES, LANES, LANES), F32)],
        compiler_params=_cparams("parallel", "arbitrary"),
        name="hgrn_scan",
    )(hg, hg, hg, hg, hg, hg, lb3)


def _outproj_router_body(x_ref, att_ref, hf_ref, hb_ref, lg_ref, of_ref, ob_ref, gg_ref, hn_ref,
                         wo_ref, gm_ref, sh_ref, sc_ref, n2_ref, rw_ref, rb_ref,
                         xo_ref, h2_ref, idx_ref, gate_ref, rank_ref, cnt_ref):
    lru_y = (hf_ref[...] + hb_ref[...]) * jax.nn.gelu(lg_ref[...], approximate=True)
    o = of_ref[...] + ob_ref[...]
    w = o.shape[1]
    gr = lax.broadcasted_iota(I32, (w, w), 0) // HEAD_DIM
    gc = lax.broadcasted_iota(I32, (w, w), 1) // HEAD_DIM
    head_mean = jnp.where(gr == gc, 1.0 / HEAD_DIM, 0.0).astype(F32)
    ms = jnp.dot(o * o, head_mean, preferred_element_type=F32, precision=lax.Precision.HIGHEST)
    gg = gg_ref[...]
    hg_y = o * lax.rsqrt(ms + EPS) * hn_ref[...] * (gg * _sigmoid(gg))
    y = jnp.dot(att_ref[...], wo_ref[0:512, :], preferred_element_type=F32)
    y = y + jnp.dot(lru_y.astype(BF16), wo_ref[512:768, :], preferred_element_type=F32)
    y = y + jnp.dot(hg_y.astype(BF16), wo_ref[768:1024, :], preferred_element_type=F32)
    x = x_ref[...] + gm_ref[...] * y
    xo_ref[...] = x
    h2 = _rms(x) * n2_ref[...]
    h2 = h2 * (1.0 + sc_ref[...]) + sh_ref[...]
    h2_ref[...] = h2

    logits = lax.dot_general(rw_ref[...], h2, NT_DIMS, preferred_element_type=F32,
                             precision=lax.Precision.HIGHEST) + rb_ref[...]
    ne, tm = logits.shape
    e_io = lax.broadcasted_iota(I32, (ne, tm), 0).astype(F32)
    work = logits
    vals, hots = [], []
    for k in range(TOP_K):
        m = jnp.max(work, axis=0, keepdims=True)
        idx = jnp.min(jnp.where(work == m, e_io, float(ne)), axis=0, keepdims=True)
        hot = e_io == idx
        vals.append(m)
        hots.append(hot)
        idx_ref[k:k + 1, :] = idx.astype(I32)
        work = jnp.where(hot, -jnp.inf, work)
    exps = [jnp.exp(v - vals[0]) for v in vals]
    denom = exps[0] + exps[1] + exps[2] + exps[3]
    for k in range(TOP_K):
        gate_ref[k:k + 1, :] = exps[k] / denom
    z4 = jnp.zeros((8 - TOP_K, tm), F32)
    gate_ref[TOP_K:8, :] = z4
    idx_ref[TOP_K:8, :] = z4.astype(I32)
    rank_ref[TOP_K:8, :] = z4.astype(I32)

    chosen = jnp.logical_or(jnp.logical_or(hots[0], hots[1]), jnp.logical_or(hots[2], hots[3]))
    sr = lax.broadcasted_iota(I32, (tm, tm), 0)
    sc = lax.broadcasted_iota(I32, (tm, tm), 1)
    before = jnp.where(sr < sc, 1.0, 0.0).astype(BF16)
    chosen_f = jnp.where(chosen, 1.0, 0.0)
    prefix = jnp.dot(chosen_f.astype(BF16), before, preferred_element_type=F32)
    for k in range(TOP_K):
        rk = jnp.sum(jnp.where(hots[k], prefix, 0.0), axis=0, keepdims=True)
        rank_ref[k:k + 1, :] = rk.astype(I32)
    cnt_ref[...] = jnp.broadcast_to(jnp.sum(chosen_f, axis=1, keepdims=True), cnt_ref.shape)


def outproj_router(xs, att, hf, hb, lru, of, ob, hg, hn_g, wo_bf, gate_msa, shift_mlp, scale_mlp,
                   n2_g, rw_t, rb, *, tiles_per_batch, n_seg):
    r, d = xs.shape
    n_tiles = r // TM

    def seg(i):
        return (jnp.minimum(i // tiles_per_batch, n_seg - 1), 0, 0)

    row = lambda w, j=0: pl.BlockSpec((TM, w), lambda i: (i, j))
    full = lambda a: pl.BlockSpec(a.shape, lambda i: (0,) * a.ndim)
    modspec = pl.BlockSpec((None, 1, d), seg)
    lane_out = pl.BlockSpec((8, TM), lambda i: (0, i))
    return pl.pallas_call(
        _outproj_router_body,
        grid=(n_tiles,),
        in_specs=[row(d), row(512), row(256), row(256), row(256, 1), row(256), row(256),
                  row(256, 4), full(hn_g), full(wo_bf), modspec, modspec, modspec, full(n2_g),
                  full(rw_t), full(rb)],
        out_specs=[row(d), row(d), lane_out, lane_out, lane_out,
                   pl.BlockSpec((None, N_EXPERTS, LANES), lambda i: (i, 0, 0))],
        out_shape=[jax.ShapeDtypeStruct((r, d), F32), jax.ShapeDtypeStruct((r, d), F32),
                   jax.ShapeDtypeStruct((8, r), I32), jax.ShapeDtypeStruct((8, r), F32),
                   jax.ShapeDtypeStruct((8, r), I32),
                   jax.ShapeDtypeStruct((n_tiles, N_EXPERTS, LANES), F32)],
        compiler_params=_cparams("parallel"),
        name="outproj_router",
    )(xs, att, hf, hb, lru, of, ob, hg, hn_g, wo_bf, gate_msa, shift_mlp, scale_mlp, n2_g, rw_t, rb)


def _local_pos(idx_ref, rank_ref, toff_ref, k):
    ne = toff_ref.shape[0]
    tm = idx_ref.shape[1]
    e_io = lax.broadcasted_iota(I32, (ne, tm), 0)
    off = jnp.sum(jnp.where(e_io == idx_ref[k:k + 1, :], toff_ref[...], 0.0), axis=0, keepdims=True)
    return off + rank_ref[k:k + 1, :].astype(F32)


def _run_copies(tab_ref, make_copy, sems, start):
    def per_expert(e, carry):
        src8 = tab_ref[0, 0, e]
        dst8 = tab_ref[0, 0, N_EXPERTS + e]
        n8 = tab_ref[0, 0, 2 * N_EXPERTS + e]
        for b in range(RUN_BITS):
            size = 8 << b
            off8 = (n8 >> (b + 1)) << (b + 1)

            @pl.when(((n8 >> b) & 1) == 1)
            def _():
                cp = make_copy(pl.multiple_of((src8 + off8) * 8, 8),
                               pl.multiple_of((dst8 + off8) * 8, 8), size, sems.at[b])
                if start:
                    cp.start()
                else:
                    cp.wait()
        return carry

    lax.fori_loop(0, N_EXPERTS, per_expert, 0)


def _fill_copies(fill_ref, zero_scr, xs_ref, sems, start):
    def go(cp):
        if start:
            cp.start()
        else:
            cp.wait()

    def per_expert(e, carry):
        t8 = fill_ref[e]
        n8 = fill_ref[N_EXPERTS + e]
        for b in range(TAIL_BITS):
            size = 8 << b
            off8 = (n8 >> (b + 1)) << (b + 1)

            @pl.when(((n8 >> b) & 1) == 1)
            def _():
                go(pltpu.make_async_copy(
                    zero_scr.at[pl.ds(0, size), :],
                    xs_ref.at[pl.ds(pl.multiple_of((t8 + off8) * 8, 8), size), :], sems.at[b]))
        return carry

    lax.fori_loop(0, N_EXPERTS, per_expert, 0)

    def per_block(j, carry):
        go(pltpu.make_async_copy(zero_scr, xs_ref.at[pl.ds(pl.multiple_of(j * TB, TB), TB), :],
                                 sems.at[TAIL_BITS]))
        return carry

    lax.fori_loop(fill_ref[2 * N_EXPERTS], xs_ref.shape[0] // TB, per_block, 0)


def _dispatch_body(fill_ref, tab_ref, h_ref, idx_ref, rank_ref, toff_ref, xs_ref, z_scr, zero_scr, sems):
    tm = h_ref.shape[0]

    @pl.when(pl.program_id(0) == 0)
    def _fill():
        zero_scr[...] = jnp.zeros_like(zero_scr)
        _fill_copies(fill_ref, zero_scr, xs_ref, sems, True)
        _fill_copies(fill_ref, zero_scr, xs_ref, sems, False)

    r_io = lax.broadcasted_iota(I32, (ZR, tm), 0).astype(F32)
    hit = r_io == _local_pos(idx_ref, rank_ref, toff_ref, 0)
    for k in range(1, TOP_K):
        hit = jnp.logical_or(hit, r_io == _local_pos(idx_ref, rank_ref, toff_ref, k))
    perm = jnp.where(hit, 1.0, 0.0).astype(BF16)
    z_scr[...] = jnp.dot(perm, h_ref[...].astype(BF16), preferred_element_type=F32)

    def make_copy(s, d, size, sem):
        return pltpu.make_async_copy(z_scr.at[pl.ds(s, size), :], xs_ref.at[pl.ds(d, size), :], sem)

    _run_copies(tab_ref, make_copy, sems, True)
    _run_copies(tab_ref, make_copy, sems, False)


def moe_dispatch(h2, idx, rank, toff_col, tab, fill, p_rows):
    r, d = h2.shape
    n_tiles = r // TM
    lane_in = pl.BlockSpec((8, TM), lambda i, f: (0, i))
    return pl.pallas_call(
        _dispatch_body,
        grid_spec=pltpu.PrefetchScalarGridSpec(
            num_scalar_prefetch=1,
            grid=(n_tiles,),
            in_specs=[pl.BlockSpec((1, 1, LANES), lambda i, f: (i, 0, 0), memory_space=pltpu.SMEM),
                      pl.BlockSpec((TM, d), lambda i, f: (i, 0)),
                      lane_in, lane_in,
                      pl.BlockSpec((None, N_EXPERTS, 1), lambda i, f: (i, 0, 0))],
            out_specs=pl.BlockSpec(memory_space=pl.ANY),
            scratch_shapes=[pltpu.VMEM((ZR, d), F32), pltpu.VMEM((TB, d), F32),
                            pltpu.SemaphoreType.DMA((RUN_BITS,))]),
        out_shape=jax.ShapeDtypeStruct((p_rows, d), F32),
        compiler_params=_cparams("arbitrary"),
        name="moe_dispatch",
    )(fill, tab, h2, idx, rank, toff_col)


def _expert_body(be_ref, nu_ref, x_ref, wgu_ref, bgu_ref, wd_ref, bd_ref, y_ref, wgu_bf, wd_bf):
    i = pl.program_id(0)

    @pl.when(i < nu_ref[0])
    def _run():
        new_expert = jnp.logical_or(i == 0, be_ref[i] != be_ref[jnp.maximum(i - 1, 0)])

        @pl.when(new_expert)
        def _cast():
            rows = 128

            def chunk(j, carry):
                sl = pl.ds(pl.multiple_of(j * rows, rows), rows)
                wgu_bf[sl, :] = wgu_ref[sl, :].astype(BF16)
                wd_bf[sl, :] = wd_ref[sl, :].astype(BF16)
                return carry

            lax.fori_loop(0, wgu_ref.shape[0] // rows, chunk, 0)

        dff = wd_ref.shape[0]
        gu = jnp.dot(x_ref[...].astype(BF16), wgu_bf[...], preferred_element_type=F32) + bgu_ref[...]
        gate = jnp.minimum(gu[:, :dff], SWIGLU_LIMIT)
        up = jnp.clip(gu[:, dff:], -SWIGLU_LIMIT, SWIGLU_LIMIT)
        glu = gate * _sigmoid(SWIGLU_ALPHA * gate)
        act = ((up + 1.0) * glu).astype(BF16)
        y_ref[...] = jnp.dot(act, wd_bf[...], preferred_element_type=F32) + bd_ref[...]

    @pl.when(i >= nu_ref[0])
    def _unused():
        y_ref[...] = jnp.zeros_like(y_ref)


def moe_experts(xs_sorted, blk_e, n_used, wgu, bgu, wd, bd):
    p_rows, d = xs_sorted.shape
    ne, _, dgu = wgu.shape
    dff = wd.shape[1]
    assert dff == d
    return pl.pallas_call(
        _expert_body,
        grid_spec=pltpu.PrefetchScalarGridSpec(
            num_scalar_prefetch=2,
            grid=(p_rows // TB,),
            in_specs=[pl.BlockSpec((TB, d), lambda i, be, nu: (i, 0)),
                      pl.BlockSpec((None, d, dgu), lambda i, be, nu: (be[i], 0, 0)),
                      pl.BlockSpec((None, 1, dgu), lambda i, be, nu: (be[i], 0, 0)),
                      pl.BlockSpec((None, dff, d), lambda i, be, nu: (be[i], 0, 0)),
                      pl.BlockSpec((None, 1, d), lambda i, be, nu: (be[i], 0, 0))],
            out_specs=pl.BlockSpec((TB, d), lambda i, be, nu: (i, 0)),
            scratch_shapes=[pltpu.VMEM((d, dgu), BF16), pltpu.VMEM((dff, d), BF16)]),
        out_shape=jax.ShapeDtypeStruct((p_rows, d), F32),
        compiler_params=_cparams("arbitrary"),
        name="moe_experts",
    )(blk_e, n_used, xs_sorted, wgu, bgu.reshape(ne, 1, dgu), wd, bd.reshape(ne, 1, d))


def _combine_body(tab_ref, x_ref, idx_ref, rank_ref, gate_ref, toff_ref, gm_ref, fg_ref, ys_ref,
                  o_ref, zy_scr, sems, *, final):
    tm = x_ref.shape[0]
    zy_scr[TOP_K * tm:ZR, :] = jnp.zeros((ZR - TOP_K * tm, zy_scr.shape[1]), F32)

    def make_copy(s, d, size, sem):
        return pltpu.make_async_copy(ys_ref.at[pl.ds(d, size), :], zy_scr.at[pl.ds(s, size), :], sem)

    _run_copies(tab_ref, make_copy, sems, True)
    pos = [_local_pos(idx_ref, rank_ref, toff_ref, k) for k in range(TOP_K)]
    packed = jnp.concatenate(pos + [gate_ref[0:TOP_K, :], jnp.zeros((LANES - 2 * TOP_K, tm), F32)],
                             axis=0)
    cols = packed.T
    c_io = lax.broadcasted_iota(I32, (tm, ZR), 1).astype(F32)
    weights = jnp.zeros((tm, ZR), F32)
    for k in range(TOP_K):
        weights = jnp.where(c_io == cols[:, k:k + 1], cols[:, TOP_K + k:TOP_K + k + 1], weights)
    _run_copies(tab_ref, make_copy, sems, False)
    acc = jnp.dot(weights.astype(BF16), zy_scr[...].astype(BF16), preferred_element_type=F32)
    x = x_ref[...] + gm_ref[...] * acc
    if final:
        x = _rms(x) * fg_ref[...]
    o_ref[...] = x


def moe_combine(xs, idx, rank, gates, toff_col, tab, ys, gate_mlp, final_g, *, tiles_per_batch,
                n_seg, n_tiles, final):
    d = xs.shape[1]

    def seg(i):
        return (jnp.minimum(i // tiles_per_batch, n_seg - 1), 0, 0)

    lane_in = pl.BlockSpec((8, TM), lambda i: (0, i))
    return pl.pallas_call(
        functools.partial(_combine_body, final=final),
        grid=(n_tiles,),
        in_specs=[pl.BlockSpec((1, 1, LANES), lambda i: (i, 0, 0), memory_space=pltpu.SMEM),
                  pl.BlockSpec((TM, d), lambda i: (i, 0)),
                  lane_in, lane_in, lane_in,
                  pl.BlockSpec((None, N_EXPERTS, 1), lambda i: (i, 0, 0)),
                  pl.BlockSpec((None, 1, d), seg),
                  pl.BlockSpec((1, d), lambda i: (0, 0)),
                  pl.BlockSpec(memory_space=pl.ANY)],
        out_specs=pl.BlockSpec((TM, d), lambda i: (i, 0)),
        out_shape=jax.ShapeDtypeStruct((n_tiles * TM, d), F32),
        scratch_shapes=[pltpu.VMEM((ZR, d), F32), pltpu.SemaphoreType.DMA((RUN_BITS,))],
        compiler_params=_cparams("arbitrary"),
        name="moe_combine",
    )(tab, xs, idx, rank, gates, toff_col, gate_mlp, final_g, ys)


def _rope_tables(seq):
    nf = HEAD_DIM // 4
    pos = jnp.arange(seq)
    rows = (pos // GRID_W).astype(F32)
    cols = (pos % GRID_W).astype(F32)
    inv_freq = ROPE_BASE ** (-jnp.arange(nf, dtype=F32) / nf)
    d = jnp.arange(LANES) % HEAD_DIM
    axis = d // (2 * nf)
    half = (d // nf) % 2
    f = d % nf
    ang = jnp.where(axis[None, :] == 0, rows[:, None], cols[:, None]) * inv_freq[f][None, :]
    cs, sn = jnp.cos(ang), jnp.sin(ang)
    ca = jnp.where(half[None, :] == 0, -sn, 0.0)
    cb = jnp.where(half[None, :] == 1, sn, 0.0)
    pad1 = jnp.ones((TM, LANES), F32)
    pad0 = jnp.zeros((TM, LANES), F32)
    return (jnp.concatenate([cs, pad1]), jnp.concatenate([ca, pad0]), jnp.concatenate([cb, pad0]))


def _block_diag(w):
    n, c, _ = w.shape
    eye = jnp.eye(n, dtype=w.dtype)
    return (eye[:, None, :, None] * w[:, :, None, :]).reshape(n * c, n * c)


def _moe_layout(tile_counts, n_blocks):
    cnt8 = (tile_counts.astype(I32) + 7) // 8 * 8
    toff = jnp.cumsum(cnt8, axis=1) - cnt8
    goff = jnp.cumsum(cnt8, axis=0) - cnt8
    padded = (jnp.sum(cnt8, axis=0) + TB - 1) // TB * TB
    pends = jnp.cumsum(padded)
    dst = (pends - padded)[None, :] + goff
    n_tiles = cnt8.shape[0]
    tab = jnp.concatenate([toff, dst, cnt8, jnp.zeros((n_tiles, LANES - 3 * N_EXPERTS), I32)], axis=1) // 8
    blk_start = jnp.arange(n_blocks, dtype=I32) * TB
    blk_e = jnp.minimum(jnp.sum((pends[None, :] <= blk_start[:, None]).astype(I32), axis=1), N_EXPERTS - 1)
    n_used = (pends[-1:] // TB).astype(I32)
    total = jnp.sum(cnt8, axis=0)
    fill = jnp.concatenate([(pends - padded + total) // 8, (padded - total) // 8, n_used,
                            jnp.zeros((LANES - 2 * N_EXPERTS - 1,), I32)]).astype(I32)
    return (tab.reshape(n_tiles, 1, LANES), toff.astype(F32).reshape(n_tiles, N_EXPERTS, 1),
            fill, blk_e.astype(I32), n_used)


def kernel(x, c, ctx, c_ctx, ada_w, ada_b, norm1_g, w_in, attn_sink, conv_w, conv_b, lru_wr, lru_br,
           lru_wi, lru_bi, lru_lambda, hgrn_lb_logits, hgrn_norm_g, w_out, norm2_g, router_w,
           router_b, moe_w_gu, moe_b_gu, moe_w_down, moe_b_down, final_g):
    n_batch, seq, d = x.shape
    ctx_len = ctx.shape[1]
    depth = ada_w.shape[0]
    assert n_batch * ctx_len == TM and seq % TM == 0 and ctx_len == TSCAN
    tiles_per_batch = seq // TM
    n_seg = n_batch + 1
    n_lat = n_batch * tiles_per_batch
    r = n_batch * seq + n_batch * ctx_len
    n_blocks = -(-(r * TOP_K + (r // TM) * N_EXPERTS * 7) // TB) + N_EXPERTS
    p_rows = n_blocks * TB

    xs = jnp.concatenate([x.reshape(n_batch * seq, d), ctx.reshape(n_batch * ctx_len, d)], axis=0)
    cond8 = jnp.zeros((8, d), F32).at[:n_batch].set(c).at[n_batch].set(c_ctx)
    mods = adaln(cond8, ada_w, ada_b)
    rope_c, rope_a, rope_b = _rope_tables(seq)
    lb_p = jax.nn.softmax(hgrn_lb_logits.astype(F32), axis=0)
    lower_bounds = jnp.cumsum(lb_p, axis=0) - lb_p[0]

    out = None
    for layer in range(depth):
        mod = [mods[layer, :, j * d:(j + 1) * d].reshape(8, 1, d) for j in range(6)]
        q, kv, lru, hg = inproj(xs, norm1_g[layer].reshape(1, d), mod[0], mod[1],
                                w_in[layer].astype(BF16), rope_c, rope_a, rope_b,
                                tiles_per_batch=tiles_per_batch, n_seg=n_seg)
        att = attention(q, kv, attn_sink[layer], tiles_per_batch=tiles_per_batch,
                        n_batch=n_batch, ctx_len=ctx_len)
        w_gates = jnp.stack([jnp.concatenate([_block_diag(lru_wr[layer, dd]),
                                              _block_diag(lru_wi[layer, dd])], axis=1)
                             for dd in range(2)]).astype(BF16)
        cw = lru_br.shape[-1]
        hf, hb = lru_scan(lru, conv_w[layer], conv_b[layer].reshape(1, cw), w_gates,
                          lru_br[layer].reshape(2, 1, cw), lru_bi[layer].reshape(2, 1, cw),
                          lru_lambda[layer].reshape(2, 1, cw),
                          n_batch=n_batch, seq=seq, ctx_len=ctx_len)
        of, ob = hgrn_scan(hg, lower_bounds[layer], n_batch=n_batch, seq=seq, ctx_len=ctx_len)
        xs, h2, idx, gates, rank, counts = outproj_router(
            xs, att, hf, hb, lru, of, ob, hg, hgrn_norm_g[layer].reshape(1, -1),
            w_out[layer].astype(BF16), mod[2], mod[3], mod[4], norm2_g[layer].reshape(1, d),
            router_w[layer].T, router_b[layer].reshape(N_EXPERTS, 1),
            tiles_per_batch=tiles_per_batch, n_seg=n_seg)
        tab, toff_col, fill, blk_e, n_used = _moe_layout(counts[:, :, 0], n_blocks)
        xs_sorted = moe_dispatch(h2, idx, rank, toff_col, tab, fill, p_rows)
        ys = moe_experts(xs_sorted, blk_e, n_used, moe_w_gu[layer], moe_b_gu[layer],
                         moe_w_down[layer], moe_b_down[layer])
        final = layer == depth - 1
        res = moe_combine(xs, idx, rank, gates, toff_col, tab, ys, mod[5], final_g.reshape(1, d),
                          tiles_per_batch=tiles_per_batch, n_seg=n_seg,
                          n_tiles=n_lat if final else n_lat + 1, final=final)
        if final:
            out = res.reshape(n_batch, seq, d)
        else:
            xs = res
    return out
```

```python
import functools

import jax
import jax.numpy as jnp
from jax import lax
from jax.experimental import pallas as pl
from jax.experimental.pallas import tpu as pltpu

F32 = jnp.float32
BF16 = jnp.bfloat16
I32 = jnp.int32

EPS = 1e-6
NEG_INF = -1e30
GRID_W = 64
ROPE_BASE = 10000.0
HEAD_DIM = 64
N_HEADS = 8
WINDOW = 128
LRU_C = 8.0
SQRT_FLOOR = 1e-12
HG_CHUNK = 32
N_EXPERTS = 32
TOP_K = 4
SWIGLU_LIMIT = 7.0
SWIGLU_ALPHA = 1.702

TM = 512
TSCAN = 256
TB = 512
RUN_BITS = 7
TAIL_BITS = 6
ZR = TOP_K * 512 + N_EXPERTS * 8
LANES = 128
VMEM_LIMIT = 56 * 1024 * 1024

NT_DIMS = (((1,), (1,)), ((), ()))


def _cparams(*sem):
    return pltpu.CompilerParams(dimension_semantics=sem, vmem_limit_bytes=VMEM_LIMIT)


def _rms(x):
    return x * lax.rsqrt(jnp.mean(x * x, axis=-1, keepdims=True) + EPS)


def _sigmoid(x):
    return 1.0 / (1.0 + jnp.exp(-x))


def _adaln_body(c_ref, w_ref, b_ref, o_ref):
    c = c_ref[...]
    cond = c * _sigmoid(c)
    o_ref[...] = jnp.dot(cond, w_ref[...], preferred_element_type=F32,
                         precision=lax.Precision.HIGHEST) + b_ref[...]


def adaln(cond8, ada_w, ada_b):
    depth, d, n = ada_w.shape
    tn = 1536
    return pl.pallas_call(
        _adaln_body,
        grid=(depth, n // tn),
        in_specs=[pl.BlockSpec((8, d), lambda l, j: (0, 0)),
                  pl.BlockSpec((None, d, tn), lambda l, j: (l, 0, j)),
                  pl.BlockSpec((None, 1, tn), lambda l, j: (l, 0, j))],
        out_specs=pl.BlockSpec((None, 8, tn), lambda l, j: (l, 0, j)),
        out_shape=jax.ShapeDtypeStruct((depth, 8, n), F32),
        compiler_params=_cparams("parallel", "parallel"),
        name="adaln",
    )(cond8, ada_w, ada_b.reshape(depth, 1, n))


def _tile_rows(refs, n_lat):
    if len(refs) == 1:
        return refs[0][...]
    return jnp.where(pl.program_id(0) < n_lat, refs[0][...], refs[1][...])


def _inproj_body(*refs, n_x, n_lat):
    (g_ref, sh_ref, sc_ref, w_ref, cs_ref, sa_ref, sb_ref, q_ref, kv_ref, lru_ref, hg_ref) = refs[n_x:]
    h = _rms(_tile_rows(refs[:n_x], n_lat)) * g_ref[...]
    h = h * (1.0 + sc_ref[...]) + sh_ref[...]
    p = jnp.dot(h.astype(BF16), w_ref[...], preferred_element_type=F32)
    cs, sa, sb = cs_ref[...], sa_ref[...], sb_ref[...]

    def rope(t):
        return t * cs + pltpu.roll(t, LANES - 16, 1) * sa + pltpu.roll(t, 16, 1) * sb

    for j in range(4):
        q_ref[:, j * LANES:(j + 1) * LANES] = (
            rope(p[:, j * LANES:(j + 1) * LANES]) * (HEAD_DIM ** -0.5)).astype(BF16)
    k = rope(p[:, 512:640])
    v = p[:, 640:768]
    lo = lax.broadcasted_iota(I32, k.shape, 1) < HEAD_DIM
    kr = pltpu.roll(k, HEAD_DIM, 1)
    vr = pltpu.roll(v, HEAD_DIM, 1)
    kv_ref[:, 0:128] = jnp.where(lo, k, kr).astype(BF16)
    kv_ref[:, 128:256] = jnp.where(lo, kr, k).astype(BF16)
    kv_ref[:, 256:384] = jnp.where(lo, v, vr).astype(BF16)
    kv_ref[:, 384:512] = jnp.where(lo, vr, v).astype(BF16)
    lru_ref[...] = p[:, 768:1280]
    hg_ref[...] = p[:, 1280:2560]


def _stream_specs(xs, n_lat, d):
    if len(xs) == 1:
        return [pl.BlockSpec((TM, d), lambda i: (i, 0))]
    return [pl.BlockSpec((TM, d), lambda i: (jnp.minimum(i, n_lat - 1), 0)),
            pl.BlockSpec((TM, d), lambda i: (0, 0))]


def inproj(xs, g, shift, scale, w_bf, rope_c, rope_a, rope_b, *, tiles_per_batch, n_seg):
    d = xs[0].shape[1]
    n_lat = tiles_per_batch * (n_seg - 1)
    n_tiles = n_lat + 1
    r = n_tiles * TM

    def seg(i):
        return (jnp.minimum(i // tiles_per_batch, n_seg - 1), 0, 0)

    def rope_idx(i):
        return (jnp.where(i < n_lat, i % tiles_per_batch, tiles_per_batch), 0)

    row = lambda w: pl.BlockSpec((TM, w), lambda i: (i, 0))
    return pl.pallas_call(
        functools.partial(_inproj_body, n_x=len(xs), n_lat=n_lat),
        grid=(n_tiles,),
        in_specs=_stream_specs(xs, n_lat, d) + [
                  pl.BlockSpec((1, d), lambda i: (0, 0)),
                  pl.BlockSpec((None, 1, d), seg),
                  pl.BlockSpec((None, 1, d), seg),
                  pl.BlockSpec(w_bf.shape, lambda i: (0, 0)),
                  pl.BlockSpec((TM, LANES), rope_idx),
                  pl.BlockSpec((TM, LANES), rope_idx),
                  pl.BlockSpec((TM, LANES), rope_idx)],
        out_specs=[row(512), row(512), row(512), row(1280)],
        out_shape=[jax.ShapeDtypeStruct((r, 512), BF16),
                   jax.ShapeDtypeStruct((r, 512), BF16),
                   jax.ShapeDtypeStruct((r, 512), F32),
                   jax.ShapeDtypeStruct((r, 1280), F32)],
        compiler_params=_cparams("parallel"),
        name="inproj",
    )(*xs, g, shift, scale, w_bf, rope_c, rope_a, rope_b)


def _softmax_pv(s_parts, v_parts, sink_col):
    m = sink_col
    for s in s_parts:
        m = jnp.maximum(m, jnp.max(s, axis=-1, keepdims=True))
    l = jnp.exp(sink_col - m)
    o = None
    for s, v in zip(s_parts, v_parts):
        p = jnp.exp(s - m)
        l = l + jnp.sum(p, axis=-1, keepdims=True)
        pv = jnp.dot(p.astype(BF16), v, preferred_element_type=F32)
        o = pv if o is None else o + pv
    return o / l


def _attn_body(sink_ref, q_ref, kvm_ref, kvp_ref, kvn_ref, kvc_ref, o_ref, *,
               tiles_per_batch, n_lat, ctx_len, n_batch):
    i = pl.program_id(0)
    blk = WINDOW
    nsub = TM // blk

    def stacked_q(q_sub, g):
        rows = q_sub.shape[0]
        lo = lax.broadcasted_iota(I32, (rows, LANES), 1) < HEAD_DIM
        parts = []
        for jq in (2 * g, 2 * g + 1):
            qg = q_sub[:, jq * LANES:(jq + 1) * LANES]
            parts.append(jnp.where(lo, qg, jnp.zeros_like(qg)))
            parts.append(jnp.where(lo, jnp.zeros_like(qg), qg))
        return jnp.concatenate(parts, axis=0)

    def sink_column(g, rows):
        blk_id = lax.broadcasted_iota(I32, (4 * rows, 1), 0) // rows
        col = jnp.full((4 * rows, 1), sink_ref[4 * g + 3], F32)
        for b in range(3):
            col = jnp.where(blk_id == b, sink_ref[4 * g + b], col)
        return col

    def write_heads(o, g, rows, row0):
        lo = lax.broadcasted_iota(I32, (rows, LANES), 1) < HEAD_DIM
        for t, jq in enumerate((2 * g, 2 * g + 1)):
            o_e0 = o[(2 * t) * rows:(2 * t + 1) * rows]
            o_e1 = o[(2 * t + 1) * rows:(2 * t + 2) * rows]
            o_ref[row0:row0 + rows, jq * LANES:(jq + 1) * LANES] = (
                jnp.where(lo, o_e0, o_e1).astype(BF16))

    @pl.when(i < n_lat)
    def _latent():
        j = i % tiles_per_batch
        r_io = lax.broadcasted_iota(I32, (4 * blk, blk), 0) % blk
        c_io = lax.broadcasted_iota(I32, (4 * blk, blk), 1)
        for sb in range(nsub):
            q_sub = q_ref[sb * blk:(sb + 1) * blk, :]
            has_prev = jnp.logical_or(j > 0, sb > 0)
            has_next = jnp.logical_or(j < tiles_per_batch - 1, sb < nsub - 1)
            m_prev = jnp.logical_and(c_io >= r_io, has_prev)
            m_next = jnp.logical_and(c_io <= r_io, has_next)
            for g in range(2):
                kc, vc = g * LANES, (2 + g) * LANES

                def kv_block(b, col):
                    if b < 0:
                        return kvp_ref[:, col:col + LANES]
                    if b >= nsub:
                        return kvn_ref[:, col:col + LANES]
                    return kvm_ref[b * blk:(b + 1) * blk, col:col + LANES]

                qs = stacked_q(q_sub, g)
                s_prev = lax.dot_general(qs, kv_block(sb - 1, kc), NT_DIMS, preferred_element_type=F32)
                s_own = lax.dot_general(qs, kv_block(sb, kc), NT_DIMS, preferred_element_type=F32)
                s_next = lax.dot_general(qs, kv_block(sb + 1, kc), NT_DIMS, preferred_element_type=F32)
                s_ctx = lax.dot_general(qs, kvc_ref[:, kc:kc + LANES], NT_DIMS, preferred_element_type=F32)
                s_prev = jnp.where(m_prev, s_prev, NEG_INF)
                s_next = jnp.where(m_next, s_next, NEG_INF)
                o = _softmax_pv(
                    [s_prev, s_own, s_next, s_ctx],
                    [kv_block(sb - 1, vc), kv_block(sb, vc), kv_block(sb + 1, vc),
                     kvc_ref[:, vc:vc + LANES]],
                    sink_column(g, blk))
                write_heads(o, g, blk, sb * blk)

    @pl.when(i >= n_lat)
    def _context():
        for bb in range(n_batch):
            r0 = bb * ctx_len
            q_sub = q_ref[r0:r0 + ctx_len, :]
            for g in range(2):
                kc, vc = g * LANES, (2 + g) * LANES
                qs = stacked_q(q_sub, g)
                s = lax.dot_general(qs, kvm_ref[r0:r0 + ctx_len, kc:kc + LANES], NT_DIMS,
                                    preferred_element_type=F32)
                o = _softmax_pv([s], [kvm_ref[r0:r0 + ctx_len, vc:vc + LANES]],
                                sink_column(g, ctx_len))
                write_heads(o, g, ctx_len, r0)


def attention(q, kv, sink, *, tiles_per_batch, n_batch, ctx_len):
    r = q.shape[0]
    n_tiles = r // TM
    n_lat = tiles_per_batch * n_batch
    assert n_batch * ctx_len == TM and n_tiles == n_lat + 1
    sub = TM // WINDOW
    n_blk128 = r // WINDOW
    ctx_blk0 = (n_lat * TM) // ctx_len

    def prev_idx(i, s):
        return (jnp.maximum(i * sub - 1, 0), 0)

    def next_idx(i, s):
        return (jnp.minimum(i * sub + sub, n_blk128 - 1), 0)

    def ctx_idx(i, s):
        return (ctx_blk0 + jnp.minimum(i // tiles_per_batch, n_batch - 1), 0)

    body = functools.partial(_attn_body, tiles_per_batch=tiles_per_batch, n_lat=n_lat,
                             ctx_len=ctx_len, n_batch=n_batch)
    return pl.pallas_call(
        body,
        grid_spec=pltpu.PrefetchScalarGridSpec(
            num_scalar_prefetch=1,
            grid=(n_tiles,),
            in_specs=[pl.BlockSpec((TM, 512), lambda i, s: (i, 0)),
                      pl.BlockSpec((TM, 512), lambda i, s: (i, 0)),
                      pl.BlockSpec((WINDOW, 512), prev_idx),
                      pl.BlockSpec((WINDOW, 512), next_idx),
                      pl.BlockSpec((ctx_len, 512), ctx_idx)],
            out_specs=pl.BlockSpec((TM, 512), lambda i, s: (i, 0))),
        out_shape=jax.ShapeDtypeStruct((r, 512), BF16),
        compiler_params=_cparams("parallel"),
        name="attention",
    )(sink, q, kv, kv, kv, kv)


def _scan_maps(n_batch, seq, ctx_len):
    assert ctx_len == TSCAN
    nl = seq // TSCAN
    ctx0 = (n_batch * seq) // TSCAN

    def fwd(b, c):
        return jnp.where(c == 0, ctx0 + b, b * nl + c - 1)

    def bwd(b, c):
        return jnp.where(c == 0, ctx0 + b, b * nl + nl - c)

    return nl, fwd, bwd


def _lru_dir(x_ref, xp_ref, xn_ref, has_prev, has_next, cw_ref, cb_ref, w_ref, br_ref, bi_ref,
             lam_ref, h_ref, a_scr, b_scr, out_ref, reverse):
    t = TSCAN
    xp = jnp.where(has_prev, xp_ref[...], 0.0)
    xn = jnp.where(has_next, xn_ref[...], 0.0)
    xpad = jnp.concatenate([xp, x_ref[...], xn], axis=0)
    n = t + 16
    cw = cw_ref[...]
    u = cb_ref[...] + jnp.zeros((t, xpad.shape[1]), F32)
    for j in range(4):
        sh = (2 - j) % n
        rolled = xpad if sh == 0 else pltpu.roll(xpad, sh, 0)
        u = u + rolled[8:8 + t] * cw[j:j + 1, :]
    gates = jnp.dot(u.astype(BF16), w_ref[...], preferred_element_type=F32)
    c = u.shape[1]
    r = _sigmoid(gates[:, :c] + br_ref[...])
    ig = _sigmoid(gates[:, c:] + bi_ref[...])
    nl = -lam_ref[...]
    softplus = jnp.maximum(nl, 0.0) + jnp.log1p(jnp.exp(-jnp.abs(nl)))
    log_a = (-LRU_C) * r * softplus
    a = jnp.exp(log_a)
    mult = jnp.sqrt(jnp.maximum(1.0 - a * a, SQRT_FLOOR))
    bv = mult * (ig * u)

    g8 = t // 8
    a3 = a.reshape(g8, 8, c)
    b3 = bv.reshape(g8, 8, c)
    r8 = lax.broadcasted_iota(I32, (g8, 8, c), 1)
    for s in (1, 2, 4):
        if reverse:
            a_sh, b_sh, ok = pltpu.roll(a3, 8 - s, 1), pltpu.roll(b3, 8 - s, 1), r8 < 8 - s
        else:
            a_sh, b_sh, ok = pltpu.roll(a3, s, 1), pltpu.roll(b3, s, 1), r8 >= s
        b3 = jnp.where(ok, a3 * b_sh + b3, b3)
        a3 = jnp.where(ok, a3 * a_sh, a3)
    a_scr[...] = a3.reshape(t, c)
    b_scr[...] = b3.reshape(t, c)
    edge = 0 if reverse else 7

    def group(gi, carry):
        g = (g8 - 1 - gi) if reverse else gi
        rows = pl.ds(pl.multiple_of(g * 8, 8), 8)
        hg = b_scr[rows, :] + a_scr[rows, :] * carry
        out_ref[rows, :] = hg
        return jnp.broadcast_to(hg[edge:edge + 1, :], hg.shape)

    h_ref[...] = lax.fori_loop(0, g8, group, h_ref[...], unroll=4)


def _lru_body(xf_ref, xfp_ref, xfn_ref, xb_ref, xbp_ref, xbn_ref, cw_ref, cb_ref, w_ref, br_ref,
              bi_ref, lam_ref, hf_ref, hb_ref, st_ref, a_scr, b_scr, *, nl):
    c = pl.program_id(1)

    @pl.when(c == 0)
    def _init():
        st_ref[...] = jnp.zeros_like(st_ref)

    lat = c > 0
    _lru_dir(xf_ref, xfp_ref, xfn_ref, jnp.logical_and(lat, c > 1), jnp.logical_and(lat, c < nl),
             cw_ref, cb_ref, w_ref.at[0], br_ref.at[0], bi_ref.at[0], lam_ref.at[0],
             st_ref.at[0], a_scr, b_scr, hf_ref, False)
    _lru_dir(xb_ref, xbp_ref, xbn_ref, jnp.logical_and(lat, c < nl), jnp.logical_and(lat, c > 1),
             cw_ref, cb_ref, w_ref.at[1], br_ref.at[1], bi_ref.at[1], lam_ref.at[1],
             st_ref.at[1], a_scr, b_scr, hb_ref, True)


def lru_scan(lru, conv_w, conv_b, w_gates, br, bi, lam, *, n_batch, seq, ctx_len):
    r = lru.shape[0]
    c = lru.shape[1] // 2
    nl, fwd, bwd = _scan_maps(n_batch, seq, ctx_len)
    per = TSCAN // 8
    n8 = r // 8

    def main(m):
        return pl.BlockSpec((TSCAN, c), lambda b, s: (m(b, s), 0))

    def prev(m):
        return pl.BlockSpec((8, c), lambda b, s: (jnp.maximum(m(b, s) * per - 1, 0), 0))

    def nxt(m):
        return pl.BlockSpec((8, c), lambda b, s: (jnp.minimum(m(b, s) * per + per, n8 - 1), 0))

    full = lambda a: pl.BlockSpec(a.shape, lambda b, s: (0,) * a.ndim)
    return pl.pallas_call(
        functools.partial(_lru_body, nl=nl),
        grid=(n_batch, nl + 1),
        in_specs=[main(fwd), prev(fwd), nxt(fwd), main(bwd), prev(bwd), nxt(bwd),
                  full(conv_w), full(conv_b), full(w_gates), full(br), full(bi), full(lam)],
        out_specs=[main(fwd), main(bwd)],
        out_shape=[jax.ShapeDtypeStruct((r, c), F32)] * 2,
        scratch_shapes=[pltpu.VMEM((2, 8, c), F32), pltpu.VMEM((TSCAN, c), F32),
                        pltpu.VMEM((TSCAN, c), F32)],
        compiler_params=_cparams("parallel", "arbitrary"),
        name="lru_scan",
    )(lru, lru, lru, lru, lru, lru, conv_w, conv_b, w_gates, br, bi, lam)


def _hgrn_dir(q_ref, z_ref, v_ref, lb_ref, st_ref, out_ref, reverse):
    t = TSCAN
    nch = t // HG_CHUNK
    q = q_ref[...]
    z = z_ref[...]
    v = v_ref[...]
    lb = lb_ref[...]
    w = q.shape[1]
    log_f = jnp.log(lb + (1.0 - lb) * _sigmoid(z))
    k = (1.0 - lb) * _sigmoid(-z)

    rc = lax.broadcasted_iota(I32, (t, w), 0) % HG_CHUNK
    b = log_f
    s = 1
    while s < HG_CHUNK:
        if reverse:
            b = b + jnp.where(rc < HG_CHUNK - s, pltpu.roll(b, t - s, 0), 0.0)
        else:
            b = b + jnp.where(rc >= s, pltpu.roll(b, s, 0), 0.0)
        s *= 2
    tot = jnp.sum(log_f.reshape(nch, HG_CHUNK, w), axis=1, keepdims=True)
    b_last = jnp.broadcast_to(tot, (nch, HG_CHUNK, w)).reshape(t, w)
    b_half = 0.5 * b_last
    q_i = (q * jnp.exp(b - b_half)).astype(BF16)
    k_i = (k * jnp.exp(b_half - b)).astype(BF16)
    q_s = (q * jnp.exp(b)).astype(BF16)
    k_e = (k * jnp.exp(b_last - b)).astype(BF16)
    decay = jnp.exp(tot.reshape(nch, w))
    v_bf = v.astype(BF16)

    ri = lax.broadcasted_iota(I32, (t, t), 0)
    ci = lax.broadcasted_iota(I32, (t, t), 1)
    same = (ri // HG_CHUNK) == (ci // HG_CHUNK)
    causal = jnp.logical_and(same, (ci >= ri) if reverse else (ci <= ri))
    lo = lax.broadcasted_iota(I32, (t, LANES), 1) < HEAD_DIM
    chunk_of_row = lax.broadcasted_iota(I32, (t, LANES), 0) // HG_CHUNK
    bd_r = lax.broadcasted_iota(I32, (LANES, LANES), 0) // HEAD_DIM
    bd_c = lax.broadcasted_iota(I32, (LANES, LANES), 1) // HEAD_DIM
    block_diag = bd_r == bd_c

    for p in range(w // LANES):
        cols = slice(p * LANES, (p + 1) * LANES)
        qi_p, ki_p, qs_p, ke_p, v_p = q_i[:, cols], k_i[:, cols], q_s[:, cols], k_e[:, cols], v_bf[:, cols]
        o_half = []
        for e in range(2):
            qm = jnp.where(lo if e == 0 else jnp.logical_not(lo), qi_p, jnp.zeros_like(qi_p))
            att = lax.dot_general(qm, ki_p, NT_DIMS, preferred_element_type=F32)
            att = jnp.where(causal, att, 0.0).astype(BF16)
            o_half.append(jnp.dot(att, v_p, preferred_element_type=F32))
        out_ref[:, cols] = jnp.where(lo, o_half[0], o_half[1])

        v_t = v[:, cols].T.astype(BF16)
        st = st_ref[p]
        order = range(nch - 1, -1, -1) if reverse else range(nch)
        for n in order:
            rows = slice(n * HG_CHUNK, (n + 1) * HG_CHUNK)
            o_inter = lax.dot_general(qs_p[rows], st.astype(BF16), NT_DIMS, preferred_element_type=F32)
            out_ref[rows, cols] = out_ref[rows, cols] + o_inter
            ke_n = jnp.where(chunk_of_row == n, ke_p, jnp.zeros_like(ke_p))
            kv_t = jnp.dot(v_t, ke_n, preferred_element_type=F32)
            st = decay[n:n + 1, cols] * st + jnp.where(block_diag, kv_t, 0.0)
        st_ref[p] = st


def _hgrn_body(qf_ref, zf_ref, vf_ref, qb_ref, zb_ref, vb_ref, lb_ref, of_ref, ob_ref, st_ref):
    c = pl.program_id(1)

    @pl.when(c == 0)
    def _init():
        st_ref[...] = jnp.zeros_like(st_ref)

    _hgrn_dir(qf_ref, zf_ref, vf_ref, lb_ref.at[0], st_ref.at[0], of_ref, False)
    _hgrn_dir(qb_ref, zb_ref, vb_ref, lb_ref.at[1], st_ref.at[1], ob_ref, True)


def hgrn_scan(hg, lower_bounds, *, n_batch, seq, ctx_len):
    r = hg.shape[0]
    w = hg.shape[1] // 5
    nl, fwd, bwd = _scan_maps(n_batch, seq, ctx_len)

    def col(m, j):
        return pl.BlockSpec((TSCAN, w), lambda b, s: (m(b, s), j))

    lb3 = lower_bounds.reshape(2, 1, w)
    return pl.pallas_call(
        _hgrn_body,
        grid=(n_batch, nl + 1),
        in_specs=[col(fwd, 0), col(fwd, 1), col(fwd, 3), col(bwd, 0), col(bwd, 2), col(bwd, 3),
                  pl.BlockSpec(lb3.shape, lambda b, s: (0, 0, 0))],
        out_specs=[col(fwd, 0), col(bwd, 0)],
        out_shape=[jax.ShapeDtypeStruct((r, w), F32)] * 2,
        scratch_shapes=[pltpu.VMEM((2, w // LANES, LANES, LANES), F32)],
        compiler_params=_cparams("parallel", "arbitrary"),
        name="hgrn_scan",
    )(hg, hg, hg, hg, hg, hg, lb3)


def _outproj_router_body(*refs, n_x, n_lat):
    (att_ref, hf_ref, hb_ref, lg_ref, of_ref, ob_ref, gg_ref, hn_ref,
     wo_ref, gm_ref, sh_ref, sc_ref, n2_ref, rw_ref, rb_ref,
     xo_ref, h2_ref, idx_ref, gate_ref, rank_ref, cnt_ref) = refs[n_x:]
    lru_y = (hf_ref[...] + hb_ref[...]) * jax.nn.gelu(lg_ref[...], approximate=True)
    o = of_ref[...] + ob_ref[...]
    w = o.shape[1]
    gr = lax.broadcasted_iota(I32, (w, w), 0) // HEAD_DIM
    gc = lax.broadcasted_iota(I32, (w, w), 1) // HEAD_DIM
    head_mean = jnp.where(gr == gc, 1.0 / HEAD_DIM, 0.0).astype(F32)
    ms = jnp.dot(o * o, head_mean, preferred_element_type=F32, precision=lax.Precision.HIGHEST)
    gg = gg_ref[...]
    hg_y = o * lax.rsqrt(ms + EPS) * hn_ref[...] * (gg * _sigmoid(gg))
    y = jnp.dot(att_ref[...], wo_ref[0:512, :], preferred_element_type=F32)
    y = y + jnp.dot(lru_y.astype(BF16), wo_ref[512:768, :], preferred_element_type=F32)
    y = y + jnp.dot(hg_y.astype(BF16), wo_ref[768:1024, :], preferred_element_type=F32)
    x = _tile_rows(refs[:n_x], n_lat) + gm_ref[...] * y
    xo_ref[...] = x
    h2 = _rms(x) * n2_ref[...]
    h2 = h2 * (1.0 + sc_ref[...]) + sh_ref[...]
    h2_ref[...] = h2

    logits = lax.dot_general(rw_ref[...], h2, NT_DIMS, preferred_element_type=F32,
                             precision=lax.Precision.HIGHEST) + rb_ref[...]
    ne, tm = logits.shape
    e_io = lax.broadcasted_iota(I32, (ne, tm), 0).astype(F32)
    work = logits
    vals, hots = [], []
    for k in range(TOP_K):
        m = jnp.max(work, axis=0, keepdims=True)
        idx = jnp.min(jnp.where(work == m, e_io, float(ne)), axis=0, keepdims=True)
        hot = e_io == idx
        vals.append(m)
        hots.append(hot)
        idx_ref[k:k + 1, :] = idx.astype(I32)
        work = jnp.where(hot, -jnp.inf, work)
    exps = [jnp.exp(v - vals[0]) for v in vals]
    denom = exps[0] + exps[1] + exps[2] + exps[3]
    for k in range(TOP_K):
        gate_ref[k:k + 1, :] = exps[k] / denom
    z4 = jnp.zeros((8 - TOP_K, tm), F32)
    gate_ref[TOP_K:8, :] = z4
    idx_ref[TOP_K:8, :] = z4.astype(I32)
    rank_ref[TOP_K:8, :] = z4.astype(I32)

    chosen = jnp.logical_or(jnp.logical_or(hots[0], hots[1]), jnp.logical_or(hots[2], hots[3]))
    sr = lax.broadcasted_iota(I32, (tm, tm), 0)
    sc = lax.broadcasted_iota(I32, (tm, tm), 1)
    before = jnp.where(sr < sc, 1.0, 0.0).astype(BF16)
    chosen_f = jnp.where(chosen, 1.0, 0.0)
    prefix = jnp.dot(chosen_f.astype(BF16), before, preferred_element_type=F32)
    for k in range(TOP_K):
        rk = jnp.sum(jnp.where(hots[k], prefix, 0.0), axis=0, keepdims=True)
        rank_ref[k:k + 1, :] = rk.astype(I32)
    cnt_ref[...] = jnp.broadcast_to(jnp.sum(chosen_f, axis=1, keepdims=True), cnt_ref.shape)


def outproj_router(xs, att, hf, hb, lru, of, ob, hg, hn_g, wo_bf, gate_msa, shift_mlp, scale_mlp,
                   n2_g, rw_t, rb, *, tiles_per_batch, n_seg):
    d = xs[0].shape[1]
    n_lat = tiles_per_batch * (n_seg - 1)
    n_tiles = n_lat + 1
    r = n_tiles * TM

    def seg(i):
        return (jnp.minimum(i // tiles_per_batch, n_seg - 1), 0, 0)

    row = lambda w, j=0: pl.BlockSpec((TM, w), lambda i: (i, j))
    full = lambda a: pl.BlockSpec(a.shape, lambda i: (0,) * a.ndim)
    modspec = pl.BlockSpec((None, 1, d), seg)
    lane_out = pl.BlockSpec((8, TM), lambda i: (0, i))
    return pl.pallas_call(
        functools.partial(_outproj_router_body, n_x=len(xs), n_lat=n_lat),
        grid=(n_tiles,),
        in_specs=_stream_specs(xs, n_lat, d) + [
                  row(512), row(256), row(256), row(256, 1), row(256), row(256),
                  row(256, 4), full(hn_g), full(wo_bf), modspec, modspec, modspec, full(n2_g),
                  full(rw_t), full(rb)],
        out_specs=[row(d), row(d), lane_out, lane_out, lane_out,
                   pl.BlockSpec((None, N_EXPERTS, LANES), lambda i: (i, 0, 0))],
        out_shape=[jax.ShapeDtypeStruct((r, d), F32), jax.ShapeDtypeStruct((r, d), F32),
                   jax.ShapeDtypeStruct((8, r), I32), jax.ShapeDtypeStruct((8, r), F32),
                   jax.ShapeDtypeStruct((8, r), I32),
                   jax.ShapeDtypeStruct((n_tiles, N_EXPERTS, LANES), F32)],
        compiler_params=_cparams("parallel"),
        name="outproj_router",
    )(*xs, att, hf, hb, lru, of, ob, hg, hn_g, wo_bf, gate_msa, shift_mlp, scale_mlp, n2_g, rw_t, rb)


def _local_pos(idx_ref, rank_ref, toff_ref, k):
    ne = toff_ref.shape[0]
    tm = idx_ref.shape[1]
    e_io = lax.broadcasted_iota(I32, (ne, tm), 0)
    off = jnp.sum(jnp.where(e_io == idx_ref[k:k + 1, :], toff_ref[...], 0.0), axis=0, keepdims=True)
    return off + rank_ref[k:k + 1, :].astype(F32)


def _run_copies(tab_ref, tile, make_copy, sems, slot, start):
    base = tile * LANES

    def per_expert(e, carry):
        src8 = tab_ref[base + e]
        dst8 = tab_ref[base + N_EXPERTS + e]
        n8 = tab_ref[base + 2 * N_EXPERTS + e]
        for b in range(RUN_BITS):
            size = 8 << b
            off8 = (n8 >> (b + 1)) << (b + 1)

            @pl.when(((n8 >> b) & 1) == 1)
            def _():
                cp = make_copy(pl.multiple_of((src8 + off8) * 8, 8),
                               pl.multiple_of((dst8 + off8) * 8, 8), size, sems.at[slot, b])
                if start:
                    cp.start()
                else:
                    cp.wait()
        return carry

    lax.fori_loop(0, N_EXPERTS, per_expert, 0)


def _fill_copies(fill_ref, zero_scr, xs_ref, sems, start):
    def go(cp):
        if start:
            cp.start()
        else:
            cp.wait()

    def per_expert(e, carry):
        t8 = fill_ref[e]
        n8 = fill_ref[N_EXPERTS + e]
        for b in range(TAIL_BITS):
            size = 8 << b
            off8 = (n8 >> (b + 1)) << (b + 1)

            @pl.when(((n8 >> b) & 1) == 1)
            def _():
                go(pltpu.make_async_copy(
                    zero_scr.at[pl.ds(0, size), :],
                    xs_ref.at[pl.ds(pl.multiple_of((t8 + off8) * 8, 8), size), :], sems.at[0, b]))
        return carry

    lax.fori_loop(0, N_EXPERTS, per_expert, 0)

    def per_block(j, carry):
        go(pltpu.make_async_copy(zero_scr, xs_ref.at[pl.ds(pl.multiple_of(j * TB, TB), TB), :],
                                 sems.at[0, TAIL_BITS]))
        return carry

    lax.fori_loop(fill_ref[2 * N_EXPERTS], xs_ref.shape[0] // TB, per_block, 0)


def _dispatch_body(fill_ref, tab_ref, h_ref, idx_ref, rank_ref, toff_ref, xs_ref, z_scr, zero_scr, sems):
    tm = h_ref.shape[0]
    i = pl.program_id(0)
    slot = i % 2

    @pl.when(i == 0)
    def _fill():
        zero_scr[...] = jnp.zeros_like(zero_scr)
        _fill_copies(fill_ref, zero_scr, xs_ref, sems, True)
        _fill_copies(fill_ref, zero_scr, xs_ref, sems, False)

    r_io = lax.broadcasted_iota(I32, (ZR, tm), 0).astype(F32)
    hit = r_io == _local_pos(idx_ref, rank_ref, toff_ref, 0)
    for k in range(1, TOP_K):
        hit = jnp.logical_or(hit, r_io == _local_pos(idx_ref, rank_ref, toff_ref, k))
    perm = jnp.where(hit, 1.0, 0.0).astype(BF16)
    z_scr[slot] = jnp.dot(perm, h_ref[...].astype(BF16), preferred_element_type=F32)

    def copy_from(buf):
        def make_copy(s, d, size, sem):
            return pltpu.make_async_copy(z_scr.at[buf, pl.ds(s, size), :],
                                         xs_ref.at[pl.ds(d, size), :], sem)
        return make_copy

    _run_copies(tab_ref, i, copy_from(slot), sems, slot, True)

    @pl.when(i > 0)
    def _drain_previous():
        _run_copies(tab_ref, i - 1, copy_from(1 - slot), sems, 1 - slot, False)

    @pl.when(i == pl.num_programs(0) - 1)
    def _drain_last():
        _run_copies(tab_ref, i, copy_from(slot), sems, slot, False)


def moe_dispatch(h2, idx, rank, toff_col, tab, fill, p_rows):
    r, d = h2.shape
    n_tiles = r // TM
    lane_in = pl.BlockSpec((8, TM), lambda i, f, t: (0, i))
    return pl.pallas_call(
        _dispatch_body,
        grid_spec=pltpu.PrefetchScalarGridSpec(
            num_scalar_prefetch=2,
            grid=(n_tiles,),
            in_specs=[pl.BlockSpec((TM, d), lambda i, f, t: (i, 0)),
                      lane_in, lane_in,
                      pl.BlockSpec((None, N_EXPERTS, 1), lambda i, f, t: (i, 0, 0))],
            out_specs=pl.BlockSpec(memory_space=pl.ANY),
            scratch_shapes=[pltpu.VMEM((2, ZR, d), F32), pltpu.VMEM((TB, d), F32),
                            pltpu.SemaphoreType.DMA((2, RUN_BITS))]),
        out_shape=jax.ShapeDtypeStruct((p_rows, d), F32),
        compiler_params=_cparams("arbitrary"),
        name="moe_dispatch",
    )(fill, tab, h2, idx, rank, toff_col)


def _expert_body(be_ref, nu_ref, x_ref, wgu_ref, bgu_ref, wd_ref, bd_ref, y_ref, wgu_bf, wd_bf):
    i = pl.program_id(0)

    @pl.when(i < nu_ref[0])
    def _run():
        new_expert = jnp.logical_or(i == 0, be_ref[i] != be_ref[jnp.maximum(i - 1, 0)])

        @pl.when(new_expert)
        def _cast():
            rows = 128

            def chunk(j, carry):
                sl = pl.ds(pl.multiple_of(j * rows, rows), rows)
                wgu_bf[sl, :] = wgu_ref[sl, :].astype(BF16)
                wd_bf[sl, :] = wd_ref[sl, :].astype(BF16)
                return carry

            lax.fori_loop(0, wgu_ref.shape[0] // rows, chunk, 0)

        dff = wd_ref.shape[0]
        gu = jnp.dot(x_ref[...].astype(BF16), wgu_bf[...], preferred_element_type=F32) + bgu_ref[...]
        gate = jnp.minimum(gu[:, :dff], SWIGLU_LIMIT)
        up = jnp.clip(gu[:, dff:], -SWIGLU_LIMIT, SWIGLU_LIMIT)
        glu = gate * _sigmoid(SWIGLU_ALPHA * gate)
        act = ((up + 1.0) * glu).astype(BF16)
        y_ref[...] = jnp.dot(act, wd_bf[...], preferred_element_type=F32) + bd_ref[...]

    @pl.when(i >= nu_ref[0])
    def _unused():
        y_ref[...] = jnp.zeros_like(y_ref)


def moe_experts(xs_sorted, blk_e, n_used, wgu, bgu, wd, bd, layer):
    p_rows, d = xs_sorted.shape
    depth, ne, _, dgu = wgu.shape
    dff = wd.shape[2]
    assert dff == d

    def expert(i, be, nu):
        return (layer * ne + be[i], 0, 0)

    return pl.pallas_call(
        _expert_body,
        grid_spec=pltpu.PrefetchScalarGridSpec(
            num_scalar_prefetch=2,
            grid=(p_rows // TB,),
            in_specs=[pl.BlockSpec((TB, d), lambda i, be, nu: (i, 0)),
                      pl.BlockSpec((None, d, dgu), expert),
                      pl.BlockSpec((None, 1, dgu), expert),
                      pl.BlockSpec((None, dff, d), expert),
                      pl.BlockSpec((None, 1, d), expert)],
            out_specs=pl.BlockSpec((TB, d), lambda i, be, nu: (i, 0)),
            scratch_shapes=[pltpu.VMEM((d, dgu), BF16), pltpu.VMEM((dff, d), BF16)]),
        out_shape=jax.ShapeDtypeStruct((p_rows, d), F32),
        compiler_params=_cparams("arbitrary"),
        name="moe_experts",
    )(blk_e, n_used, xs_sorted, wgu.reshape(depth * ne, d, dgu), bgu.reshape(depth * ne, 1, dgu),
      wd.reshape(depth * ne, dff, d), bd.reshape(depth * ne, 1, d))


def _combine_body(tab_ref, x_ref, idx_ref, rank_ref, gate_ref, toff_ref, gm_ref, fg_ref, ys_ref,
                  o_ref, zy_scr, sems, *, final):
    tm = x_ref.shape[0]
    i = pl.program_id(0)
    slot = i % 2

    def copy_into(buf):
        def make_copy(s, d, size, sem):
            return pltpu.make_async_copy(ys_ref.at[pl.ds(d, size), :],
                                         zy_scr.at[buf, pl.ds(s, size), :], sem)
        return make_copy

    def fetch(tile, buf):
        zy_scr[buf, TOP_K * tm:ZR, :] = jnp.zeros((ZR - TOP_K * tm, zy_scr.shape[2]), F32)
        _run_copies(tab_ref, tile, copy_into(buf), sems, buf, True)

    @pl.when(i == 0)
    def _first():
        fetch(0, 0)

    @pl.when(i + 1 < pl.num_programs(0))
    def _prefetch_next():
        fetch(i + 1, 1 - slot)

    pos = [_local_pos(idx_ref, rank_ref, toff_ref, k) for k in range(TOP_K)]
    packed = jnp.concatenate(pos + [gate_ref[0:TOP_K, :], jnp.zeros((LANES - 2 * TOP_K, tm), F32)],
                             axis=0)
    cols = packed.T
    c_io = lax.broadcasted_iota(I32, (tm, ZR), 1).astype(F32)
    weights = jnp.zeros((tm, ZR), F32)
    for k in range(TOP_K):
        weights = jnp.where(c_io == cols[:, k:k + 1], cols[:, TOP_K + k:TOP_K + k + 1], weights)
    _run_copies(tab_ref, i, copy_into(slot), sems, slot, False)
    acc = jnp.dot(weights.astype(BF16), zy_scr[slot].astype(BF16), preferred_element_type=F32)
    x = x_ref[...] + gm_ref[...] * acc
    if final:
        x = _rms(x) * fg_ref[...]
    o_ref[...] = x


def moe_combine(xs, idx, rank, gates, toff_col, tab, ys, gate_mlp, final_g, *, tiles_per_batch,
                n_seg, n_tiles, final):
    d = xs.shape[1]

    def seg(i):
        return (jnp.minimum(i // tiles_per_batch, n_seg - 1), 0, 0)

    lane_in = pl.BlockSpec((8, TM), lambda i, t: (0, i))
    return pl.pallas_call(
        functools.partial(_combine_body, final=final),
        grid_spec=pltpu.PrefetchScalarGridSpec(
            num_scalar_prefetch=1,
            grid=(n_tiles,),
            in_specs=[pl.BlockSpec((TM, d), lambda i, t: (i, 0)),
                      lane_in, lane_in, lane_in,
                      pl.BlockSpec((None, N_EXPERTS, 1), lambda i, t: (i, 0, 0)),
                      pl.BlockSpec((None, 1, d), lambda i, t: seg(i)),
                      pl.BlockSpec((1, d), lambda i, t: (0, 0)),
                      pl.BlockSpec(memory_space=pl.ANY)],
            out_specs=pl.BlockSpec((TM, d), lambda i, t: (i, 0)),
            scratch_shapes=[pltpu.VMEM((2, ZR, d), F32), pltpu.SemaphoreType.DMA((2, RUN_BITS))]),
        out_shape=jax.ShapeDtypeStruct((n_tiles * TM, d), F32),
        compiler_params=_cparams("arbitrary"),
        name="moe_combine",
    )(tab, xs, idx, rank, gates, toff_col, gate_mlp, final_g, ys)


def _rope_tables(seq):
    nf = HEAD_DIM // 4
    pos = jnp.arange(seq)
    rows = (pos // GRID_W).astype(F32)
    cols = (pos % GRID_W).astype(F32)
    inv_freq = ROPE_BASE ** (-jnp.arange(nf, dtype=F32) / nf)
    d = jnp.arange(LANES) % HEAD_DIM
    axis = d // (2 * nf)
    half = (d // nf) % 2
    f = d % nf
    ang = jnp.where(axis[None, :] == 0, rows[:, None], cols[:, None]) * inv_freq[f][None, :]
    cs, sn = jnp.cos(ang), jnp.sin(ang)
    ca = jnp.where(half[None, :] == 0, -sn, 0.0)
    cb = jnp.where(half[None, :] == 1, sn, 0.0)
    pad1 = jnp.ones((TM, LANES), F32)
    pad0 = jnp.zeros((TM, LANES), F32)
    return (jnp.concatenate([cs, pad1]), jnp.concatenate([ca, pad0]), jnp.concatenate([cb, pad0]))


def _block_diag(w):
    n, c, _ = w.shape
    eye = jnp.eye(n, dtype=w.dtype)
    return (eye[:, None, :, None] * w[:, :, None, :]).reshape(n * c, n * c)


def _moe_layout(tile_counts, n_blocks):
    cnt8 = (tile_counts.astype(I32) + 7) // 8 * 8
    toff = jnp.cumsum(cnt8, axis=1) - cnt8
    goff = jnp.cumsum(cnt8, axis=0) - cnt8
    padded = (jnp.sum(cnt8, axis=0) + TB - 1) // TB * TB
    pends = jnp.cumsum(padded)
    dst = (pends - padded)[None, :] + goff
    n_tiles = cnt8.shape[0]
    tab = jnp.concatenate([toff, dst, cnt8, jnp.zeros((n_tiles, LANES - 3 * N_EXPERTS), I32)], axis=1) // 8
    blk_start = jnp.arange(n_blocks, dtype=I32) * TB
    blk_e = jnp.minimum(jnp.sum((pends[None, :] <= blk_start[:, None]).astype(I32), axis=1), N_EXPERTS - 1)
    n_used = (pends[-1:] // TB).astype(I32)
    total = jnp.sum(cnt8, axis=0)
    fill = jnp.concatenate([(pends - padded + total) // 8, (padded - total) // 8, n_used,
                            jnp.zeros((LANES - 2 * N_EXPERTS - 1,), I32)]).astype(I32)
    return (tab.reshape(n_tiles * LANES), toff.astype(F32).reshape(n_tiles, N_EXPERTS, 1),
            fill, blk_e.astype(I32), n_used)


def kernel(x, c, ctx, c_ctx, ada_w, ada_b, norm1_g, w_in, attn_sink, conv_w, conv_b, lru_wr, lru_br,
           lru_wi, lru_bi, lru_lambda, hgrn_lb_logits, hgrn_norm_g, w_out, norm2_g, router_w,
           router_b, moe_w_gu, moe_b_gu, moe_w_down, moe_b_down, final_g):
    n_batch, seq, d = x.shape
    ctx_len = ctx.shape[1]
    depth = ada_w.shape[0]
    assert n_batch * ctx_len == TM and seq % TM == 0 and ctx_len == TSCAN
    tiles_per_batch = seq // TM
    n_seg = n_batch + 1
    n_lat = n_batch * tiles_per_batch
    r = n_batch * seq + n_batch * ctx_len
    n_blocks = -(-(r * TOP_K + (r // TM) * N_EXPERTS * 7) // TB) + N_EXPERTS
    p_rows = n_blocks * TB

    xs = (x.reshape(n_batch * seq, d), ctx.reshape(n_batch * ctx_len, d))
    cond8 = jnp.zeros((8, d), F32).at[:n_batch].set(c).at[n_batch].set(c_ctx)
    mods = adaln(cond8, ada_w, ada_b)
    rope_c, rope_a, rope_b = _rope_tables(seq)
    lb_p = jax.nn.softmax(hgrn_lb_logits.astype(F32), axis=0)
    lower_bounds = jnp.cumsum(lb_p, axis=0) - lb_p[0]

    out = None
    for layer in range(depth):
        mod = [mods[layer, :, j * d:(j + 1) * d].reshape(8, 1, d) for j in range(6)]
        q, kv, lru, hg = inproj(xs, norm1_g[layer].reshape(1, d), mod[0], mod[1],
                                w_in[layer].astype(BF16), rope_c, rope_a, rope_b,
                                tiles_per_batch=tiles_per_batch, n_seg=n_seg)
        att = attention(q, kv, attn_sink[layer], tiles_per_batch=tiles_per_batch,
                        n_batch=n_batch, ctx_len=ctx_len)
        w_gates = jnp.stack([jnp.concatenate([_block_diag(lru_wr[layer, dd]),
                                              _block_diag(lru_wi[layer, dd])], axis=1)
                             for dd in range(2)]).astype(BF16)
        cw = lru_br.shape[-1]
        hf, hb = lru_scan(lru, conv_w[layer], conv_b[layer].reshape(1, cw), w_gates,
                          lru_br[layer].reshape(2, 1, cw), lru_bi[layer].reshape(2, 1, cw),
                          lru_lambda[layer].reshape(2, 1, cw),
                          n_batch=n_batch, seq=seq, ctx_len=ctx_len)
        of, ob = hgrn_scan(hg, lower_bounds[layer], n_batch=n_batch, seq=seq, ctx_len=ctx_len)
        x_mid, h2, idx, gates, rank, counts = outproj_router(
            xs, att, hf, hb, lru, of, ob, hg, hgrn_norm_g[layer].reshape(1, -1),
            w_out[layer].astype(BF16), mod[2], mod[3], mod[4], norm2_g[layer].reshape(1, d),
            router_w[layer].T, router_b[layer].reshape(N_EXPERTS, 1),
            tiles_per_batch=tiles_per_batch, n_seg=n_seg)
        tab, toff_col, fill, blk_e, n_used = _moe_layout(counts[:, :, 0], n_blocks)
        xs_sorted = moe_dispatch(h2, idx, rank, toff_col, tab, fill, p_rows)
        ys = moe_experts(xs_sorted, blk_e, n_used, moe_w_gu, moe_b_gu, moe_w_down, moe_b_down, layer)
        final = layer == depth - 1
        res = moe_combine(x_mid, idx, rank, gates, toff_col, tab, ys, mod[5], final_g.reshape(1, d),
                          tiles_per_batch=tiles_per_batch, n_seg=n_seg,
                          n_tiles=n_lat if final else n_lat + 1, final=final)
        if final:
            out = res.reshape(n_batch, seq, d)
        else:
            xs = (res,)
    return out
```

```python
import functools

import jax
import jax.numpy as jnp
from jax import lax
from jax.experimental import pallas as pl
from jax.experimental.pallas import tpu as pltpu

F32 = jnp.float32
BF16 = jnp.bfloat16
I32 = jnp.int32

EPS = 1e-6
NEG_INF = -1e30
GRID_W = 64
ROPE_BASE = 10000.0
HEAD_DIM = 64
N_HEADS = 8
WINDOW = 128
LRU_C = 8.0
SQRT_FLOOR = 1e-12
HG_CHUNK = 32
N_EXPERTS = 32
TOP_K = 4
SWIGLU_LIMIT = 7.0
SWIGLU_ALPHA = 1.702

TM = 512
TSCAN = 256
TB = 512
RUN_BITS = 7
TAIL_BITS = 6
ZR = TOP_K * 512 + N_EXPERTS * 8
LANES = 128
VMEM_LIMIT = 56 * 1024 * 1024

NT_DIMS = (((1,), (1,)), ((), ()))


def _cparams(*sem):
    return pltpu.CompilerParams(dimension_semantics=sem, vmem_limit_bytes=VMEM_LIMIT)


def _rms(x):
    return x * lax.rsqrt(jnp.mean(x * x, axis=-1, keepdims=True) + EPS)


def _sigmoid(x):
    return 1.0 / (1.0 + jnp.exp(-x))


def _sigmoid_abs(x):
    return 0.5 * jnp.tanh(0.5 * x) + 0.5


def _split_bf16(x):
    hi = x.astype(BF16)
    return hi, (x - hi.astype(F32)).astype(BF16)


def _adaln_body(c_ref, w_ref, b_ref, o_ref):
    c = c_ref[...]
    cond = c * _sigmoid(c)
    o_ref[...] = jnp.dot(cond, w_ref[...], preferred_element_type=F32,
                         precision=lax.Precision.HIGHEST) + b_ref[...]


def adaln(cond8, ada_w, ada_b):
    depth, d, n = ada_w.shape
    tn = 1536
    return pl.pallas_call(
        _adaln_body,
        grid=(depth, n // tn),
        in_specs=[pl.BlockSpec((8, d), lambda l, j: (0, 0)),
                  pl.BlockSpec((None, d, tn), lambda l, j: (l, 0, j)),
                  pl.BlockSpec((None, 1, tn), lambda l, j: (l, 0, j))],
        out_specs=pl.BlockSpec((None, 8, tn), lambda l, j: (l, 0, j)),
        out_shape=jax.ShapeDtypeStruct((depth, 8, n), F32),
        compiler_params=_cparams("parallel", "parallel"),
        name="adaln",
    )(cond8, ada_w, ada_b.reshape(depth, 1, n))


def _tile_rows(refs, n_lat):
    if len(refs) == 1:
        return refs[0][...]
    return jnp.where(pl.program_id(0) < n_lat, refs[0][...], refs[1][...])


def _inproj_body(*refs, n_x, n_lat):
    (g_ref, sh_ref, sc_ref, w_ref, cs_ref, sa_ref, sb_ref, q_ref, kv_ref, lru_ref, hg_ref) = refs[n_x:]
    h = _rms(_tile_rows(refs[:n_x], n_lat)) * g_ref[...]
    h = h * (1.0 + sc_ref[...]) + sh_ref[...]
    p = jnp.dot(h.astype(BF16), w_ref[...], preferred_element_type=F32)
    cs, sa, sb = cs_ref[...], sa_ref[...], sb_ref[...]

    def rope(t):
        return t * cs + pltpu.roll(t, LANES - 16, 1) * sa + pltpu.roll(t, 16, 1) * sb

    for j in range(4):
        q_ref[:, j * LANES:(j + 1) * LANES] = (
            rope(p[:, j * LANES:(j + 1) * LANES]) * (HEAD_DIM ** -0.5)).astype(BF16)
    k = rope(p[:, 512:640])
    v = p[:, 640:768]
    lo = lax.broadcasted_iota(I32, k.shape, 1) < HEAD_DIM
    kr = pltpu.roll(k, HEAD_DIM, 1)
    vr = pltpu.roll(v, HEAD_DIM, 1)
    kv_ref[:, 0:128] = jnp.where(lo, k, kr).astype(BF16)
    kv_ref[:, 128:256] = jnp.where(lo, kr, k).astype(BF16)
    kv_ref[:, 256:384] = jnp.where(lo, v, vr).astype(BF16)
    kv_ref[:, 384:512] = jnp.where(lo, vr, v).astype(BF16)
    lru_ref[...] = p[:, 768:1280]
    hg_ref[...] = p[:, 1280:2560]


def _stream_specs(xs, n_lat, d):
    if len(xs) == 1:
        return [pl.BlockSpec((TM, d), lambda i: (i, 0))]
    return [pl.BlockSpec((TM, d), lambda i: (jnp.minimum(i, n_lat - 1), 0)),
            pl.BlockSpec((TM, d), lambda i: (0, 0))]


def inproj(xs, g, shift, scale, w_bf, rope_c, rope_a, rope_b, *, tiles_per_batch, n_seg):
    d = xs[0].shape[1]
    n_lat = tiles_per_batch * (n_seg - 1)
    n_tiles = n_lat + 1
    r = n_tiles * TM

    def seg(i):
        return (jnp.minimum(i // tiles_per_batch, n_seg - 1), 0, 0)

    def rope_idx(i):
        return (jnp.where(i < n_lat, i % tiles_per_batch, tiles_per_batch), 0)

    row = lambda w: pl.BlockSpec((TM, w), lambda i: (i, 0))
    return pl.pallas_call(
        functools.partial(_inproj_body, n_x=len(xs), n_lat=n_lat),
        grid=(n_tiles,),
        in_specs=_stream_specs(xs, n_lat, d) + [
                  pl.BlockSpec((1, d), lambda i: (0, 0)),
                  pl.BlockSpec((None, 1, d), seg),
                  pl.BlockSpec((None, 1, d), seg),
                  pl.BlockSpec(w_bf.shape, lambda i: (0, 0)),
                  pl.BlockSpec((TM, LANES), rope_idx),
                  pl.BlockSpec((TM, LANES), rope_idx),
                  pl.BlockSpec((TM, LANES), rope_idx)],
        out_specs=[row(512), row(512), row(512), row(1280)],
        out_shape=[jax.ShapeDtypeStruct((r, 512), BF16),
                   jax.ShapeDtypeStruct((r, 512), BF16),
                   jax.ShapeDtypeStruct((r, 512), F32),
                   jax.ShapeDtypeStruct((r, 1280), F32)],
        compiler_params=_cparams("parallel"),
        name="inproj",
    )(*xs, g, shift, scale, w_bf, rope_c, rope_a, rope_b)


def _softmax_pv(s_parts, v_parts, sink_col):
    def lane_blocks(a):
        return [a[:, j:j + LANES] for j in range(0, a.shape[1], LANES)]

    folded = None
    for s in s_parts:
        for blk in lane_blocks(s):
            folded = blk if folded is None else jnp.maximum(folded, blk)
    m = jnp.maximum(sink_col, jnp.max(folded, axis=-1, keepdims=True))
    psum = None
    o = None
    for s, v in zip(s_parts, v_parts):
        p = jnp.exp(s - m)
        for blk in lane_blocks(p):
            psum = blk if psum is None else psum + blk
        pv = jnp.dot(p.astype(BF16), v, preferred_element_type=F32)
        o = pv if o is None else o + pv
    l = jnp.exp(sink_col - m) + jnp.sum(psum, axis=-1, keepdims=True)
    return o / l


def _attn_body(sink_ref, q_ref, kvm_ref, kvp_ref, kvn_ref, kvc_ref, o_ref, *,
               tiles_per_batch, n_lat, ctx_len, n_batch):
    i = pl.program_id(0)
    blk = WINDOW
    nsub = TM // blk

    def stacked_q(q_sub, g):
        rows = q_sub.shape[0]
        lo = lax.broadcasted_iota(I32, (rows, LANES), 1) < HEAD_DIM
        parts = []
        for jq in (2 * g, 2 * g + 1):
            qg = q_sub[:, jq * LANES:(jq + 1) * LANES]
            parts.append(jnp.where(lo, qg, jnp.zeros_like(qg)))
            parts.append(jnp.where(lo, jnp.zeros_like(qg), qg))
        return jnp.concatenate(parts, axis=0)

    def sink_column(g, rows):
        blk_id = lax.broadcasted_iota(I32, (4 * rows, 1), 0) // rows
        col = jnp.full((4 * rows, 1), sink_ref[4 * g + 3], F32)
        for b in range(3):
            col = jnp.where(blk_id == b, sink_ref[4 * g + b], col)
        return col

    def write_heads(o, g, rows, row0):
        lo = lax.broadcasted_iota(I32, (rows, LANES), 1) < HEAD_DIM
        for t, jq in enumerate((2 * g, 2 * g + 1)):
            o_e0 = o[(2 * t) * rows:(2 * t + 1) * rows]
            o_e1 = o[(2 * t + 1) * rows:(2 * t + 2) * rows]
            o_ref[row0:row0 + rows, jq * LANES:(jq + 1) * LANES] = (
                jnp.where(lo, o_e0, o_e1).astype(BF16))

    @pl.when(i < n_lat)
    def _latent():
        j = i % tiles_per_batch
        r_io = lax.broadcasted_iota(I32, (4 * blk, blk), 0) % blk
        c_io = lax.broadcasted_iota(I32, (4 * blk, blk), 1)
        for sb in range(nsub):
            q_sub = q_ref[sb * blk:(sb + 1) * blk, :]
            has_prev = jnp.logical_or(j > 0, sb > 0)
            has_next = jnp.logical_or(j < tiles_per_batch - 1, sb < nsub - 1)
            m_prev = jnp.logical_and(c_io >= r_io, has_prev)
            m_next = jnp.logical_and(c_io <= r_io, has_next)
            for g in range(2):
                kc, vc = g * LANES, (2 + g) * LANES

                def kv_block(b, col):
                    if b < 0:
                        return kvp_ref[:, col:col + LANES]
                    if b >= nsub:
                        return kvn_ref[:, col:col + LANES]
                    return kvm_ref[b * blk:(b + 1) * blk, col:col + LANES]

                qs = stacked_q(q_sub, g)
                s_prev = lax.dot_general(qs, kv_block(sb - 1, kc), NT_DIMS, preferred_element_type=F32)
                s_own = lax.dot_general(qs, kv_block(sb, kc), NT_DIMS, preferred_element_type=F32)
                s_next = lax.dot_general(qs, kv_block(sb + 1, kc), NT_DIMS, preferred_element_type=F32)
                s_ctx = lax.dot_general(qs, kvc_ref[:, kc:kc + LANES], NT_DIMS, preferred_element_type=F32)
                s_prev = jnp.where(m_prev, s_prev, NEG_INF)
                s_next = jnp.where(m_next, s_next, NEG_INF)
                o = _softmax_pv(
                    [s_prev, s_own, s_next, s_ctx],
                    [kv_block(sb - 1, vc), kv_block(sb, vc), kv_block(sb + 1, vc),
                     kvc_ref[:, vc:vc + LANES]],
                    sink_column(g, blk))
                write_heads(o, g, blk, sb * blk)

    @pl.when(i >= n_lat)
    def _context():
        for bb in range(n_batch):
            r0 = bb * ctx_len
            q_sub = q_ref[r0:r0 + ctx_len, :]
            for g in range(2):
                kc, vc = g * LANES, (2 + g) * LANES
                qs = stacked_q(q_sub, g)
                s = lax.dot_general(qs, kvm_ref[r0:r0 + ctx_len, kc:kc + LANES], NT_DIMS,
                                    preferred_element_type=F32)
                o = _softmax_pv([s], [kvm_ref[r0:r0 + ctx_len, vc:vc + LANES]],
                                sink_column(g, ctx_len))
                write_heads(o, g, ctx_len, r0)


def attention(q, kv, sink, *, tiles_per_batch, n_batch, ctx_len, with_ctx):
    r = q.shape[0]
    n_lat = tiles_per_batch * n_batch
    assert n_batch * ctx_len == TM and r // TM == n_lat + 1
    n_tiles = n_lat + (1 if with_ctx else 0)
    sub = TM // WINDOW
    n_blk128 = r // WINDOW
    ctx_blk0 = (n_lat * TM) // ctx_len

    def prev_idx(i, s):
        return (jnp.maximum(i * sub - 1, 0), 0)

    def next_idx(i, s):
        return (jnp.minimum(i * sub + sub, n_blk128 - 1), 0)

    def ctx_idx(i, s):
        return (ctx_blk0 + jnp.minimum(i // tiles_per_batch, n_batch - 1), 0)

    body = functools.partial(_attn_body, tiles_per_batch=tiles_per_batch, n_lat=n_lat,
                             ctx_len=ctx_len, n_batch=n_batch)
    return pl.pallas_call(
        body,
        grid_spec=pltpu.PrefetchScalarGridSpec(
            num_scalar_prefetch=1,
            grid=(n_tiles,),
            in_specs=[pl.BlockSpec((TM, 512), lambda i, s: (i, 0)),
                      pl.BlockSpec((TM, 512), lambda i, s: (i, 0)),
                      pl.BlockSpec((WINDOW, 512), prev_idx),
                      pl.BlockSpec((WINDOW, 512), next_idx),
                      pl.BlockSpec((ctx_len, 512), ctx_idx)],
            out_specs=pl.BlockSpec((TM, 512), lambda i, s: (i, 0))),
        out_shape=jax.ShapeDtypeStruct((n_tiles * TM, 512), BF16),
        compiler_params=_cparams("parallel"),
        name="attention",
    )(sink, q, kv, kv, kv, kv)


def _scan_maps(n_batch, seq, ctx_len):
    assert ctx_len == TSCAN
    nl = seq // TSCAN
    ctx0 = (n_batch * seq) // TSCAN

    def fwd(b, c):
        return jnp.where(c == 0, ctx0 + b, b * nl + c - 1)

    def bwd(b, c):
        return jnp.where(c == 0, ctx0 + b, b * nl + nl - c)

    return nl, fwd, bwd


def _lru_dir(x_ref, xp_ref, xn_ref, has_prev, has_next, cw_ref, cb_ref, w_ref, br_ref, bi_ref,
             lam_ref, a_scr, b_scr, reverse):
    t = TSCAN
    xp = jnp.where(has_prev, xp_ref[...], 0.0)
    xn = jnp.where(has_next, xn_ref[...], 0.0)
    xpad = jnp.concatenate([xp, x_ref[...], xn], axis=0)
    n = t + 16
    cw = cw_ref[...]
    u = cb_ref[...] + jnp.zeros((t, xpad.shape[1]), F32)
    for j in range(4):
        sh = (2 - j) % n
        rolled = xpad if sh == 0 else pltpu.roll(xpad, sh, 0)
        u = u + rolled[8:8 + t] * cw[j:j + 1, :]
    gates = jnp.dot(u.astype(BF16), w_ref[...], preferred_element_type=F32)
    c = u.shape[1]
    r = _sigmoid_abs(gates[:, :c] + br_ref[...])
    ig = _sigmoid_abs(gates[:, c:] + bi_ref[...])
    nl = -lam_ref[...]
    softplus = jnp.maximum(nl, 0.0) + jnp.log1p(jnp.exp(-jnp.abs(nl)))
    log_a = (-LRU_C) * r * softplus
    a = jnp.exp(log_a)
    mult = jnp.sqrt(jnp.maximum(1.0 - a * a, SQRT_FLOOR))
    bv = mult * (ig * u)

    g8 = t // 8
    a3 = a.reshape(g8, 8, c)
    b3 = bv.reshape(g8, 8, c)
    r8 = lax.broadcasted_iota(I32, (g8, 8, c), 1)
    for s in (1, 2, 4):
        if reverse:
            a_sh, b_sh, ok = pltpu.roll(a3, 8 - s, 1), pltpu.roll(b3, 8 - s, 1), r8 < 8 - s
        else:
            a_sh, b_sh, ok = pltpu.roll(a3, s, 1), pltpu.roll(b3, s, 1), r8 >= s
        b3 = jnp.where(ok, a3 * b_sh + b3, b3)
        a3 = jnp.where(ok, a3 * a_sh, a3)
    a_scr[...] = a3.reshape(t, c)
    b_scr[...] = b3.reshape(t, c)


def _lru_body(xf_ref, xfp_ref, xfn_ref, xb_ref, xbp_ref, xbn_ref, cw_ref, cb_ref, w_ref, br_ref,
              bi_ref, lam_ref, hf_ref, hb_ref, st_ref, a_scr, b_scr, *, nl):
    c = pl.program_id(1)

    @pl.when(c == 0)
    def _init():
        st_ref[...] = jnp.zeros_like(st_ref)

    lat = c > 0
    _lru_dir(xf_ref, xfp_ref, xfn_ref, jnp.logical_and(lat, c > 1), jnp.logical_and(lat, c < nl),
             cw_ref, cb_ref, w_ref.at[0], br_ref.at[0], bi_ref.at[0], lam_ref.at[0],
             a_scr.at[0], b_scr.at[0], False)
    _lru_dir(xb_ref, xbp_ref, xbn_ref, jnp.logical_and(lat, c < nl), jnp.logical_and(lat, c > 1),
             cw_ref, cb_ref, w_ref.at[1], br_ref.at[1], bi_ref.at[1], lam_ref.at[1],
             a_scr.at[1], b_scr.at[1], True)

    g8 = TSCAN // 8

    def group(gi, carry):
        cf, cb = carry
        rf = pl.ds(pl.multiple_of(gi * 8, 8), 8)
        rb = pl.ds(pl.multiple_of((g8 - 1 - gi) * 8, 8), 8)
        hf = b_scr[0, rf, :] + a_scr[0, rf, :] * cf
        hb = b_scr[1, rb, :] + a_scr[1, rb, :] * cb
        hf_ref[rf, :] = hf
        hb_ref[rb, :] = hb
        return (jnp.broadcast_to(hf[7:8, :], hf.shape), jnp.broadcast_to(hb[0:1, :], hb.shape))

    cf, cb = lax.fori_loop(0, g8, group, (st_ref[0], st_ref[1]), unroll=4)
    st_ref[0] = cf
    st_ref[1] = cb


def lru_scan(lru, conv_w, conv_b, w_gates, br, bi, lam, *, n_batch, seq, ctx_len):
    r = lru.shape[0]
    c = lru.shape[1] // 2
    nl, fwd, bwd = _scan_maps(n_batch, seq, ctx_len)
    per = TSCAN // 8
    n8 = r // 8

    def main(m):
        return pl.BlockSpec((TSCAN, c), lambda b, s: (m(b, s), 0))

    def prev(m):
        return pl.BlockSpec((8, c), lambda b, s: (jnp.maximum(m(b, s) * per - 1, 0), 0))

    def nxt(m):
        return pl.BlockSpec((8, c), lambda b, s: (jnp.minimum(m(b, s) * per + per, n8 - 1), 0))

    full = lambda a: pl.BlockSpec(a.shape, lambda b, s: (0,) * a.ndim)
    return pl.pallas_call(
        functools.partial(_lru_body, nl=nl),
        grid=(n_batch, nl + 1),
        in_specs=[main(fwd), prev(fwd), nxt(fwd), main(bwd), prev(bwd), nxt(bwd),
                  full(conv_w), full(conv_b), full(w_gates), full(br), full(bi), full(lam)],
        out_specs=[main(fwd), main(bwd)],
        out_shape=[jax.ShapeDtypeStruct((r, c), F32)] * 2,
        scratch_shapes=[pltpu.VMEM((2, 8, c), F32), pltpu.VMEM((2, TSCAN, c), F32),
                        pltpu.VMEM((2, TSCAN, c), F32)],
        compiler_params=_cparams("parallel", "arbitrary"),
        name="lru_scan",
    )(lru, lru, lru, lru, lru, lru, conv_w, conv_b, w_gates, br, bi, lam)


def _hgrn_dir(q_ref, z_ref, v_ref, lb_ref, st_ref, out_ref, reverse):
    t = TSCAN
    nch = t // HG_CHUNK
    q = q_ref[...]
    z = z_ref[...]
    v = v_ref[...]
    lb = lb_ref[...]
    w = q.shape[1]
    log_f = jnp.log(lb + (1.0 - lb) * _sigmoid(z))
    k = (1.0 - lb) * _sigmoid(-z)

    rc = lax.broadcasted_iota(I32, (t, w), 0) % HG_CHUNK
    b = log_f
    s = 1
    while s < HG_CHUNK:
        if reverse:
            b = b + jnp.where(rc < HG_CHUNK - s, pltpu.roll(b, t - s, 0), 0.0)
        else:
            b = b + jnp.where(rc >= s, pltpu.roll(b, s, 0), 0.0)
        s *= 2
    tot = jnp.sum(log_f.reshape(nch, HG_CHUNK, w), axis=1, keepdims=True)
    b_last = jnp.broadcast_to(tot, (nch, HG_CHUNK, w)).reshape(t, w)
    b_half = 0.5 * b_last
    q_i = (q * jnp.exp(b - b_half)).astype(BF16)
    k_i = (k * jnp.exp(b_half - b)).astype(BF16)
    q_s = (q * jnp.exp(b)).astype(BF16)
    k_e = (k * jnp.exp(b_last - b)).astype(BF16)
    decay = jnp.exp(tot.reshape(nch, w))
    v_bf = v.astype(BF16)

    ri = lax.broadcasted_iota(I32, (t, t), 0)
    ci = lax.broadcasted_iota(I32, (t, t), 1)
    same = (ri // HG_CHUNK) == (ci // HG_CHUNK)
    causal = jnp.logical_and(same, (ci >= ri) if reverse else (ci <= ri))
    lo = lax.broadcasted_iota(I32, (t, LANES), 1) < HEAD_DIM
    chunk_of_row = lax.broadcasted_iota(I32, (t, LANES), 0) // HG_CHUNK
    bd_r = lax.broadcasted_iota(I32, (LANES, LANES), 0) // HEAD_DIM
    bd_c = lax.broadcasted_iota(I32, (LANES, LANES), 1) // HEAD_DIM
    block_diag = bd_r == bd_c

    for p in range(w // LANES):
        cols = slice(p * LANES, (p + 1) * LANES)
        qi_p, ki_p, qs_p, ke_p, v_p = q_i[:, cols], k_i[:, cols], q_s[:, cols], k_e[:, cols], v_bf[:, cols]
        o_half = []
        for e in range(2):
            qm = jnp.where(lo if e == 0 else jnp.logical_not(lo), qi_p, jnp.zeros_like(qi_p))
            att = lax.dot_general(qm, ki_p, NT_DIMS, preferred_element_type=F32)
            att = jnp.where(causal, att, 0.0).astype(BF16)
            o_half.append(jnp.dot(att, v_p, preferred_element_type=F32))
        out_ref[:, cols] = jnp.where(lo, o_half[0], o_half[1])

        v_t = v[:, cols].T.astype(BF16)
        st = st_ref[p]
        order = range(nch - 1, -1, -1) if reverse else range(nch)
        for n in order:
            rows = slice(n * HG_CHUNK, (n + 1) * HG_CHUNK)
            o_inter = lax.dot_general(qs_p[rows], st.astype(BF16), NT_DIMS, preferred_element_type=F32)
            out_ref[rows, cols] = out_ref[rows, cols] + o_inter
            ke_n = jnp.where(chunk_of_row == n, ke_p, jnp.zeros_like(ke_p))
            kv_t = jnp.dot(v_t, ke_n, preferred_element_type=F32)
            st = decay[n:n + 1, cols] * st + jnp.where(block_diag, kv_t, 0.0)
        st_ref[p] = st


def _hgrn_body(qf_ref, zf_ref, vf_ref, qb_ref, zb_ref, vb_ref, lb_ref, of_ref, ob_ref, st_ref):
    c = pl.program_id(1)

    @pl.when(c == 0)
    def _init():
        st_ref[...] = jnp.zeros_like(st_ref)

    _hgrn_dir(qf_ref, zf_ref, vf_ref, lb_ref.at[0], st_ref.at[0], of_ref, False)
    _hgrn_dir(qb_ref, zb_ref, vb_ref, lb_ref.at[1], st_ref.at[1], ob_ref, True)


def hgrn_scan(hg, lower_bounds, *, n_batch, seq, ctx_len):
    r = hg.shape[0]
    w = hg.shape[1] // 5
    nl, fwd, bwd = _scan_maps(n_batch, seq, ctx_len)

    def col(m, j):
        return pl.BlockSpec((TSCAN, w), lambda b, s: (m(b, s), j))

    lb3 = lower_bounds.reshape(2, 1, w)
    return pl.pallas_call(
        _hgrn_body,
        grid=(n_batch, nl + 1),
        in_specs=[col(fwd, 0), col(fwd, 1), col(fwd, 3), col(bwd, 0), col(bwd, 2), col(bwd, 3),
                  pl.BlockSpec(lb3.shape, lambda b, s: (0, 0, 0))],
        out_specs=[col(fwd, 0), col(bwd, 0)],
        out_shape=[jax.ShapeDtypeStruct((r, w), F32)] * 2,
        scratch_shapes=[pltpu.VMEM((2, w // LANES, LANES, LANES), F32)],
        compiler_params=_cparams("parallel", "arbitrary"),
        name="hgrn_scan",
    )(hg, hg, hg, hg, hg, hg, lb3)


def _outproj_router_body(*refs, n_x, n_lat):
    (att_ref, hf_ref, hb_ref, lg_ref, of_ref, ob_ref, gg_ref, hn_ref,
     wo_ref, gm_ref, sh_ref, sc_ref, n2_ref, rw_ref, rb_ref,
     xo_ref, h2_ref, idx_ref, gate_ref, rank_ref, cnt_ref) = refs[n_x:]
    lru_y = (hf_ref[...] + hb_ref[...]) * jax.nn.gelu(lg_ref[...], approximate=True)
    o = of_ref[...] + ob_ref[...]
    w = o.shape[1]
    gr = lax.broadcasted_iota(I32, (w, w), 0) // HEAD_DIM
    gc = lax.broadcasted_iota(I32, (w, w), 1) // HEAD_DIM
    head_mean = jnp.where(gr == gc, 1.0 / HEAD_DIM, 0.0).astype(BF16)
    sq_hi, sq_lo = _split_bf16(o * o)
    ms = (jnp.dot(sq_hi, head_mean, preferred_element_type=F32)
          + jnp.dot(sq_lo, head_mean, preferred_element_type=F32))
    gg = gg_ref[...]
    hg_y = o * lax.rsqrt(ms + EPS) * hn_ref[...] * (gg * _sigmoid_abs(gg))
    y = jnp.dot(att_ref[...], wo_ref[0:512, :], preferred_element_type=F32)
    y = y + jnp.dot(lru_y.astype(BF16), wo_ref[512:768, :], preferred_element_type=F32)
    y = y + jnp.dot(hg_y.astype(BF16), wo_ref[768:1024, :], preferred_element_type=F32)
    x = _tile_rows(refs[:n_x], n_lat) + gm_ref[...] * y
    xo_ref[...] = x
    h2 = _rms(x) * n2_ref[...]
    h2 = h2 * (1.0 + sc_ref[...]) + sh_ref[...]
    h2_ref[...] = h2

    logits = lax.dot_general(rw_ref[...], h2, NT_DIMS, preferred_element_type=F32,
                             precision=lax.Precision.HIGHEST) + rb_ref[...]
    ne, tm = logits.shape
    e_io = lax.broadcasted_iota(I32, (ne, tm), 0).astype(F32)
    work = logits
    vals, hots = [], []
    for k in range(TOP_K):
        m = jnp.max(work, axis=0, keepdims=True)
        idx = jnp.min(jnp.where(work == m, e_io, float(ne)), axis=0, keepdims=True)
        hot = e_io == idx
        vals.append(m)
        hots.append(hot)
        idx_ref[k:k + 1, :] = idx.astype(I32)
        work = jnp.where(hot, -jnp.inf, work)
    exps = [jnp.exp(v - vals[0]) for v in vals]
    denom = exps[0] + exps[1] + exps[2] + exps[3]
    for k in range(TOP_K):
        gate_ref[k:k + 1, :] = exps[k] / denom
    z4 = jnp.zeros((8 - TOP_K, tm), F32)
    gate_ref[TOP_K:8, :] = z4
    idx_ref[TOP_K:8, :] = z4.astype(I32)
    rank_ref[TOP_K:8, :] = z4.astype(I32)

    chosen = jnp.logical_or(jnp.logical_or(hots[0], hots[1]), jnp.logical_or(hots[2], hots[3]))
    sr = lax.broadcasted_iota(I32, (tm, tm), 0)
    sc = lax.broadcasted_iota(I32, (tm, tm), 1)
    before = jnp.where(sr < sc, 1.0, 0.0).astype(BF16)
    chosen_f = jnp.where(chosen, 1.0, 0.0)
    prefix = jnp.dot(chosen_f.astype(BF16), before, preferred_element_type=F32)
    for k in range(TOP_K):
        rk = jnp.sum(jnp.where(hots[k], prefix, 0.0), axis=0, keepdims=True)
        rank_ref[k:k + 1, :] = rk.astype(I32)
    cnt_ref[...] = jnp.broadcast_to(jnp.sum(chosen_f, axis=1, keepdims=True), cnt_ref.shape)


def outproj_router(xs, att, hf, hb, lru, of, ob, hg, hn_g, wo_bf, gate_msa, shift_mlp, scale_mlp,
                   n2_g, rw_t, rb, *, tiles_per_batch, n_seg, with_ctx):
    d = xs[0].shape[1]
    n_lat = tiles_per_batch * (n_seg - 1)
    n_tiles = n_lat + (1 if with_ctx else 0)
    r = n_tiles * TM

    def seg(i):
        return (jnp.minimum(i // tiles_per_batch, n_seg - 1), 0, 0)

    row = lambda w, j=0: pl.BlockSpec((TM, w), lambda i: (i, j))
    full = lambda a: pl.BlockSpec(a.shape, lambda i: (0,) * a.ndim)
    modspec = pl.BlockSpec((None, 1, d), seg)
    lane_out = pl.BlockSpec((8, TM), lambda i: (0, i))
    return pl.pallas_call(
        functools.partial(_outproj_router_body, n_x=len(xs), n_lat=n_lat),
        grid=(n_tiles,),
        in_specs=_stream_specs(xs, n_lat, d) + [
                  row(512), row(256), row(256), row(256, 1), row(256), row(256),
                  row(256, 4), full(hn_g), full(wo_bf), modspec, modspec, modspec, full(n2_g),
                  full(rw_t), full(rb)],
        out_specs=[row(d), row(d), lane_out, lane_out, lane_out,
                   pl.BlockSpec((None, N_EXPERTS, LANES), lambda i: (i, 0, 0))],
        out_shape=[jax.ShapeDtypeStruct((r, d), F32), jax.ShapeDtypeStruct((r, d), F32),
                   jax.ShapeDtypeStruct((8, r), I32), jax.ShapeDtypeStruct((8, r), F32),
                   jax.ShapeDtypeStruct((8, r), I32),
                   jax.ShapeDtypeStruct((n_tiles, N_EXPERTS, LANES), F32)],
        compiler_params=_cparams("parallel"),
        name="outproj_router",
    )(*xs, att, hf, hb, lru, of, ob, hg, hn_g, wo_bf, gate_msa, shift_mlp, scale_mlp, n2_g, rw_t, rb)


def _local_pos(idx_ref, rank_ref, toff_ref, k):
    ne = toff_ref.shape[0]
    tm = idx_ref.shape[1]
    e_io = lax.broadcasted_iota(I32, (ne, tm), 0)
    off = jnp.sum(jnp.where(e_io == idx_ref[k:k + 1, :], toff_ref[...], 0.0), axis=0, keepdims=True)
    return off + rank_ref[k:k + 1, :].astype(F32)


def _run_copies(tab_ref, tile, make_copy, sems, slot, start):
    base = tile * LANES

    def per_expert(e, carry):
        src8 = tab_ref[base + e]
        dst8 = tab_ref[base + N_EXPERTS + e]
        n8 = tab_ref[base + 2 * N_EXPERTS + e]

        for b in range(RUN_BITS):
            size = 8 << b
            off8 = (n8 >> (b + 1)) << (b + 1)

            @pl.when(((n8 >> b) & 1) == 1)
            def _():
                cp = make_copy(pl.multiple_of((src8 + off8) * 8, 8),
                               pl.multiple_of((dst8 + off8) * 8, 8), size, sems.at[slot, b])
                if start:
                    cp.start()
                else:
                    cp.wait()
        return carry

    lax.fori_loop(0, N_EXPERTS, per_expert, 0)


def _fill_copies(fill_ref, zero_scr, xs_ref, sems, start):
    def go(cp):
        if start:
            cp.start()
        else:
            cp.wait()

    def per_expert(e, carry):
        t8 = fill_ref[e]
        n8 = fill_ref[N_EXPERTS + e]
        for b in range(TAIL_BITS):
            size = 8 << b
            off8 = (n8 >> (b + 1)) << (b + 1)

            @pl.when(((n8 >> b) & 1) == 1)
            def _():
                go(pltpu.make_async_copy(
                    zero_scr.at[pl.ds(0, size), :],
                    xs_ref.at[pl.ds(pl.multiple_of((t8 + off8) * 8, 8), size), :], sems.at[0, b]))
        return carry

    lax.fori_loop(0, N_EXPERTS, per_expert, 0)

    def per_block(j, carry):
        go(pltpu.make_async_copy(zero_scr, xs_ref.at[pl.ds(pl.multiple_of(j * TB, TB), TB), :],
                                 sems.at[0, TAIL_BITS]))
        return carry

    lax.fori_loop(fill_ref[2 * N_EXPERTS], xs_ref.shape[0] // TB, per_block, 0)


def _dispatch_body(fill_ref, tab_ref, h_ref, idx_ref, rank_ref, toff_ref, xs_ref, z_scr, zero_scr, sems):
    tm = h_ref.shape[0]
    i = pl.program_id(0)
    slot = i % 2

    @pl.when(i == 0)
    def _fill():
        zero_scr[...] = jnp.zeros_like(zero_scr)
        _fill_copies(fill_ref, zero_scr, xs_ref, sems, True)
        _fill_copies(fill_ref, zero_scr, xs_ref, sems, False)

    r_io = lax.broadcasted_iota(I32, (ZR, tm), 0).astype(F32)
    hit = r_io == _local_pos(idx_ref, rank_ref, toff_ref, 0)
    for k in range(1, TOP_K):
        hit = jnp.logical_or(hit, r_io == _local_pos(idx_ref, rank_ref, toff_ref, k))
    perm = jnp.where(hit, 1.0, 0.0).astype(BF16)
    z_scr[slot] = jnp.dot(perm, h_ref[...].astype(BF16), preferred_element_type=F32)

    def copy_from(buf):
        def make_copy(s, d, size, sem):
            return pltpu.make_async_copy(z_scr.at[buf, pl.ds(s, size), :],
                                         xs_ref.at[pl.ds(d, size), :], sem)
        return make_copy

    _run_copies(tab_ref, i, copy_from(slot), sems, slot, True)

    @pl.when(i > 0)
    def _drain_previous():
        _run_copies(tab_ref, i - 1, copy_from(1 - slot), sems, 1 - slot, False)

    @pl.when(i == pl.num_programs(0) - 1)
    def _drain_last():
        _run_copies(tab_ref, i, copy_from(slot), sems, slot, False)


def moe_dispatch(h2, idx, rank, toff_col, tab, fill, p_rows):
    r, d = h2.shape
    n_tiles = r // TM
    lane_in = pl.BlockSpec((8, TM), lambda i, f, t: (0, i))
    return pl.pallas_call(
        _dispatch_body,
        grid_spec=pltpu.PrefetchScalarGridSpec(
            num_scalar_prefetch=2,
            grid=(n_tiles,),
            in_specs=[pl.BlockSpec((TM, d), lambda i, f, t: (i, 0)),
                      lane_in, lane_in,
                      pl.BlockSpec((None, N_EXPERTS, 1), lambda i, f, t: (i, 0, 0))],
            out_specs=pl.BlockSpec(memory_space=pl.ANY),
            scratch_shapes=[pltpu.VMEM((2, ZR, d), F32), pltpu.VMEM((TB, d), F32),
                            pltpu.SemaphoreType.DMA((2, RUN_BITS))]),
        out_shape=jax.ShapeDtypeStruct((p_rows, d), F32),
        compiler_params=_cparams("arbitrary"),
        name="moe_dispatch",
    )(fill, tab, h2, idx, rank, toff_col)


def _expert_body(be_ref, nu_ref, x_ref, wgu_ref, bgu_ref, wd_ref, bd_ref, y_ref, wgu_bf, wd_bf):
    i = pl.program_id(0)

    @pl.when(i < nu_ref[0])
    def _run():
        new_expert = jnp.logical_or(i == 0, be_ref[i] != be_ref[jnp.maximum(i - 1, 0)])

        @pl.when(new_expert)
        def _cast():
            rows = 128

            def chunk(j, carry):
                sl = pl.ds(pl.multiple_of(j * rows, rows), rows)
                wgu_bf[sl, :] = wgu_ref[sl, :].astype(BF16)
                wd_bf[sl, :] = wd_ref[sl, :].astype(BF16)
                return carry

            lax.fori_loop(0, wgu_ref.shape[0] // rows, chunk, 0)

        dff = wd_ref.shape[0]
        gu = jnp.dot(x_ref[...].astype(BF16), wgu_bf[...], preferred_element_type=F32) + bgu_ref[...]
        gate = jnp.minimum(gu[:, :dff], SWIGLU_LIMIT)
        up = jnp.clip(gu[:, dff:], -SWIGLU_LIMIT, SWIGLU_LIMIT)
        glu = gate * _sigmoid_abs(SWIGLU_ALPHA * gate)
        act = ((up + 1.0) * glu).astype(BF16)
        y_ref[...] = jnp.dot(act, wd_bf[...], preferred_element_type=F32) + bd_ref[...]

    @pl.when(i >= nu_ref[0])
    def _unused():
        y_ref[...] = jnp.zeros_like(y_ref)


def moe_experts(xs_sorted, blk_e, n_used, wgu, bgu, wd, bd, layer):
    p_rows, d = xs_sorted.shape
    depth, ne, _, dgu = wgu.shape
    dff = wd.shape[2]
    assert dff == d

    def expert(i, be, nu):
        return (layer * ne + be[i], 0, 0)

    return pl.pallas_call(
        _expert_body,
        grid_spec=pltpu.PrefetchScalarGridSpec(
            num_scalar_prefetch=2,
            grid=(p_rows // TB,),
            in_specs=[pl.BlockSpec((TB, d), lambda i, be, nu: (i, 0)),
                      pl.BlockSpec((None, d, dgu), expert),
                      pl.BlockSpec((None, 1, dgu), expert),
                      pl.BlockSpec((None, dff, d), expert),
                      pl.BlockSpec((None, 1, d), expert)],
            out_specs=pl.BlockSpec((TB, d), lambda i, be, nu: (i, 0)),
            scratch_shapes=[pltpu.VMEM((d, dgu), BF16), pltpu.VMEM((dff, d), BF16)]),
        out_shape=jax.ShapeDtypeStruct((p_rows, d), F32),
        compiler_params=_cparams("arbitrary"),
        name="moe_experts",
    )(blk_e, n_used, xs_sorted, wgu.reshape(depth * ne, d, dgu), bgu.reshape(depth * ne, 1, dgu),
      wd.reshape(depth * ne, dff, d), bd.reshape(depth * ne, 1, d))


def _combine_body(tab_ref, x_ref, idx_ref, rank_ref, gate_ref, toff_ref, gm_ref, fg_ref, ys_ref,
                  o_ref, zy_scr, sems, *, final):
    tm = x_ref.shape[0]
    i = pl.program_id(0)
    slot = i % 2

    def copy_into(buf):
        def make_copy(s, d, size, sem):
            return pltpu.make_async_copy(ys_ref.at[pl.ds(d, size), :],
                                         zy_scr.at[buf, pl.ds(s, size), :], sem)
        return make_copy

    def fetch(tile, buf):
        zy_scr[buf, TOP_K * tm:ZR, :] = jnp.zeros((ZR - TOP_K * tm, zy_scr.shape[2]), F32)
        _run_copies(tab_ref, tile, copy_into(buf), sems, buf, True)

    @pl.when(i == 0)
    def _first():
        fetch(0, 0)

    @pl.when(i + 1 < pl.num_programs(0))
    def _prefetch_next():
        fetch(i + 1, 1 - slot)

    pos = [_local_pos(idx_ref, rank_ref, toff_ref, k) for k in range(TOP_K)]
    packed = jnp.concatenate(pos + [gate_ref[0:TOP_K, :], jnp.zeros((LANES - 2 * TOP_K, tm), F32)],
                             axis=0)
    cols = packed.T
    c_io = lax.broadcasted_iota(I32, (tm, ZR), 1).astype(F32)
    weights = jnp.zeros((tm, ZR), F32)
    for k in range(TOP_K):
        weights = jnp.where(c_io == cols[:, k:k + 1], cols[:, TOP_K + k:TOP_K + k + 1], weights)
    _run_copies(tab_ref, i, copy_into(slot), sems, slot, False)
    acc = jnp.dot(weights.astype(BF16), zy_scr[slot].astype(BF16), preferred_element_type=F32)
    x = x_ref[...] + gm_ref[...] * acc
    if final:
        x = _rms(x) * fg_ref[...]
    o_ref[...] = x


def moe_combine(xs, idx, rank, gates, toff_col, tab, ys, gate_mlp, final_g, *, tiles_per_batch,
                n_seg, n_tiles, final):
    d = xs.shape[1]

    def seg(i):
        return (jnp.minimum(i // tiles_per_batch, n_seg - 1), 0, 0)

    lane_in = pl.BlockSpec((8, TM), lambda i, t: (0, i))
    return pl.pallas_call(
        functools.partial(_combine_body, final=final),
        grid_spec=pltpu.PrefetchScalarGridSpec(
            num_scalar_prefetch=1,
            grid=(n_tiles,),
            in_specs=[pl.BlockSpec((TM, d), lambda i, t: (i, 0)),
                      lane_in, lane_in, lane_in,
                      pl.BlockSpec((None, N_EXPERTS, 1), lambda i, t: (i, 0, 0)),
                      pl.BlockSpec((None, 1, d), lambda i, t: seg(i)),
                      pl.BlockSpec((1, d), lambda i, t: (0, 0)),
                      pl.BlockSpec(memory_space=pl.ANY)],
            out_specs=pl.BlockSpec((TM, d), lambda i, t: (i, 0)),
            scratch_shapes=[pltpu.VMEM((2, ZR, d), F32), pltpu.SemaphoreType.DMA((2, RUN_BITS))]),
        out_shape=jax.ShapeDtypeStruct((n_tiles * TM, d), F32),
        compiler_params=_cparams("arbitrary"),
        name="moe_combine",
    )(tab, xs, idx, rank, gates, toff_col, gate_mlp, final_g, ys)


def _rope_tables(seq):
    nf = HEAD_DIM // 4
    pos = jnp.arange(seq)
    rows = (pos // GRID_W).astype(F32)
    cols = (pos % GRID_W).astype(F32)
    inv_freq = ROPE_BASE ** (-jnp.arange(nf, dtype=F32) / nf)
    d = jnp.arange(LANES) % HEAD_DIM
    axis = d // (2 * nf)
    half = (d // nf) % 2
    f = d % nf
    ang = jnp.where(axis[None, :] == 0, rows[:, None], cols[:, None]) * inv_freq[f][None, :]
    cs, sn = jnp.cos(ang), jnp.sin(ang)
    ca = jnp.where(half[None, :] == 0, -sn, 0.0)
    cb = jnp.where(half[None, :] == 1, sn, 0.0)
    pad1 = jnp.ones((TM, LANES), F32)
    pad0 = jnp.zeros((TM, LANES), F32)
    return (jnp.concatenate([cs, pad1]), jnp.concatenate([ca, pad0]), jnp.concatenate([cb, pad0]))


def _block_diag(w):
    n, c, _ = w.shape
    eye = jnp.eye(n, dtype=w.dtype)
    return (eye[:, None, :, None] * w[:, :, None, :]).reshape(n * c, n * c)


def _moe_layout(tile_counts, n_blocks):
    cnt8 = (tile_counts.astype(I32) + 7) // 8 * 8
    toff = jnp.cumsum(cnt8, axis=1) - cnt8
    goff = jnp.cumsum(cnt8, axis=0) - cnt8
    padded = (jnp.sum(cnt8, axis=0) + TB - 1) // TB * TB
    pends = jnp.cumsum(padded)
    dst = (pends - padded)[None, :] + goff
    n_tiles = cnt8.shape[0]
    tab = jnp.concatenate([toff, dst, cnt8, jnp.zeros((n_tiles, LANES - 3 * N_EXPERTS), I32)], axis=1) // 8
    blk_start = jnp.arange(n_blocks, dtype=I32) * TB
    blk_e = jnp.minimum(jnp.sum((pends[None, :] <= blk_start[:, None]).astype(I32), axis=1), N_EXPERTS - 1)
    n_used = (pends[-1:] // TB).astype(I32)
    total = jnp.sum(cnt8, axis=0)
    fill = jnp.concatenate([(pends - padded + total) // 8, (padded - total) // 8, n_used,
                            jnp.zeros((LANES - 2 * N_EXPERTS - 1,), I32)]).astype(I32)
    return (tab.reshape(n_tiles * LANES), toff.astype(F32).reshape(n_tiles, N_EXPERTS, 1),
            fill, blk_e.astype(I32), n_used)


def kernel(x, c, ctx, c_ctx, ada_w, ada_b, norm1_g, w_in, attn_sink, conv_w, conv_b, lru_wr, lru_br,
           lru_wi, lru_bi, lru_lambda, hgrn_lb_logits, hgrn_norm_g, w_out, norm2_g, router_w,
           router_b, moe_w_gu, moe_b_gu, moe_w_down, moe_b_down, final_g):
    n_batch, seq, d = x.shape
    ctx_len = ctx.shape[1]
    depth = ada_w.shape[0]
    assert n_batch * ctx_len == TM and seq % TM == 0 and ctx_len == TSCAN
    tiles_per_batch = seq // TM
    n_seg = n_batch + 1
    n_lat = n_batch * tiles_per_batch
    r = n_batch * seq + n_batch * ctx_len
    n_blocks = -(-(r * TOP_K + (r // TM) * N_EXPERTS * 7) // TB) + N_EXPERTS
    p_rows = n_blocks * TB

    xs = (x.reshape(n_batch * seq, d), ctx.reshape(n_batch * ctx_len, d))
    cond8 = jnp.zeros((8, d), F32).at[:n_batch].set(c).at[n_batch].set(c_ctx)
    mods = adaln(cond8, ada_w, ada_b)
    rope_c, rope_a, rope_b = _rope_tables(seq)
    lb_p = jax.nn.softmax(hgrn_lb_logits.astype(F32), axis=0)
    lower_bounds = jnp.cumsum(lb_p, axis=0) - lb_p[0]

    out = None
    for layer in range(depth):
        mod = [mods[layer, :, j * d:(j + 1) * d].reshape(8, 1, d) for j in range(6)]
        q, kv, lru, hg = inproj(xs, norm1_g[layer].reshape(1, d), mod[0], mod[1],
                                w_in[layer].astype(BF16), rope_c, rope_a, rope_b,
                                tiles_per_batch=tiles_per_batch, n_seg=n_seg)
        final = layer == depth - 1
        att = attention(q, kv, attn_sink[layer], tiles_per_batch=tiles_per_batch,
                        n_batch=n_batch, ctx_len=ctx_len, with_ctx=not final)
        w_gates = jnp.stack([jnp.concatenate([_block_diag(lru_wr[layer, dd]),
                                              _block_diag(lru_wi[layer, dd])], axis=1)
                             for dd in range(2)]).astype(BF16)
        cw = lru_br.shape[-1]
        hf, hb = lru_scan(lru, conv_w[layer], conv_b[layer].reshape(1, cw), w_gates,
                          lru_br[layer].reshape(2, 1, cw), lru_bi[layer].reshape(2, 1, cw),
                          lru_lambda[layer].reshape(2, 1, cw),
                          n_batch=n_batch, seq=seq, ctx_len=ctx_len)
        of, ob = hgrn_scan(hg, lower_bounds[layer], n_batch=n_batch, seq=seq, ctx_len=ctx_len)
        x_mid, h2, idx, gates, rank, counts = outproj_router(
            xs, att, hf, hb, lru, of, ob, hg, hgrn_norm_g[layer].reshape(1, -1),
            w_out[layer].astype(BF16), mod[2], mod[3], mod[4], norm2_g[layer].reshape(1, d),
            router_w[layer].T, router_b[layer].reshape(N_EXPERTS, 1),
            tiles_per_batch=tiles_per_batch, n_seg=n_seg, with_ctx=not final)
        tab, toff_col, fill, blk_e, n_used = _moe_layout(counts[:, :, 0], n_blocks)
        xs_sorted = moe_dispatch(h2, idx, rank, toff_col, tab, fill, p_rows)
        ys = moe_experts(xs_sorted, blk_e, n_used, moe_w_gu, moe_b_gu, moe_w_down, moe_b_down, layer)
        res = moe_combine(x_mid, idx, rank, gates, toff_col, tab, ys, mod[5], final_g.reshape(1, d),
                          tiles_per_batch=tiles_per_batch, n_seg=n_seg,
                          n_tiles=n_lat if final else n_lat + 1, final=final)
        if final:
            out = res.reshape(n_batch, seq, d)
        else:
            xs = (res,)
    return out
```

```python
import functools

import jax
import jax.numpy as jnp
from jax import lax
from jax.experimental import pallas as pl
from jax.experimental.pallas import tpu as pltpu

F32 = jnp.float32
BF16 = jnp.bfloat16
I32 = jnp.int32

EPS = 1e-6
NEG_INF = -1e30
GRID_W = 64
ROPE_BASE = 10000.0
HEAD_DIM = 64
N_HEADS = 8
WINDOW = 128
LRU_C = 8.0
SQRT_FLOOR = 1e-12
HG_CHUNK = 32
N_EXPERTS = 32
TOP_K = 4
SWIGLU_LIMIT = 7.0
SWIGLU_ALPHA = 1.702

TM = 512
TSCAN = 256
TB = 1024
TB_STEP = 256
RUN_BITS = 7
TAIL_BITS = 7
N_COPY_SEMS = max(RUN_BITS, TAIL_BITS + 1)
ZR = TOP_K * 512 + N_EXPERTS * 8
LANES = 128
VMEM_LIMIT = 56 * 1024 * 1024

NT_DIMS = (((1,), (1,)), ((), ()))


def _cparams(*sem):
    return pltpu.CompilerParams(dimension_semantics=sem, vmem_limit_bytes=VMEM_LIMIT)


def _rms(x):
    return x * lax.rsqrt(jnp.mean(x * x, axis=-1, keepdims=True) + EPS)


def _sigmoid(x):
    return 1.0 / (1.0 + jnp.exp(-x))


def _sigmoid_abs(x):
    return 0.5 * jnp.tanh(0.5 * x) + 0.5


def _split_bf16(x):
    hi = x.astype(BF16)
    return hi, (x - hi.astype(F32)).astype(BF16)


def _adaln_body(c_ref, w_ref, b_ref, o_ref):
    c = c_ref[...]
    cond = c * _sigmoid(c)
    o_ref[...] = jnp.dot(cond, w_ref[...], preferred_element_type=F32,
                         precision=lax.Precision.HIGHEST) + b_ref[...]


def adaln(cond8, ada_w, ada_b):
    depth, d, n = ada_w.shape
    tn = 1536
    return pl.pallas_call(
        _adaln_body,
        grid=(depth, n // tn),
        in_specs=[pl.BlockSpec((8, d), lambda l, j: (0, 0)),
                  pl.BlockSpec((None, d, tn), lambda l, j: (l, 0, j)),
                  pl.BlockSpec((None, 1, tn), lambda l, j: (l, 0, j))],
        out_specs=pl.BlockSpec((None, 8, tn), lambda l, j: (l, 0, j)),
        out_shape=jax.ShapeDtypeStruct((depth, 8, n), F32),
        compiler_params=_cparams("parallel", "parallel"),
        name="adaln",
    )(cond8, ada_w, ada_b.reshape(depth, 1, n))


def _tile_rows(refs, n_lat):
    if len(refs) == 1:
        return refs[0][...]
    return jnp.where(pl.program_id(0) < n_lat, refs[0][...], refs[1][...])


def _inproj_body(*refs, n_x, n_lat):
    (g_ref, sh_ref, sc_ref, w_ref, cs_ref, sa_ref, sb_ref, q_ref, kv_ref, lru_ref, hg_ref) = refs[n_x:]
    h = _rms(_tile_rows(refs[:n_x], n_lat)) * g_ref[...]
    h = h * (1.0 + sc_ref[...]) + sh_ref[...]
    p = jnp.dot(h.astype(BF16), w_ref[...], preferred_element_type=F32)
    cs, sa, sb = cs_ref[...], sa_ref[...], sb_ref[...]

    def rope(t):
        return t * cs + pltpu.roll(t, LANES - 16, 1) * sa + pltpu.roll(t, 16, 1) * sb

    for j in range(4):
        q_ref[:, j * LANES:(j + 1) * LANES] = (
            rope(p[:, j * LANES:(j + 1) * LANES]) * (HEAD_DIM ** -0.5)).astype(BF16)
    k = rope(p[:, 512:640])
    v = p[:, 640:768]
    lo = lax.broadcasted_iota(I32, k.shape, 1) < HEAD_DIM
    kr = pltpu.roll(k, HEAD_DIM, 1)
    vr = pltpu.roll(v, HEAD_DIM, 1)
    kv_ref[:, 0:128] = jnp.where(lo, k, kr).astype(BF16)
    kv_ref[:, 128:256] = jnp.where(lo, kr, k).astype(BF16)
    kv_ref[:, 256:384] = jnp.where(lo, v, vr).astype(BF16)
    kv_ref[:, 384:512] = jnp.where(lo, vr, v).astype(BF16)
    lru_ref[...] = p[:, 768:1280]
    hg_ref[...] = p[:, 1280:2560]


def _stream_specs(xs, n_lat, d):
    if len(xs) == 1:
        return [pl.BlockSpec((TM, d), lambda i: (i, 0))]
    return [pl.BlockSpec((TM, d), lambda i: (jnp.minimum(i, n_lat - 1), 0)),
            pl.BlockSpec((TM, d), lambda i: (0, 0))]


def inproj(xs, g, shift, scale, w_bf, rope_c, rope_a, rope_b, *, tiles_per_batch, n_seg):
    d = xs[0].shape[1]
    n_lat = tiles_per_batch * (n_seg - 1)
    n_tiles = n_lat + 1
    r = n_tiles * TM

    def seg(i):
        return (jnp.minimum(i // tiles_per_batch, n_seg - 1), 0, 0)

    def rope_idx(i):
        return (jnp.where(i < n_lat, i % tiles_per_batch, tiles_per_batch), 0)

    row = lambda w: pl.BlockSpec((TM, w), lambda i: (i, 0))
    return pl.pallas_call(
        functools.partial(_inproj_body, n_x=len(xs), n_lat=n_lat),
        grid=(n_tiles,),
        in_specs=_stream_specs(xs, n_lat, d) + [
                  pl.BlockSpec((1, d), lambda i: (0, 0)),
                  pl.BlockSpec((None, 1, d), seg),
                  pl.BlockSpec((None, 1, d), seg),
                  pl.BlockSpec(w_bf.shape, lambda i: (0, 0)),
                  pl.BlockSpec((TM, LANES), rope_idx),
                  pl.BlockSpec((TM, LANES), rope_idx),
                  pl.BlockSpec((TM, LANES), rope_idx)],
        out_specs=[row(512), row(512), row(512), row(1280)],
        out_shape=[jax.ShapeDtypeStruct((r, 512), BF16),
                   jax.ShapeDtypeStruct((r, 512), BF16),
                   jax.ShapeDtypeStruct((r, 512), F32),
                   jax.ShapeDtypeStruct((r, 1280), F32)],
        compiler_params=_cparams("parallel"),
        name="inproj",
    )(*xs, g, shift, scale, w_bf, rope_c, rope_a, rope_b)


def _softmax_pv(s_parts, v_parts, sink_col):
    def lane_blocks(a):
        return [a[:, j:j + LANES] for j in range(0, a.shape[1], LANES)]

    folded = None
    for s in s_parts:
        for blk in lane_blocks(s):
            folded = blk if folded is None else jnp.maximum(folded, blk)
    m = jnp.maximum(sink_col, jnp.max(folded, axis=-1, keepdims=True))
    psum = None
    o = None
    for s, v in zip(s_parts, v_parts):
        p = jnp.exp(s - m)
        for blk in lane_blocks(p):
            psum = blk if psum is None else psum + blk
        pv = jnp.dot(p.astype(BF16), v, preferred_element_type=F32)
        o = pv if o is None else o + pv
    l = jnp.exp(sink_col - m) + jnp.sum(psum, axis=-1, keepdims=True)
    return o / l


def _attn_body(sink_ref, q_ref, kvm_ref, kvp_ref, kvn_ref, kvc_ref, o_ref, *,
               tiles_per_batch, n_lat, ctx_len, n_batch):
    i = pl.program_id(0)
    blk = WINDOW
    nsub = TM // blk

    def stacked_q(q_sub, g):
        rows = q_sub.shape[0]
        lo = lax.broadcasted_iota(I32, (rows, LANES), 1) < HEAD_DIM
        parts = []
        for jq in (2 * g, 2 * g + 1):
            qg = q_sub[:, jq * LANES:(jq + 1) * LANES]
            parts.append(jnp.where(lo, qg, jnp.zeros_like(qg)))
            parts.append(jnp.where(lo, jnp.zeros_like(qg), qg))
        return jnp.concatenate(parts, axis=0)

    def sink_column(g, rows):
        blk_id = lax.broadcasted_iota(I32, (4 * rows, 1), 0) // rows
        col = jnp.full((4 * rows, 1), sink_ref[4 * g + 3], F32)
        for b in range(3):
            col = jnp.where(blk_id == b, sink_ref[4 * g + b], col)
        return col

    def write_heads(o, g, rows, row0):
        lo = lax.broadcasted_iota(I32, (rows, LANES), 1) < HEAD_DIM
        for t, jq in enumerate((2 * g, 2 * g + 1)):
            o_e0 = o[(2 * t) * rows:(2 * t + 1) * rows]
            o_e1 = o[(2 * t + 1) * rows:(2 * t + 2) * rows]
            o_ref[row0:row0 + rows, jq * LANES:(jq + 1) * LANES] = (
                jnp.where(lo, o_e0, o_e1).astype(BF16))

    @pl.when(i < n_lat)
    def _latent():
        j = i % tiles_per_batch
        r_io = lax.broadcasted_iota(I32, (4 * blk, blk), 0) % blk
        c_io = lax.broadcasted_iota(I32, (4 * blk, blk), 1)
        for sb in range(nsub):
            q_sub = q_ref[sb * blk:(sb + 1) * blk, :]
            has_prev = jnp.logical_or(j > 0, sb > 0)
            has_next = jnp.logical_or(j < tiles_per_batch - 1, sb < nsub - 1)
            m_prev = jnp.logical_and(c_io >= r_io, has_prev)
            m_next = jnp.logical_and(c_io <= r_io, has_next)
            for g in range(2):
                kc, vc = g * LANES, (2 + g) * LANES

                def kv_block(b, col):
                    if b < 0:
                        return kvp_ref[:, col:col + LANES]
                    if b >= nsub:
                        return kvn_ref[:, col:col + LANES]
                    return kvm_ref[b * blk:(b + 1) * blk, col:col + LANES]

                qs = stacked_q(q_sub, g)
                s_prev = lax.dot_general(qs, kv_block(sb - 1, kc), NT_DIMS, preferred_element_type=F32)
                s_own = lax.dot_general(qs, kv_block(sb, kc), NT_DIMS, preferred_element_type=F32)
                s_next = lax.dot_general(qs, kv_block(sb + 1, kc), NT_DIMS, preferred_element_type=F32)
                s_ctx = lax.dot_general(qs, kvc_ref[:, kc:kc + LANES], NT_DIMS, preferred_element_type=F32)
                s_prev = jnp.where(m_prev, s_prev, NEG_INF)
                s_next = jnp.where(m_next, s_next, NEG_INF)
                o = _softmax_pv(
                    [s_prev, s_own, s_next, s_ctx],
                    [kv_block(sb - 1, vc), kv_block(sb, vc), kv_block(sb + 1, vc),
                     kvc_ref[:, vc:vc + LANES]],
                    sink_column(g, blk))
                write_heads(o, g, blk, sb * blk)

    @pl.when(i >= n_lat)
    def _context():
        for bb in range(n_batch):
            r0 = bb * ctx_len
            q_sub = q_ref[r0:r0 + ctx_len, :]
            for g in range(2):
                kc, vc = g * LANES, (2 + g) * LANES
                qs = stacked_q(q_sub, g)
                s = lax.dot_general(qs, kvm_ref[r0:r0 + ctx_len, kc:kc + LANES], NT_DIMS,
                                    preferred_element_type=F32)
                o = _softmax_pv([s], [kvm_ref[r0:r0 + ctx_len, vc:vc + LANES]],
                                sink_column(g, ctx_len))
                write_heads(o, g, ctx_len, r0)


def attention(q, kv, sink, *, tiles_per_batch, n_batch, ctx_len, with_ctx):
    r = q.shape[0]
    n_lat = tiles_per_batch * n_batch
    assert n_batch * ctx_len == TM and r // TM == n_lat + 1
    n_tiles = n_lat + (1 if with_ctx else 0)
    sub = TM // WINDOW
    n_blk128 = r // WINDOW
    ctx_blk0 = (n_lat * TM) // ctx_len

    def prev_idx(i, s):
        return (jnp.maximum(i * sub - 1, 0), 0)

    def next_idx(i, s):
        return (jnp.minimum(i * sub + sub, n_blk128 - 1), 0)

    def ctx_idx(i, s):
        return (ctx_blk0 + jnp.minimum(i // tiles_per_batch, n_batch - 1), 0)

    body = functools.partial(_attn_body, tiles_per_batch=tiles_per_batch, n_lat=n_lat,
                             ctx_len=ctx_len, n_batch=n_batch)
    return pl.pallas_call(
        body,
        grid_spec=pltpu.PrefetchScalarGridSpec(
            num_scalar_prefetch=1,
            grid=(n_tiles,),
            in_specs=[pl.BlockSpec((TM, 512), lambda i, s: (i, 0)),
                      pl.BlockSpec((TM, 512), lambda i, s: (i, 0)),
                      pl.BlockSpec((WINDOW, 512), prev_idx),
                      pl.BlockSpec((WINDOW, 512), next_idx),
                      pl.BlockSpec((ctx_len, 512), ctx_idx)],
            out_specs=pl.BlockSpec((TM, 512), lambda i, s: (i, 0))),
        out_shape=jax.ShapeDtypeStruct((n_tiles * TM, 512), BF16),
        compiler_params=_cparams("parallel"),
        name="attention",
    )(sink, q, kv, kv, kv, kv)


def _scan_maps(n_batch, seq, ctx_len):
    assert ctx_len == TSCAN
    nl = seq // TSCAN
    ctx0 = (n_batch * seq) // TSCAN

    def fwd(b, c):
        return jnp.where(c == 0, ctx0 + b, b * nl + c - 1)

    def bwd(b, c):
        return jnp.where(c == 0, ctx0 + b, b * nl + nl - c)

    return nl, fwd, bwd


def _lru_dir(x_ref, xp_ref, xn_ref, has_prev, has_next, cw_ref, cb_ref, w_ref, br_ref, bi_ref,
             lam_ref, a_scr, b_scr, reverse):
    t = TSCAN
    xp = jnp.where(has_prev, xp_ref[...], 0.0)
    xn = jnp.where(has_next, xn_ref[...], 0.0)
    xpad = jnp.concatenate([xp, x_ref[...], xn], axis=0)
    n = t + 16
    cw = cw_ref[...]
    u = cb_ref[...] + jnp.zeros((t, xpad.shape[1]), F32)
    for j in range(4):
        sh = (2 - j) % n
        rolled = xpad if sh == 0 else pltpu.roll(xpad, sh, 0)
        u = u + rolled[8:8 + t] * cw[j:j + 1, :]
    gates = jnp.dot(u.astype(BF16), w_ref[...], preferred_element_type=F32)
    c = u.shape[1]
    r = _sigmoid_abs(gates[:, :c] + br_ref[...])
    ig = _sigmoid_abs(gates[:, c:] + bi_ref[...])
    nl = -lam_ref[...]
    softplus = jnp.maximum(nl, 0.0) + jnp.log1p(jnp.exp(-jnp.abs(nl)))
    log_a = (-LRU_C) * r * softplus
    a = jnp.exp(log_a)
    mult = jnp.sqrt(jnp.maximum(1.0 - a * a, SQRT_FLOOR))
    bv = mult * (ig * u)

    g8 = t // 8
    a3 = a.reshape(g8, 8, c)
    b3 = bv.reshape(g8, 8, c)
    r8 = lax.broadcasted_iota(I32, (g8, 8, c), 1)
    for s in (1, 2, 4):
        if reverse:
            a_sh, b_sh, ok = pltpu.roll(a3, 8 - s, 1), pltpu.roll(b3, 8 - s, 1), r8 < 8 - s
        else:
            a_sh, b_sh, ok = pltpu.roll(a3, s, 1), pltpu.roll(b3, s, 1), r8 >= s
        b3 = jnp.where(ok, a3 * b_sh + b3, b3)
        a3 = jnp.where(ok, a3 * a_sh, a3)
    a_scr[...] = a3.reshape(t, c)
    b_scr[...] = b3.reshape(t, c)


def _lru_body(xf_ref, xfp_ref, xfn_ref, xb_ref, xbp_ref, xbn_ref, cw_ref, cb_ref, w_ref, br_ref,
              bi_ref, lam_ref, hf_ref, hb_ref, st_ref, a_scr, b_scr, *, nl):
    c = pl.program_id(1)

    @pl.when(c == 0)
    def _init():
        st_ref[...] = jnp.zeros_like(st_ref)

    lat = c > 0
    _lru_dir(xf_ref, xfp_ref, xfn_ref, jnp.logical_and(lat, c > 1), jnp.logical_and(lat, c < nl),
             cw_ref, cb_ref, w_ref.at[0], br_ref.at[0], bi_ref.at[0], lam_ref.at[0],
             a_scr.at[0], b_scr.at[0], False)
    _lru_dir(xb_ref, xbp_ref, xbn_ref, jnp.logical_and(lat, c < nl), jnp.logical_and(lat, c > 1),
             cw_ref, cb_ref, w_ref.at[1], br_ref.at[1], bi_ref.at[1], lam_ref.at[1],
             a_scr.at[1], b_scr.at[1], True)

    g8 = TSCAN // 8

    def group(gi, carry):
        cf, cb = carry
        rf = pl.ds(pl.multiple_of(gi * 8, 8), 8)
        rb = pl.ds(pl.multiple_of((g8 - 1 - gi) * 8, 8), 8)
        hf = b_scr[0, rf, :] + a_scr[0, rf, :] * cf
        hb = b_scr[1, rb, :] + a_scr[1, rb, :] * cb
        hf_ref[rf, :] = hf
        hb_ref[rb, :] = hb
        return (jnp.broadcast_to(hf[7:8, :], hf.shape), jnp.broadcast_to(hb[0:1, :], hb.shape))

    cf, cb = lax.fori_loop(0, g8, group, (st_ref[0], st_ref[1]), unroll=4)
    st_ref[0] = cf
    st_ref[1] = cb


def lru_scan(lru, conv_w, conv_b, w_gates, br, bi, lam, *, n_batch, seq, ctx_len):
    r = lru.shape[0]
    c = lru.shape[1] // 2
    nl, fwd, bwd = _scan_maps(n_batch, seq, ctx_len)
    per = TSCAN // 8
    n8 = r // 8

    def main(m):
        return pl.BlockSpec((TSCAN, c), lambda b, s: (m(b, s), 0))

    def prev(m):
        return pl.BlockSpec((8, c), lambda b, s: (jnp.maximum(m(b, s) * per - 1, 0), 0))

    def nxt(m):
        return pl.BlockSpec((8, c), lambda b, s: (jnp.minimum(m(b, s) * per + per, n8 - 1), 0))

    full = lambda a: pl.BlockSpec(a.shape, lambda b, s: (0,) * a.ndim)
    return pl.pallas_call(
        functools.partial(_lru_body, nl=nl),
        grid=(n_batch, nl + 1),
        in_specs=[main(fwd), prev(fwd), nxt(fwd), main(bwd), prev(bwd), nxt(bwd),
                  full(conv_w), full(conv_b), full(w_gates), full(br), full(bi), full(lam)],
        out_specs=[main(fwd), main(bwd)],
        out_shape=[jax.ShapeDtypeStruct((r, c), F32)] * 2,
        scratch_shapes=[pltpu.VMEM((2, 8, c), F32), pltpu.VMEM((2, TSCAN, c), F32),
                        pltpu.VMEM((2, TSCAN, c), F32)],
        compiler_params=_cparams("parallel", "arbitrary"),
        name="lru_scan",
    )(lru, lru, lru, lru, lru, lru, conv_w, conv_b, w_gates, br, bi, lam)


def _hgrn_dir(q_ref, z_ref, v_ref, lb_ref, st_ref, out_ref, reverse):
    t = TSCAN
    nch = t // HG_CHUNK
    q = q_ref[...]
    z = z_ref[...]
    v = v_ref[...]
    lb = lb_ref[...]
    w = q.shape[1]
    log_f = jnp.log(lb + (1.0 - lb) * _sigmoid(z))
    k = (1.0 - lb) * _sigmoid(-z)

    ri = lax.broadcasted_iota(I32, (t, t), 0)
    ci = lax.broadcasted_iota(I32, (t, t), 1)
    same = (ri // HG_CHUNK) == (ci // HG_CHUNK)
    causal = jnp.logical_and(same, (ci >= ri) if reverse else (ci <= ri))

    rc = lax.broadcasted_iota(I32, (t, w), 0) % HG_CHUNK
    b = log_f
    s = 1
    while s < HG_CHUNK:
        if reverse:
            b = b + jnp.where(rc < HG_CHUNK - s, pltpu.roll(b, t - s, 0), 0.0)
        else:
            b = b + jnp.where(rc >= s, pltpu.roll(b, s, 0), 0.0)
        s *= 2
    tot = jnp.sum(log_f.reshape(nch, HG_CHUNK, w), axis=1, keepdims=True)
    b_last = jnp.broadcast_to(tot, (nch, HG_CHUNK, w)).reshape(t, w)
    b_half = 0.5 * b_last
    q_i = (q * jnp.exp(b - b_half)).astype(BF16)
    k_i = (k * jnp.exp(b_half - b)).astype(BF16)
    q_s = (q * jnp.exp(b)).astype(BF16)
    k_e = (k * jnp.exp(b_last - b)).astype(BF16)
    v_bf = v.astype(BF16)
    lo = lax.broadcasted_iota(I32, (t, LANES), 1) < HEAD_DIM
    chunk_of_row = lax.broadcasted_iota(I32, (t, LANES), 0) // HG_CHUNK
    bd_r = lax.broadcasted_iota(I32, (LANES, LANES), 0) // HEAD_DIM
    bd_c = lax.broadcasted_iota(I32, (LANES, LANES), 1) // HEAD_DIM
    block_diag = bd_r == bd_c

    for p in range(w // LANES):
        cols = slice(p * LANES, (p + 1) * LANES)
        qi_p, ki_p, qs_p, ke_p, v_p = q_i[:, cols], k_i[:, cols], q_s[:, cols], k_e[:, cols], v_bf[:, cols]
        o_half = []
        for e in range(2):
            qm = jnp.where(lo if e == 0 else jnp.logical_not(lo), qi_p, jnp.zeros_like(qi_p))
            att = lax.dot_general(qm, ki_p, NT_DIMS, preferred_element_type=F32)
            att = jnp.where(causal, att, 0.0).astype(BF16)
            o_half.append(jnp.dot(att, v_p, preferred_element_type=F32))
        v_t = v[:, cols].T.astype(BF16)
        st = st_ref[p]
        o_inter = [None] * nch
        order = range(nch - 1, -1, -1) if reverse else range(nch)
        for n in order:
            rows = slice(n * HG_CHUNK, (n + 1) * HG_CHUNK)
            o_inter[n] = lax.dot_general(qs_p[rows], st.astype(BF16), NT_DIMS,
                                         preferred_element_type=F32)
            ke_n = jnp.where(chunk_of_row == n, ke_p, jnp.zeros_like(ke_p))
            kv_t = jnp.dot(v_t, ke_n, preferred_element_type=F32)
            decay = jnp.exp(b_last[n * HG_CHUNK:n * HG_CHUNK + 1, cols])
            st = decay * st + jnp.where(block_diag, kv_t, 0.0)
        st_ref[p] = st
        out_ref[:, cols] = jnp.where(lo, o_half[0], o_half[1]) + jnp.concatenate(o_inter, axis=0)


def _hgrn_body(qf_ref, zf_ref, vf_ref, qb_ref, zb_ref, vb_ref, lb_ref, of_ref, ob_ref, st_ref):
    c = pl.program_id(1)

    @pl.when(c == 0)
    def _init():
        st_ref[...] = jnp.zeros_like(st_ref)

    _hgrn_dir(qf_ref, zf_ref, vf_ref, lb_ref.at[0], st_ref.at[0], of_ref, False)
    _hgrn_dir(qb_ref, zb_ref, vb_ref, lb_ref.at[1], st_ref.at[1], ob_ref, True)


def hgrn_scan(hg, lower_bounds, *, n_batch, seq, ctx_len):
    r = hg.shape[0]
    w = hg.shape[1] // 5
    nl, fwd, bwd = _scan_maps(n_batch, seq, ctx_len)

    def col(m, j):
        return pl.BlockSpec((TSCAN, w), lambda b, s: (m(b, s), j))

    lb3 = lower_bounds.reshape(2, 1, w)
    return pl.pallas_call(
        _hgrn_body,
        grid=(n_batch, nl + 1),
        in_specs=[col(fwd, 0), col(fwd, 1), col(fwd, 3), col(bwd, 0), col(bwd, 2), col(bwd, 3),
                  pl.BlockSpec(lb3.shape, lambda b, s: (0, 0, 0))],
        out_specs=[col(fwd, 0), col(bwd, 0)],
        out_shape=[jax.ShapeDtypeStruct((r, w), F32)] * 2,
        scratch_shapes=[pltpu.VMEM((2, w // LANES, LANES, LANES), F32)],
        compiler_params=_cparams("parallel", "arbitrary"),
        name="hgrn_scan",
    )(hg, hg, hg, hg, hg, hg, lb3)


def _outproj_router_body(*refs, n_x, n_lat):
    (att_ref, hf_ref, hb_ref, lg_ref, of_ref, ob_ref, gg_ref, hn_ref,
     wo_ref, gm_ref, sh_ref, sc_ref, n2_ref, rw_ref, rb_ref,
     xo_ref, h2_ref, idx_ref, gate_ref, rank_ref, cnt_ref) = refs[n_x:]
    lru_y = (hf_ref[...] + hb_ref[...]) * jax.nn.gelu(lg_ref[...], approximate=True)
    o = of_ref[...] + ob_ref[...]
    w = o.shape[1]
    gr = lax.broadcasted_iota(I32, (w, w), 0) // HEAD_DIM
    gc = lax.broadcasted_iota(I32, (w, w), 1) // HEAD_DIM
    head_mean = jnp.where(gr == gc, 1.0 / HEAD_DIM, 0.0).astype(BF16)
    sq_hi, sq_lo = _split_bf16(o * o)
    ms = (jnp.dot(sq_hi, head_mean, preferred_element_type=F32)
          + jnp.dot(sq_lo, head_mean, preferred_element_type=F32))
    gg = gg_ref[...]
    hg_y = o * lax.rsqrt(ms + EPS) * hn_ref[...] * (gg * _sigmoid_abs(gg))
    y = jnp.dot(att_ref[...], wo_ref[0:512, :], preferred_element_type=F32)
    y = y + jnp.dot(lru_y.astype(BF16), wo_ref[512:768, :], preferred_element_type=F32)
    y = y + jnp.dot(hg_y.astype(BF16), wo_ref[768:1024, :], preferred_element_type=F32)
    x = _tile_rows(refs[:n_x], n_lat) + gm_ref[...] * y
    xo_ref[...] = x
    h2 = _rms(x) * n2_ref[...]
    h2 = h2 * (1.0 + sc_ref[...]) + sh_ref[...]
    h2_ref[...] = h2

    logits = lax.dot_general(rw_ref[...], h2, NT_DIMS, preferred_element_type=F32,
                             precision=lax.Precision.HIGHEST) + rb_ref[...]
    ne, tm = logits.shape
    e_io = lax.broadcasted_iota(I32, (ne, tm), 0).astype(F32)
    work = logits
    vals, hots = [], []
    for k in range(TOP_K):
        m = jnp.max(work, axis=0, keepdims=True)
        idx = jnp.min(jnp.where(work == m, e_io, float(ne)), axis=0, keepdims=True)
        hot = e_io == idx
        vals.append(m)
        hots.append(hot)
        idx_ref[k:k + 1, :] = idx.astype(I32)
        work = jnp.where(hot, -jnp.inf, work)
    exps = [jnp.exp(v - vals[0]) for v in vals]
    denom = exps[0] + exps[1] + exps[2] + exps[3]
    for k in range(TOP_K):
        gate_ref[k:k + 1, :] = exps[k] / denom
    z4 = jnp.zeros((8 - TOP_K, tm), F32)
    gate_ref[TOP_K:8, :] = z4
    idx_ref[TOP_K:8, :] = z4.astype(I32)
    rank_ref[TOP_K:8, :] = z4.astype(I32)

    chosen = jnp.logical_or(jnp.logical_or(hots[0], hots[1]), jnp.logical_or(hots[2], hots[3]))
    sr = lax.broadcasted_iota(I32, (tm, tm), 0)
    sc = lax.broadcasted_iota(I32, (tm, tm), 1)
    before = jnp.where(sr < sc, 1.0, 0.0).astype(BF16)
    chosen_f = jnp.where(chosen, 1.0, 0.0)
    prefix = jnp.dot(chosen_f.astype(BF16), before, preferred_element_type=F32)
    for k in range(TOP_K):
        rk = jnp.sum(jnp.where(hots[k], prefix, 0.0), axis=0, keepdims=True)
        rank_ref[k:k + 1, :] = rk.astype(I32)
    cnt_ref[...] = jnp.broadcast_to(jnp.sum(chosen_f, axis=1, keepdims=True), cnt_ref.shape)


def outproj_router(xs, att, hf, hb, lru, of, ob, hg, hn_g, wo_bf, gate_msa, shift_mlp, scale_mlp,
                   n2_g, rw_t, rb, *, tiles_per_batch, n_seg, with_ctx):
    d = xs[0].shape[1]
    n_lat = tiles_per_batch * (n_seg - 1)
    n_tiles = n_lat + (1 if with_ctx else 0)
    r = n_tiles * TM

    def seg(i):
        return (jnp.minimum(i // tiles_per_batch, n_seg - 1), 0, 0)

    row = lambda w, j=0: pl.BlockSpec((TM, w), lambda i: (i, j))
    full = lambda a: pl.BlockSpec(a.shape, lambda i: (0,) * a.ndim)
    modspec = pl.BlockSpec((None, 1, d), seg)
    lane_out = pl.BlockSpec((8, TM), lambda i: (0, i))
    return pl.pallas_call(
        functools.partial(_outproj_router_body, n_x=len(xs), n_lat=n_lat),
        grid=(n_tiles,),
        in_specs=_stream_specs(xs, n_lat, d) + [
                  row(512), row(256), row(256), row(256, 1), row(256), row(256),
                  row(256, 4), full(hn_g), full(wo_bf), modspec, modspec, modspec, full(n2_g),
                  full(rw_t), full(rb)],
        out_specs=[row(d), row(d), lane_out, lane_out, lane_out,
                   pl.BlockSpec((None, N_EXPERTS, LANES), lambda i: (i, 0, 0))],
        out_shape=[jax.ShapeDtypeStruct((r, d), F32), jax.ShapeDtypeStruct((r, d), F32),
                   jax.ShapeDtypeStruct((8, r), I32), jax.ShapeDtypeStruct((8, r), F32),
                   jax.ShapeDtypeStruct((8, r), I32),
                   jax.ShapeDtypeStruct((n_tiles, N_EXPERTS, LANES), F32)],
        compiler_params=_cparams("parallel"),
        name="outproj_router",
    )(*xs, att, hf, hb, lru, of, ob, hg, hn_g, wo_bf, gate_msa, shift_mlp, scale_mlp, n2_g, rw_t, rb)


def _local_pos(idx_ref, rank_ref, toff_ref, k):
    ne = toff_ref.shape[0]
    tm = idx_ref.shape[1]
    e_io = lax.broadcasted_iota(I32, (ne, tm), 0)
    off = jnp.sum(jnp.where(e_io == idx_ref[k:k + 1, :], toff_ref[...], 0.0), axis=0, keepdims=True)
    return off + rank_ref[k:k + 1, :].astype(F32)


def _run_copies(tab_ref, tile, make_copy, sems, slot, start):
    base = tile * LANES

    def per_expert(e, carry):
        src8 = tab_ref[base + e]
        dst8 = tab_ref[base + N_EXPERTS + e]
        n8 = tab_ref[base + 2 * N_EXPERTS + e]

        for b in range(RUN_BITS):
            size = 8 << b
            off8 = (n8 >> (b + 1)) << (b + 1)

            @pl.when(((n8 >> b) & 1) == 1)
            def _():
                cp = make_copy(pl.multiple_of((src8 + off8) * 8, 8),
                               pl.multiple_of((dst8 + off8) * 8, 8), size, sems.at[slot, b])
                if start:
                    cp.start()
                else:
                    cp.wait()
        return carry

    lax.fori_loop(0, N_EXPERTS, per_expert, 0)


def _fill_copies(fill_ref, zero_scr, xs_ref, sems, start):
    def go(cp):
        if start:
            cp.start()
        else:
            cp.wait()

    def per_expert(e, carry):
        t8 = fill_ref[e]
        n8 = fill_ref[N_EXPERTS + e]
        for b in range(TAIL_BITS):
            size = 8 << b
            off8 = (n8 >> (b + 1)) << (b + 1)

            @pl.when(((n8 >> b) & 1) == 1)
            def _():
                go(pltpu.make_async_copy(
                    zero_scr.at[pl.ds(0, size), :],
                    xs_ref.at[pl.ds(pl.multiple_of((t8 + off8) * 8, 8), size), :], sems.at[0, b]))
        return carry

    lax.fori_loop(0, N_EXPERTS, per_expert, 0)

    def per_block(j, carry):
        go(pltpu.make_async_copy(zero_scr, xs_ref.at[pl.ds(pl.multiple_of(j * TB, TB), TB), :],
                                 sems.at[0, TAIL_BITS]))
        return carry

    lax.fori_loop(fill_ref[2 * N_EXPERTS], xs_ref.shape[0] // TB, per_block, 0)


def _dispatch_body(fill_ref, tab_ref, h_ref, idx_ref, rank_ref, toff_ref, xs_ref, z_scr, zero_scr, sems):
    tm = h_ref.shape[0]
    i = pl.program_id(0)
    slot = i % 2

    @pl.when(i == 0)
    def _fill():
        zero_scr[...] = jnp.zeros_like(zero_scr)
        _fill_copies(fill_ref, zero_scr, xs_ref, sems, True)
        _fill_copies(fill_ref, zero_scr, xs_ref, sems, False)

    r_io = lax.broadcasted_iota(I32, (ZR, tm), 0).astype(F32)
    hit = r_io == _local_pos(idx_ref, rank_ref, toff_ref, 0)
    for k in range(1, TOP_K):
        hit = jnp.logical_or(hit, r_io == _local_pos(idx_ref, rank_ref, toff_ref, k))
    perm = jnp.where(hit, 1.0, 0.0).astype(BF16)
    z_scr[slot] = jnp.dot(perm, h_ref[...].astype(BF16), preferred_element_type=F32)

    def copy_from(buf):
        def make_copy(s, d, size, sem):
            return pltpu.make_async_copy(z_scr.at[buf, pl.ds(s, size), :],
                                         xs_ref.at[pl.ds(d, size), :], sem)
        return make_copy

    _run_copies(tab_ref, i, copy_from(slot), sems, slot, True)

    @pl.when(i > 0)
    def _drain_previous():
        _run_copies(tab_ref, i - 1, copy_from(1 - slot), sems, 1 - slot, False)

    @pl.when(i == pl.num_programs(0) - 1)
    def _drain_last():
        _run_copies(tab_ref, i, copy_from(slot), sems, slot, False)


def moe_dispatch(h2, idx, rank, toff_col, tab, fill, p_rows):
    r, d = h2.shape
    n_tiles = r // TM
    lane_in = pl.BlockSpec((8, TM), lambda i, f, t: (0, i))
    return pl.pallas_call(
        _dispatch_body,
        grid_spec=pltpu.PrefetchScalarGridSpec(
            num_scalar_prefetch=2,
            grid=(n_tiles,),
            in_specs=[pl.BlockSpec((TM, d), lambda i, f, t: (i, 0)),
                      lane_in, lane_in,
                      pl.BlockSpec((None, N_EXPERTS, 1), lambda i, f, t: (i, 0, 0))],
            out_specs=pl.BlockSpec(memory_space=pl.ANY),
            scratch_shapes=[pltpu.VMEM((2, ZR, d), F32), pltpu.VMEM((TB, d), F32),
                            pltpu.SemaphoreType.DMA((2, N_COPY_SEMS))]),
        out_shape=jax.ShapeDtypeStruct((p_rows, d), F32),
        compiler_params=_cparams("arbitrary"),
        name="moe_dispatch",
    )(fill, tab, h2, idx, rank, toff_col)


def _expert_body(be_ref, rows_ref, next_ref, x_ref, bgu_ref, bd_ref, wgu_hbm, wd_hbm, y_ref,
                 wgu_stage, wd_stage, wgu_bf, wd_bf, sems, *, e0):
    i = pl.program_id(0)
    n_rows = rows_ref[i]

    def weight_copies(e):
        return (pltpu.make_async_copy(wgu_hbm.at[e], wgu_stage, sems.at[0]),
                pltpu.make_async_copy(wd_hbm.at[e], wd_stage, sems.at[1]))

    @pl.when(i == 0)
    def _first_fetch():
        for cp in weight_copies(be_ref[0] + e0):
            cp.start()

    @pl.when(jnp.logical_and(n_rows > 0,
                             jnp.logical_or(i == 0, be_ref[i] != be_ref[jnp.maximum(i - 1, 0)])))
    def _new_expert():
        for cp in weight_copies(be_ref[i] + e0):
            cp.wait()
        rows = 128

        def chunk(j, carry):
            sl = pl.ds(pl.multiple_of(j * rows, rows), rows)
            wgu_bf[sl, :] = wgu_stage[sl, :].astype(BF16)
            wd_bf[sl, :] = wd_stage[sl, :].astype(BF16)
            return carry

        lax.fori_loop(0, wgu_stage.shape[0] // rows, chunk, 0)

        @pl.when(next_ref[i] >= 0)
        def _prefetch():
            for cp in weight_copies(next_ref[i] + e0):
                cp.start()

    def mlp(rows):
        dff = wd_bf.shape[0]
        gu = jnp.dot(x_ref[0:rows, :].astype(BF16), wgu_bf[...], preferred_element_type=F32) + bgu_ref[...]
        gate = jnp.minimum(gu[:, :dff].astype(BF16), SWIGLU_LIMIT)
        up = jnp.clip(gu[:, dff:].astype(BF16), -SWIGLU_LIMIT, SWIGLU_LIMIT)
        act = (up + 1.0) * (gate * _sigmoid_abs(SWIGLU_ALPHA * gate))
        y_ref[0:rows, :] = jnp.dot(act, wd_bf[...], preferred_element_type=F32) + bd_ref[...]

    for rows in range(TB_STEP, TB + 1, TB_STEP):
        @pl.when(jnp.logical_and(n_rows > rows - TB_STEP, n_rows <= rows))
        def _piece(rows=rows):
            mlp(rows)
            if rows < TB:
                y_ref[rows:TB, :] = jnp.zeros((TB - rows, y_ref.shape[1]), F32)

    @pl.when(n_rows == 0)
    def _unused():
        y_ref[...] = jnp.zeros_like(y_ref)


def moe_experts(xs_sorted, blk_e, blk_rows, blk_next, wgu, bgu, wd, bd, layer):
    p_rows, d = xs_sorted.shape
    depth, ne, _, dgu = wgu.shape
    dff = wd.shape[2]
    assert dff == d

    def expert(i, be, nr, nx):
        return (layer * ne + be[i], 0, 0)

    return pl.pallas_call(
        functools.partial(_expert_body, e0=layer * ne),
        grid_spec=pltpu.PrefetchScalarGridSpec(
            num_scalar_prefetch=3,
            grid=(p_rows // TB,),
            in_specs=[pl.BlockSpec((TB, d), lambda i, be, nr, nx: (i, 0)),
                      pl.BlockSpec((None, 1, dgu), expert),
                      pl.BlockSpec((None, 1, d), expert),
                      pl.BlockSpec(memory_space=pl.ANY),
                      pl.BlockSpec(memory_space=pl.ANY)],
            out_specs=pl.BlockSpec((TB, d), lambda i, be, nr, nx: (i, 0)),
            scratch_shapes=[pltpu.VMEM((d, dgu), F32), pltpu.VMEM((dff, d), F32),
                            pltpu.VMEM((d, dgu), BF16), pltpu.VMEM((dff, d), BF16),
                            pltpu.SemaphoreType.DMA((2,))]),
        out_shape=jax.ShapeDtypeStruct((p_rows, d), F32),
        compiler_params=_cparams("arbitrary"),
        name="moe_experts",
    )(blk_e, blk_rows, blk_next, xs_sorted, bgu.reshape(depth * ne, 1, dgu), bd.reshape(depth * ne, 1, d),
      wgu.reshape(depth * ne, d, dgu), wd.reshape(depth * ne, dff, d))


def _combine_body(tab_ref, x_ref, idx_ref, rank_ref, gate_ref, toff_ref, gm_ref, fg_ref, ys_ref,
                  o_ref, zy_scr, sems, *, final):
    tm = x_ref.shape[0]
    i = pl.program_id(0)
    slot = i % 2

    def copy_into(buf):
        def make_copy(s, d, size, sem):
            return pltpu.make_async_copy(ys_ref.at[pl.ds(d, size), :],
                                         zy_scr.at[buf, pl.ds(s, size), :], sem)
        return make_copy

    def fetch(tile, buf):
        zy_scr[buf, TOP_K * tm:ZR, :] = jnp.zeros((ZR - TOP_K * tm, zy_scr.shape[2]), F32)
        _run_copies(tab_ref, tile, copy_into(buf), sems, buf, True)

    @pl.when(i == 0)
    def _first():
        fetch(0, 0)

    @pl.when(i + 1 < pl.num_programs(0))
    def _prefetch_next():
        fetch(i + 1, 1 - slot)

    pos = [_local_pos(idx_ref, rank_ref, toff_ref, k) for k in range(TOP_K)]
    packed = jnp.concatenate(pos + [gate_ref[0:TOP_K, :], jnp.zeros((LANES - 2 * TOP_K, tm), F32)],
                             axis=0)
    cols = packed.T
    c_io = lax.broadcasted_iota(I32, (tm, ZR), 1).astype(F32)
    weights = jnp.zeros((tm, ZR), F32)
    for k in range(TOP_K):
        weights = jnp.where(c_io == cols[:, k:k + 1], cols[:, TOP_K + k:TOP_K + k + 1], weights)
    _run_copies(tab_ref, i, copy_into(slot), sems, slot, False)
    acc = jnp.dot(weights.astype(BF16), zy_scr[slot].astype(BF16), preferred_element_type=F32)
    x = x_ref[...] + gm_ref[...] * acc
    if final:
        x = _rms(x) * fg_ref[...]
    o_ref[...] = x


def moe_combine(xs, idx, rank, gates, toff_col, tab, ys, gate_mlp, final_g, *, tiles_per_batch,
                n_seg, n_tiles, final):
    d = xs.shape[1]

    def seg(i):
        return (jnp.minimum(i // tiles_per_batch, n_seg - 1), 0, 0)

    lane_in = pl.BlockSpec((8, TM), lambda i, t: (0, i))
    return pl.pallas_call(
        functools.partial(_combine_body, final=final),
        grid_spec=pltpu.PrefetchScalarGridSpec(
            num_scalar_prefetch=1,
            grid=(n_tiles,),
            in_specs=[pl.BlockSpec((TM, d), lambda i, t: (i, 0)),
                      lane_in, lane_in, lane_in,
                      pl.BlockSpec((None, N_EXPERTS, 1), lambda i, t: (i, 0, 0)),
                      pl.BlockSpec((None, 1, d), lambda i, t: seg(i)),
                      pl.BlockSpec((1, d), lambda i, t: (0, 0)),
                      pl.BlockSpec(memory_space=pl.ANY)],
            out_specs=pl.BlockSpec((TM, d), lambda i, t: (i, 0)),
            scratch_shapes=[pltpu.VMEM((2, ZR, d), F32), pltpu.SemaphoreType.DMA((2, N_COPY_SEMS))]),
        out_shape=jax.ShapeDtypeStruct((n_tiles * TM, d), F32),
        compiler_params=_cparams("arbitrary"),
        name="moe_combine",
    )(tab, xs, idx, rank, gates, toff_col, gate_mlp, final_g, ys)


def _rope_tables(seq):
    nf = HEAD_DIM // 4
    pos = jnp.arange(seq)
    rows = (pos // GRID_W).astype(F32)
    cols = (pos % GRID_W).astype(F32)
    inv_freq = ROPE_BASE ** (-jnp.arange(nf, dtype=F32) / nf)
    d = jnp.arange(LANES) % HEAD_DIM
    axis = d // (2 * nf)
    half = (d // nf) % 2
    f = d % nf
    ang = jnp.where(axis[None, :] == 0, rows[:, None], cols[:, None]) * inv_freq[f][None, :]
    cs, sn = jnp.cos(ang), jnp.sin(ang)
    ca = jnp.where(half[None, :] == 0, -sn, 0.0)
    cb = jnp.where(half[None, :] == 1, sn, 0.0)
    pad1 = jnp.ones((TM, LANES), F32)
    pad0 = jnp.zeros((TM, LANES), F32)
    return (jnp.concatenate([cs, pad1]), jnp.concatenate([ca, pad0]), jnp.concatenate([cb, pad0]))


def _block_diag(w):
    n, c, _ = w.shape
    eye = jnp.eye(n, dtype=w.dtype)
    return (eye[:, None, :, None] * w[:, :, None, :]).reshape(n * c, n * c)


def _moe_layout(tile_counts, n_blocks):
    cnt8 = (tile_counts.astype(I32) + 7) // 8 * 8
    toff = jnp.cumsum(cnt8, axis=1) - cnt8
    goff = jnp.cumsum(cnt8, axis=0) - cnt8
    padded = (jnp.sum(cnt8, axis=0) + TB - 1) // TB * TB
    pends = jnp.cumsum(padded)
    dst = (pends - padded)[None, :] + goff
    n_tiles = cnt8.shape[0]
    tab = jnp.concatenate([toff, dst, cnt8, jnp.zeros((n_tiles, LANES - 3 * N_EXPERTS), I32)], axis=1) // 8
    blk_start = jnp.arange(n_blocks, dtype=I32) * TB
    blk_e = jnp.minimum(jnp.sum((pends[None, :] <= blk_start[:, None]).astype(I32), axis=1), N_EXPERTS - 1)
    n_used = (pends[-1:] // TB).astype(I32)
    total = jnp.sum(cnt8, axis=0)
    region_end = pends - padded + total
    of_blk = blk_e[:, None] == jnp.arange(N_EXPERTS, dtype=I32)[None, :]
    blk_rows = jnp.clip(jnp.sum(jnp.where(of_blk, region_end[None, :], 0), axis=1) - blk_start, 0, TB)
    e_ids = jnp.arange(N_EXPERTS, dtype=I32)
    later = (e_ids[None, :] > e_ids[:, None]) & (padded[None, :] > 0)
    next_e = jnp.min(jnp.where(later, e_ids[None, :], N_EXPERTS), axis=1)
    next_e = jnp.where(next_e == N_EXPERTS, -1, next_e)
    blk_next = jnp.sum(jnp.where(of_blk, next_e[None, :], 0), axis=1)
    fill = jnp.concatenate([(pends - padded + total) // 8, (padded - total) // 8, n_used,
                            jnp.zeros((LANES - 2 * N_EXPERTS - 1,), I32)]).astype(I32)
    return (tab.astype(I32).reshape(n_tiles * LANES), toff.astype(F32).reshape(n_tiles, N_EXPERTS, 1),
            fill, blk_e.astype(I32), blk_rows.astype(I32), blk_next.astype(I32))


def kernel(x, c, ctx, c_ctx, ada_w, ada_b, norm1_g, w_in, attn_sink, conv_w, conv_b, lru_wr, lru_br,
           lru_wi, lru_bi, lru_lambda, hgrn_lb_logits, hgrn_norm_g, w_out, norm2_g, router_w,
           router_b, moe_w_gu, moe_b_gu, moe_w_down, moe_b_down, final_g):
    n_batch, seq, d = x.shape
    ctx_len = ctx.shape[1]
    depth = ada_w.shape[0]
    assert n_batch * ctx_len == TM and seq % TM == 0 and ctx_len == TSCAN
    tiles_per_batch = seq // TM
    n_seg = n_batch + 1
    n_lat = n_batch * tiles_per_batch
    r = n_batch * seq + n_batch * ctx_len
    n_blocks = -(-(r * TOP_K + (r // TM) * N_EXPERTS * 7) // TB) + N_EXPERTS
    p_rows = n_blocks * TB

    xs = (x.reshape(n_batch * seq, d), ctx.reshape(n_batch * ctx_len, d))
    cond8 = jnp.zeros((8, d), F32).at[:n_batch].set(c).at[n_batch].set(c_ctx)
    mods = adaln(cond8, ada_w, ada_b)
    rope_c, rope_a, rope_b = _rope_tables(seq)
    lb_p = jax.nn.softmax(hgrn_lb_logits.astype(F32), axis=0)
    lower_bounds = jnp.cumsum(lb_p, axis=0) - lb_p[0]

    out = None
    for layer in range(depth):
        mod = [mods[layer, :, j * d:(j + 1) * d].reshape(8, 1, d) for j in range(6)]
        q, kv, lru, hg = inproj(xs, norm1_g[layer].reshape(1, d), mod[0], mod[1],
                                w_in[layer].astype(BF16), rope_c, rope_a, rope_b,
                                tiles_per_batch=tiles_per_batch, n_seg=n_seg)
        final = layer == depth - 1
        att = attention(q, kv, attn_sink[layer], tiles_per_batch=tiles_per_batch,
                        n_batch=n_batch, ctx_len=ctx_len, with_ctx=not final)
        w_gates = jnp.stack([jnp.concatenate([_block_diag(lru_wr[layer, dd]),
                                              _block_diag(lru_wi[layer, dd])], axis=1)
                             for dd in range(2)]).astype(BF16)
        cw = lru_br.shape[-1]
        hf, hb = lru_scan(lru, conv_w[layer], conv_b[layer].reshape(1, cw), w_gates,
                          lru_br[layer].reshape(2, 1, cw), lru_bi[layer].reshape(2, 1, cw),
                          lru_lambda[layer].reshape(2, 1, cw),
                          n_batch=n_batch, seq=seq, ctx_len=ctx_len)
        of, ob = hgrn_scan(hg, lower_bounds[layer], n_batch=n_batch, seq=seq, ctx_len=ctx_len)
        x_mid, h2, idx, gates, rank, counts = outproj_router(
            xs, att, hf, hb, lru, of, ob, hg, hgrn_norm_g[layer].reshape(1, -1),
            w_out[layer].astype(BF16), mod[2], mod[3], mod[4], norm2_g[layer].reshape(1, d),
            router_w[layer].T, router_b[layer].reshape(N_EXPERTS, 1),
            tiles_per_batch=tiles_per_batch, n_seg=n_seg, with_ctx=not final)
        tab, toff_col, fill, blk_e, blk_rows, blk_next = _moe_layout(counts[:, :, 0], n_blocks)
        xs_sorted = moe_dispatch(h2, idx, rank, toff_col, tab, fill, p_rows)
        ys = moe_experts(xs_sorted, blk_e, blk_rows, blk_next, moe_w_gu, moe_b_gu, moe_w_down, moe_b_down, layer)
        res = moe_combine(x_mid, idx, rank, gates, toff_col, tab, ys, mod[5], final_g.reshape(1, d),
                          tiles_per_batch=tiles_per_batch, n_seg=n_seg,
                          n_tiles=n_lat if final else n_lat + 1, final=final)
        if final:
            out = res.reshape(n_batch, seq, d)
        else:
            xs = (res,)
    return out
```

```python
import functools

import jax
import jax.numpy as jnp
from jax import lax
from jax.experimental import pallas as pl
from jax.experimental.pallas import tpu as pltpu

F32 = jnp.float32
BF16 = jnp.bfloat16
I32 = jnp.int32

EPS = 1e-6
NEG_INF = -1e30
GRID_W = 64
ROPE_BASE = 10000.0
HEAD_DIM = 64
N_HEADS = 8
WINDOW = 128
LRU_C = 8.0
SQRT_FLOOR = 1e-12
HG_CHUNK = 32
N_EXPERTS = 32
TOP_K = 4
SWIGLU_LIMIT = 7.0
SWIGLU_ALPHA = 1.702

TM = 512
TSCAN = 256
TB = 1024
TB_STEP = 256
RUN_BITS = 7
TAIL_BITS = 7
N_COPY_SEMS = max(RUN_BITS, TAIL_BITS + 1)
ZR = TOP_K * 512 + N_EXPERTS * 8
LANES = 128
VMEM_LIMIT = 56 * 1024 * 1024

NT_DIMS = (((1,), (1,)), ((), ()))


def _cparams(*sem):
    return pltpu.CompilerParams(dimension_semantics=sem, vmem_limit_bytes=VMEM_LIMIT)


def _rms(x):
    return x * lax.rsqrt(jnp.mean(x * x, axis=-1, keepdims=True) + EPS)


def _sigmoid(x):
    return 1.0 / (1.0 + jnp.exp(-x))


def _sigmoid_abs(x):
    return 0.5 * jnp.tanh(0.5 * x) + 0.5


def _alternate(stages):
    stages = list(stages)
    while stages:
        for g in list(stages):
            if next(g, "done") == "done":
                stages.remove(g)


def _split_bf16(x):
    hi = x.astype(BF16)
    return hi, (x - hi.astype(F32)).astype(BF16)


def _adaln_body(c_ref, w_ref, b_ref, o_ref):
    c = c_ref[...]
    cond = c * _sigmoid(c)
    o_ref[...] = jnp.dot(cond, w_ref[...], preferred_element_type=F32,
                         precision=lax.Precision.HIGHEST) + b_ref[...]


def adaln(cond8, ada_w, ada_b):
    depth, d, n = ada_w.shape
    tn = 1536
    return pl.pallas_call(
        _adaln_body,
        grid=(depth, n // tn),
        in_specs=[pl.BlockSpec((8, d), lambda l, j: (0, 0)),
                  pl.BlockSpec((None, d, tn), lambda l, j: (l, 0, j)),
                  pl.BlockSpec((None, 1, tn), lambda l, j: (l, 0, j))],
        out_specs=pl.BlockSpec((None, 8, tn), lambda l, j: (l, 0, j)),
        out_shape=jax.ShapeDtypeStruct((depth, 8, n), F32),
        compiler_params=_cparams("parallel", "parallel"),
        name="adaln",
    )(cond8, ada_w, ada_b.reshape(depth, 1, n))


def _tile_rows(refs, n_lat):
    if len(refs) == 1:
        return refs[0][...]
    return jnp.where(pl.program_id(0) < n_lat, refs[0][...], refs[1][...])


def _inproj_body(*refs, n_x, n_lat):
    (g_ref, sh_ref, sc_ref, w_ref, cs_ref, sa_ref, sb_ref, q_ref, kv_ref, lru_ref, hg_ref) = refs[n_x:]
    h = _rms(_tile_rows(refs[:n_x], n_lat)) * g_ref[...]
    h = h * (1.0 + sc_ref[...]) + sh_ref[...]
    p = jnp.dot(h.astype(BF16), w_ref[...], preferred_element_type=F32)
    cs, sa, sb = cs_ref[...], sa_ref[...], sb_ref[...]

    def rope(t):
        return t * cs + pltpu.roll(t, LANES - 16, 1) * sa + pltpu.roll(t, 16, 1) * sb

    for j in range(4):
        q_ref[:, j * LANES:(j + 1) * LANES] = (
            rope(p[:, j * LANES:(j + 1) * LANES]) * (HEAD_DIM ** -0.5)).astype(BF16)
    k = rope(p[:, 512:640])
    v = p[:, 640:768]
    lo = lax.broadcasted_iota(I32, k.shape, 1) < HEAD_DIM
    kr = pltpu.roll(k, HEAD_DIM, 1)
    vr = pltpu.roll(v, HEAD_DIM, 1)
    kv_ref[:, 0:128] = jnp.where(lo, k, kr).astype(BF16)
    kv_ref[:, 128:256] = jnp.where(lo, kr, k).astype(BF16)
    kv_ref[:, 256:384] = jnp.where(lo, v, vr).astype(BF16)
    kv_ref[:, 384:512] = jnp.where(lo, vr, v).astype(BF16)
    lru_ref[...] = p[:, 768:1280]
    hg_ref[...] = p[:, 1280:2560]


def _stream_specs(xs, n_lat, d):
    if len(xs) == 1:
        return [pl.BlockSpec((TM, d), lambda i: (i, 0))]
    return [pl.BlockSpec((TM, d), lambda i: (jnp.minimum(i, n_lat - 1), 0)),
            pl.BlockSpec((TM, d), lambda i: (0, 0))]


def inproj(xs, g, shift, scale, w_bf, rope_c, rope_a, rope_b, *, tiles_per_batch, n_seg):
    d = xs[0].shape[1]
    n_lat = tiles_per_batch * (n_seg - 1)
    n_tiles = n_lat + 1
    r = n_tiles * TM

    def seg(i):
        return (jnp.minimum(i // tiles_per_batch, n_seg - 1), 0, 0)

    def rope_idx(i):
        return (jnp.where(i < n_lat, i % tiles_per_batch, tiles_per_batch), 0)

    row = lambda w: pl.BlockSpec((TM, w), lambda i: (i, 0))
    return pl.pallas_call(
        functools.partial(_inproj_body, n_x=len(xs), n_lat=n_lat),
        grid=(n_tiles,),
        in_specs=_stream_specs(xs, n_lat, d) + [
                  pl.BlockSpec((1, d), lambda i: (0, 0)),
                  pl.BlockSpec((None, 1, d), seg),
                  pl.BlockSpec((None, 1, d), seg),
                  pl.BlockSpec(w_bf.shape, lambda i: (0, 0)),
                  pl.BlockSpec((TM, LANES), rope_idx),
                  pl.BlockSpec((TM, LANES), rope_idx),
                  pl.BlockSpec((TM, LANES), rope_idx)],
        out_specs=[row(512), row(512), row(512), row(1280)],
        out_shape=[jax.ShapeDtypeStruct((r, 512), BF16),
                   jax.ShapeDtypeStruct((r, 512), BF16),
                   jax.ShapeDtypeStruct((r, 512), F32),
                   jax.ShapeDtypeStruct((r, 1280), F32)],
        compiler_params=_cparams("parallel"),
        name="inproj",
    )(*xs, g, shift, scale, w_bf, rope_c, rope_a, rope_b)


def _softmax_pv(s_parts, v_parts, sink_col):
    def lane_blocks(a):
        return [a[:, j:j + LANES] for j in range(0, a.shape[1], LANES)]

    folded = None
    for s in s_parts:
        for blk in lane_blocks(s):
            folded = blk if folded is None else jnp.maximum(folded, blk)
    m = jnp.maximum(sink_col, jnp.max(folded, axis=-1, keepdims=True))
    psum = None
    o = None
    for s, v in zip(s_parts, v_parts):
        p = jnp.exp(s - m)
        for blk in lane_blocks(p):
            psum = blk if psum is None else psum + blk
        pv = jnp.dot(p.astype(BF16), v, preferred_element_type=F32)
        o = pv if o is None else o + pv
    l = jnp.exp(sink_col - m) + jnp.sum(psum, axis=-1, keepdims=True)
    return o / l


def _attn_body(sink_ref, q_ref, kvm_ref, kvp_ref, kvn_ref, kvc_ref, o_ref, *,
               tiles_per_batch, n_lat, ctx_len, n_batch):
    i = pl.program_id(0)
    blk = WINDOW
    nsub = TM // blk

    def stacked_q(q_sub, g):
        rows = q_sub.shape[0]
        lo = lax.broadcasted_iota(I32, (rows, LANES), 1) < HEAD_DIM
        parts = []
        for jq in (2 * g, 2 * g + 1):
            qg = q_sub[:, jq * LANES:(jq + 1) * LANES]
            parts.append(jnp.where(lo, qg, jnp.zeros_like(qg)))
            parts.append(jnp.where(lo, jnp.zeros_like(qg), qg))
        return jnp.concatenate(parts, axis=0)

    def sink_column(g, rows):
        blk_id = lax.broadcasted_iota(I32, (4 * rows, 1), 0) // rows
        col = jnp.full((4 * rows, 1), sink_ref[4 * g + 3], F32)
        for b in range(3):
            col = jnp.where(blk_id == b, sink_ref[4 * g + b], col)
        return col

    def write_heads(o, g, rows, row0):
        lo = lax.broadcasted_iota(I32, (rows, LANES), 1) < HEAD_DIM
        for t, jq in enumerate((2 * g, 2 * g + 1)):
            o_e0 = o[(2 * t) * rows:(2 * t + 1) * rows]
            o_e1 = o[(2 * t + 1) * rows:(2 * t + 2) * rows]
            o_ref[row0:row0 + rows, jq * LANES:(jq + 1) * LANES] = (
                jnp.where(lo, o_e0, o_e1).astype(BF16))

    @pl.when(i < n_lat)
    def _latent():
        j = i % tiles_per_batch
        r_io = lax.broadcasted_iota(I32, (4 * blk, blk), 0) % blk
        c_io = lax.broadcasted_iota(I32, (4 * blk, blk), 1)
        def kv_block(b, col):
            if b < 0:
                return kvp_ref[:, col:col + LANES]
            if b >= nsub:
                return kvn_ref[:, col:col + LANES]
            return kvm_ref[b * blk:(b + 1) * blk, col:col + LANES]

        def unit(sb, g):
            kc, vc = g * LANES, (2 + g) * LANES
            q_sub = q_ref[sb * blk:(sb + 1) * blk, :]
            has_prev = jnp.logical_or(j > 0, sb > 0)
            has_next = jnp.logical_or(j < tiles_per_batch - 1, sb < nsub - 1)
            qs = stacked_q(q_sub, g)
            s_prev = lax.dot_general(qs, kv_block(sb - 1, kc), NT_DIMS, preferred_element_type=F32)
            s_own = lax.dot_general(qs, kv_block(sb, kc), NT_DIMS, preferred_element_type=F32)
            s_next = lax.dot_general(qs, kv_block(sb + 1, kc), NT_DIMS, preferred_element_type=F32)
            s_ctx = lax.dot_general(qs, kvc_ref[:, kc:kc + LANES], NT_DIMS, preferred_element_type=F32)
            yield
            s_prev = jnp.where(jnp.logical_and(c_io >= r_io, has_prev), s_prev, NEG_INF)
            s_next = jnp.where(jnp.logical_and(c_io <= r_io, has_next), s_next, NEG_INF)
            o = _softmax_pv(
                [s_prev, s_own, s_next, s_ctx],
                [kv_block(sb - 1, vc), kv_block(sb, vc), kv_block(sb + 1, vc),
                 kvc_ref[:, vc:vc + LANES]],
                sink_column(g, blk))
            yield
            write_heads(o, g, blk, sb * blk)

        for sb in range(0, nsub, 2):
            _alternate([unit(sb, 0), unit(sb, 1), unit(sb + 1, 0), unit(sb + 1, 1)])

    @pl.when(i >= n_lat)
    def _context():
        for bb in range(n_batch):
            r0 = bb * ctx_len
            q_sub = q_ref[r0:r0 + ctx_len, :]
            for g in range(2):
                kc, vc = g * LANES, (2 + g) * LANES
                qs = stacked_q(q_sub, g)
                s = lax.dot_general(qs, kvm_ref[r0:r0 + ctx_len, kc:kc + LANES], NT_DIMS,
                                    preferred_element_type=F32)
                o = _softmax_pv([s], [kvm_ref[r0:r0 + ctx_len, vc:vc + LANES]],
                                sink_column(g, ctx_len))
                write_heads(o, g, ctx_len, r0)


def attention(q, kv, sink, *, tiles_per_batch, n_batch, ctx_len, with_ctx):
    r = q.shape[0]
    n_lat = tiles_per_batch * n_batch
    assert n_batch * ctx_len == TM and r // TM == n_lat + 1
    n_tiles = n_lat + (1 if with_ctx else 0)
    sub = TM // WINDOW
    n_blk128 = r // WINDOW
    ctx_blk0 = (n_lat * TM) // ctx_len

    def prev_idx(i, s):
        return (jnp.maximum(i * sub - 1, 0), 0)

    def next_idx(i, s):
        return (jnp.minimum(i * sub + sub, n_blk128 - 1), 0)

    def ctx_idx(i, s):
        return (ctx_blk0 + jnp.minimum(i // tiles_per_batch, n_batch - 1), 0)

    body = functools.partial(_attn_body, tiles_per_batch=tiles_per_batch, n_lat=n_lat,
                             ctx_len=ctx_len, n_batch=n_batch)
    return pl.pallas_call(
        body,
        grid_spec=pltpu.PrefetchScalarGridSpec(
            num_scalar_prefetch=1,
            grid=(n_tiles,),
            in_specs=[pl.BlockSpec((TM, 512), lambda i, s: (i, 0)),
                      pl.BlockSpec((TM, 512), lambda i, s: (i, 0)),
                      pl.BlockSpec((WINDOW, 512), prev_idx),
                      pl.BlockSpec((WINDOW, 512), next_idx),
                      pl.BlockSpec((ctx_len, 512), ctx_idx)],
            out_specs=pl.BlockSpec((TM, 512), lambda i, s: (i, 0))),
        out_shape=jax.ShapeDtypeStruct((n_tiles * TM, 512), BF16),
        compiler_params=_cparams("parallel"),
        name="attention",
    )(sink, q, kv, kv, kv, kv)


def _scan_maps(n_batch, seq, ctx_len):
    assert ctx_len == TSCAN
    nl = seq // TSCAN
    ctx0 = (n_batch * seq) // TSCAN

    def fwd(b, c):
        return jnp.where(c == 0, ctx0 + b, b * nl + c - 1)

    def bwd(b, c):
        return jnp.where(c == 0, ctx0 + b, b * nl + nl - c)

    return nl, fwd, bwd


def _lru_dir(x_ref, xp_ref, xn_ref, has_prev, has_next, cw_ref, cb_ref, w_ref, br_ref, bi_ref,
             lam_ref, a_scr, b_scr, reverse):
    t = TSCAN
    xp = jnp.where(has_prev, xp_ref[...], 0.0)
    xn = jnp.where(has_next, xn_ref[...], 0.0)
    xpad = jnp.concatenate([xp, x_ref[...], xn], axis=0)
    n = t + 16
    cw = cw_ref[...]
    u = cb_ref[...] + jnp.zeros((t, xpad.shape[1]), F32)
    for j in range(4):
        sh = (2 - j) % n
        rolled = xpad if sh == 0 else pltpu.roll(xpad, sh, 0)
        u = u + rolled[8:8 + t] * cw[j:j + 1, :]
    gates = jnp.dot(u.astype(BF16), w_ref[...], preferred_element_type=F32)
    yield
    c = u.shape[1]
    r = _sigmoid_abs(gates[:, :c] + br_ref[...])
    ig = _sigmoid_abs(gates[:, c:] + bi_ref[...])
    nl = -lam_ref[...]
    softplus = jnp.maximum(nl, 0.0) + jnp.log1p(jnp.exp(-jnp.abs(nl)))
    log_a = (-LRU_C) * r * softplus
    a = jnp.exp(log_a)
    mult = jnp.sqrt(jnp.maximum(1.0 - a * a, SQRT_FLOOR))
    bv = mult * (ig * u)
    yield

    g8 = t // 8
    a3 = a.reshape(g8, 8, c)
    b3 = bv.reshape(g8, 8, c)
    r8 = lax.broadcasted_iota(I32, (g8, 8, c), 1)
    for s in (1, 2, 4):
        if reverse:
            a_sh, b_sh, ok = pltpu.roll(a3, 8 - s, 1), pltpu.roll(b3, 8 - s, 1), r8 < 8 - s
        else:
            a_sh, b_sh, ok = pltpu.roll(a3, s, 1), pltpu.roll(b3, s, 1), r8 >= s
        b3 = jnp.where(ok, a3 * b_sh + b3, b3)
        a3 = jnp.where(ok, a3 * a_sh, a3)
    a_scr[...] = a3.reshape(t, c)
    b_scr[...] = b3.reshape(t, c)


def _lru_body(xf_ref, xfp_ref, xfn_ref, xb_ref, xbp_ref, xbn_ref, cw_ref, cb_ref, w_ref, br_ref,
              bi_ref, lam_ref, hf_ref, hb_ref, st_ref, a_scr, b_scr, *, nl):
    c = pl.program_id(1)

    @pl.when(c == 0)
    def _init():
        st_ref[...] = jnp.zeros_like(st_ref)

    lat = c > 0
    _alternate([
        _lru_dir(xf_ref, xfp_ref, xfn_ref, jnp.logical_and(lat, c > 1), jnp.logical_and(lat, c < nl),
                 cw_ref, cb_ref, w_ref.at[0], br_ref.at[0], bi_ref.at[0], lam_ref.at[0],
                 a_scr.at[0], b_scr.at[0], False),
        _lru_dir(xb_ref, xbp_ref, xbn_ref, jnp.logical_and(lat, c < nl), jnp.logical_and(lat, c > 1),
                 cw_ref, cb_ref, w_ref.at[1], br_ref.at[1], bi_ref.at[1], lam_ref.at[1],
                 a_scr.at[1], b_scr.at[1], True)])

    g8 = TSCAN // 8

    def group(gi, carry):
        cf, cb = carry
        rf = pl.ds(pl.multiple_of(gi * 8, 8), 8)
        rb = pl.ds(pl.multiple_of((g8 - 1 - gi) * 8, 8), 8)
        hf = b_scr[0, rf, :] + a_scr[0, rf, :] * cf
        hb = b_scr[1, rb, :] + a_scr[1, rb, :] * cb
        hf_ref[rf, :] = hf
        hb_ref[rb, :] = hb
        return (jnp.broadcast_to(hf[7:8, :], hf.shape), jnp.broadcast_to(hb[0:1, :], hb.shape))

    cf, cb = lax.fori_loop(0, g8, group, (st_ref[0], st_ref[1]), unroll=4)
    st_ref[0] = cf
    st_ref[1] = cb


def lru_scan(lru, conv_w, conv_b, w_gates, br, bi, lam, *, n_batch, seq, ctx_len):
    r = lru.shape[0]
    c = lru.shape[1] // 2
    nl, fwd, bwd = _scan_maps(n_batch, seq, ctx_len)
    per = TSCAN // 8
    n8 = r // 8

    def main(m):
        return pl.BlockSpec((TSCAN, c), lambda b, s: (m(b, s), 0))

    def prev(m):
        return pl.BlockSpec((8, c), lambda b, s: (jnp.maximum(m(b, s) * per - 1, 0), 0))

    def nxt(m):
        return pl.BlockSpec((8, c), lambda b, s: (jnp.minimum(m(b, s) * per + per, n8 - 1), 0))

    full = lambda a: pl.BlockSpec(a.shape, lambda b, s: (0,) * a.ndim)
    return pl.pallas_call(
        functools.partial(_lru_body, nl=nl),
        grid=(n_batch, nl + 1),
        in_specs=[main(fwd), prev(fwd), nxt(fwd), main(bwd), prev(bwd), nxt(bwd),
                  full(conv_w), full(conv_b), full(w_gates), full(br), full(bi), full(lam)],
        out_specs=[main(fwd), main(bwd)],
        out_shape=[jax.ShapeDtypeStruct((r, c), F32)] * 2,
        scratch_shapes=[pltpu.VMEM((2, 8, c), F32), pltpu.VMEM((2, TSCAN, c), F32),
                        pltpu.VMEM((2, TSCAN, c), F32)],
        compiler_params=_cparams("parallel", "arbitrary"),
        name="lru_scan",
    )(lru, lru, lru, lru, lru, lru, conv_w, conv_b, w_gates, br, bi, lam)


def _hgrn_dir(q_ref, z_ref, v_ref, lb_ref, st_ref, out_ref, reverse):
    t = TSCAN
    nch = t // HG_CHUNK
    q = q_ref[...]
    z = z_ref[...]
    v = v_ref[...]
    lb = lb_ref[...]
    w = q.shape[1]
    log_f = jnp.log(lb + (1.0 - lb) * _sigmoid(z))
    k = (1.0 - lb) * _sigmoid(-z)

    ri = lax.broadcasted_iota(I32, (t, t), 0)
    ci = lax.broadcasted_iota(I32, (t, t), 1)
    same = (ri // HG_CHUNK) == (ci // HG_CHUNK)
    causal = jnp.logical_and(same, (ci >= ri) if reverse else (ci <= ri))

    yield
    rc = lax.broadcasted_iota(I32, (t, w), 0) % HG_CHUNK
    b = log_f
    s = 1
    while s < HG_CHUNK:
        if reverse:
            b = b + jnp.where(rc < HG_CHUNK - s, pltpu.roll(b, t - s, 0), 0.0)
        else:
            b = b + jnp.where(rc >= s, pltpu.roll(b, s, 0), 0.0)
        s *= 2
    yield
    tot = jnp.sum(log_f.reshape(nch, HG_CHUNK, w), axis=1, keepdims=True)
    b_last = jnp.broadcast_to(tot, (nch, HG_CHUNK, w)).reshape(t, w)
    b_half = 0.5 * b_last
    q_i = (q * jnp.exp(b - b_half)).astype(BF16)
    k_i = (k * jnp.exp(b_half - b)).astype(BF16)
    q_s = (q * jnp.exp(b)).astype(BF16)
    k_e = (k * jnp.exp(b_last - b)).astype(BF16)
    v_bf = v.astype(BF16)
    lo = lax.broadcasted_iota(I32, (t, LANES), 1) < HEAD_DIM
    chunk_of_row = lax.broadcasted_iota(I32, (t, LANES), 0) // HG_CHUNK
    bd_r = lax.broadcasted_iota(I32, (LANES, LANES), 0) // HEAD_DIM
    bd_c = lax.broadcasted_iota(I32, (LANES, LANES), 1) // HEAD_DIM
    block_diag = bd_r == bd_c
    yield

    def head_pair(p):
        cols = slice(p * LANES, (p + 1) * LANES)
        qi_p, ki_p, qs_p, ke_p, v_p = q_i[:, cols], k_i[:, cols], q_s[:, cols], k_e[:, cols], v_bf[:, cols]
        o_half = []
        for e in range(2):
            qm = jnp.where(lo if e == 0 else jnp.logical_not(lo), qi_p, jnp.zeros_like(qi_p))
            att = lax.dot_general(qm, ki_p, NT_DIMS, preferred_element_type=F32)
            att = jnp.where(causal, att, 0.0).astype(BF16)
            o_half.append(jnp.dot(att, v_p, preferred_element_type=F32))
            yield
        v_t = v[:, cols].T.astype(BF16)
        st = st_ref[p]
        o_inter = [None] * nch
        order = range(nch - 1, -1, -1) if reverse else range(nch)
        for n in order:
            rows = slice(n * HG_CHUNK, (n + 1) * HG_CHUNK)
            o_inter[n] = lax.dot_general(qs_p[rows], st.astype(BF16), NT_DIMS,
                                         preferred_element_type=F32)
            ke_n = jnp.where(chunk_of_row == n, ke_p, jnp.zeros_like(ke_p))
            kv_t = jnp.dot(v_t, ke_n, preferred_element_type=F32)
            decay = jnp.exp(b_last[n * HG_CHUNK:n * HG_CHUNK + 1, cols])
            st = decay * st + jnp.where(block_diag, kv_t, 0.0)
            yield
        st_ref[p] = st
        out_ref[:, cols] = jnp.where(lo, o_half[0], o_half[1]) + jnp.concatenate(o_inter, axis=0)

    pairs = [head_pair(p) for p in range(w // LANES)]
    while pairs:
        for g in list(pairs):
            if next(g, "done") == "done":
                pairs.remove(g)
        yield


def _hgrn_body(qf_ref, zf_ref, vf_ref, qb_ref, zb_ref, vb_ref, lb_ref, of_ref, ob_ref, st_ref):
    c = pl.program_id(1)

    @pl.when(c == 0)
    def _init():
        st_ref[...] = jnp.zeros_like(st_ref)

    _alternate([_hgrn_dir(qf_ref, zf_ref, vf_ref, lb_ref.at[0], st_ref.at[0], of_ref, False),
                _hgrn_dir(qb_ref, zb_ref, vb_ref, lb_ref.at[1], st_ref.at[1], ob_ref, True)])


def hgrn_scan(hg, lower_bounds, *, n_batch, seq, ctx_len):
    r = hg.shape[0]
    w = hg.shape[1] // 5
    nl, fwd, bwd = _scan_maps(n_batch, seq, ctx_len)

    def col(m, j):
        return pl.BlockSpec((TSCAN, w), lambda b, s: (m(b, s), j))

    lb3 = lower_bounds.reshape(2, 1, w)
    return pl.pallas_call(
        _hgrn_body,
        grid=(n_batch, nl + 1),
        in_specs=[col(fwd, 0), col(fwd, 1), col(fwd, 3), col(bwd, 0), col(bwd, 2), col(bwd, 3),
                  pl.BlockSpec(lb3.shape, lambda b, s: (0, 0, 0))],
        out_specs=[col(fwd, 0), col(bwd, 0)],
        out_shape=[jax.ShapeDtypeStruct((r, w), F32)] * 2,
        scratch_shapes=[pltpu.VMEM((2, w // LANES, LANES, LANES), F32)],
        compiler_params=_cparams("parallel", "arbitrary"),
        name="hgrn_scan",
    )(hg, hg, hg, hg, hg, hg, lb3)


def _outproj_router_body(*refs, n_x, n_lat):
    (att_ref, hf_ref, hb_ref, lg_ref, of_ref, ob_ref, gg_ref, hn_ref,
     wo_ref, gm_ref, sh_ref, sc_ref, n2_ref, rw_ref, rb_ref,
     xo_ref, h2_ref, idx_ref, gate_ref, rank_ref, cnt_ref) = refs[n_x:]
    lru_y = (hf_ref[...] + hb_ref[...]) * jax.nn.gelu(lg_ref[...], approximate=True)
    o = of_ref[...] + ob_ref[...]
    w = o.shape[1]
    gr = lax.broadcasted_iota(I32, (w, w), 0) // HEAD_DIM
    gc = lax.broadcasted_iota(I32, (w, w), 1) // HEAD_DIM
    head_mean = jnp.where(gr == gc, 1.0 / HEAD_DIM, 0.0).astype(BF16)
    sq_hi, sq_lo = _split_bf16(o * o)
    ms = (jnp.dot(sq_hi, head_mean, preferred_element_type=F32)
          + jnp.dot(sq_lo, head_mean, preferred_element_type=F32))
    gg = gg_ref[...]
    hg_y = o * lax.rsqrt(ms + EPS) * hn_ref[...] * (gg * _sigmoid_abs(gg))
    y = jnp.dot(att_ref[...], wo_ref[0:512, :], preferred_element_type=F32)
    y = y + jnp.dot(lru_y.astype(BF16), wo_ref[512:768, :], preferred_element_type=F32)
    y = y + jnp.dot(hg_y.astype(BF16), wo_ref[768:1024, :], preferred_element_type=F32)
    x = _tile_rows(refs[:n_x], n_lat) + gm_ref[...] * y
    xo_ref[...] = x
    h2 = _rms(x) * n2_ref[...]
    h2 = h2 * (1.0 + sc_ref[...]) + sh_ref[...]
    h2_ref[...] = h2

    logits = lax.dot_general(rw_ref[...], h2, NT_DIMS, preferred_element_type=F32,
                             precision=lax.Precision.HIGHEST) + rb_ref[...]
    ne, tm = logits.shape
    e_io = lax.broadcasted_iota(I32, (ne, tm), 0).astype(F32)
    work = logits
    vals, hots = [], []
    for k in range(TOP_K):
        m = jnp.max(work, axis=0, keepdims=True)
        idx = jnp.min(jnp.where(work == m, e_io, float(ne)), axis=0, keepdims=True)
        hot = e_io == idx
        vals.append(m)
        hots.append(hot)
        idx_ref[k:k + 1, :] = idx.astype(I32)
        work = jnp.where(hot, -jnp.inf, work)
    exps = [jnp.exp(v - vals[0]) for v in vals]
    denom = exps[0] + exps[1] + exps[2] + exps[3]
    for k in range(TOP_K):
        gate_ref[k:k + 1, :] = exps[k] / denom
    z4 = jnp.zeros((8 - TOP_K, tm), F32)
    gate_ref[TOP_K:8, :] = z4
    idx_ref[TOP_K:8, :] = z4.astype(I32)
    rank_ref[TOP_K:8, :] = z4.astype(I32)

    chosen = jnp.logical_or(jnp.logical_or(hots[0], hots[1]), jnp.logical_or(hots[2], hots[3]))
    sr = lax.broadcasted_iota(I32, (tm, tm), 0)
    sc = lax.broadcasted_iota(I32, (tm, tm), 1)
    before = jnp.where(sr < sc, 1.0, 0.0).astype(BF16)
    chosen_f = jnp.where(chosen, 1.0, 0.0)
    prefix = jnp.dot(chosen_f.astype(BF16), before, preferred_element_type=F32)
    for k in range(TOP_K):
        rk = jnp.sum(jnp.where(hots[k], prefix, 0.0), axis=0, keepdims=True)
        rank_ref[k:k + 1, :] = rk.astype(I32)
    cnt_ref[...] = jnp.broadcast_to(jnp.sum(chosen_f, axis=1, keepdims=True), cnt_ref.shape)


def outproj_router(xs, att, hf, hb, lru, of, ob, hg, hn_g, wo_bf, gate_msa, shift_mlp, scale_mlp,
                   n2_g, rw_t, rb, *, tiles_per_batch, n_seg, with_ctx):
    d = xs[0].shape[1]
    n_lat = tiles_per_batch * (n_seg - 1)
    n_tiles = n_lat + (1 if with_ctx else 0)
    r = n_tiles * TM

    def seg(i):
        return (jnp.minimum(i // tiles_per_batch, n_seg - 1), 0, 0)

    row = lambda w, j=0: pl.BlockSpec((TM, w), lambda i: (i, j))
    full = lambda a: pl.BlockSpec(a.shape, lambda i: (0,) * a.ndim)
    modspec = pl.BlockSpec((None, 1, d), seg)
    lane_out = pl.BlockSpec((8, TM), lambda i: (0, i))
    return pl.pallas_call(
        functools.partial(_outproj_router_body, n_x=len(xs), n_lat=n_lat),
        grid=(n_tiles,),
        in_specs=_stream_specs(xs, n_lat, d) + [
                  row(512), row(256), row(256), row(256, 1), row(256), row(256),
                  row(256, 4), full(hn_g), full(wo_bf), modspec, modspec, modspec, full(n2_g),
                  full(rw_t), full(rb)],
        out_specs=[row(d), row(d), lane_out, lane_out, lane_out,
                   pl.BlockSpec((None, N_EXPERTS, LANES), lambda i: (i, 0, 0))],
        out_shape=[jax.ShapeDtypeStruct((r, d), F32), jax.ShapeDtypeStruct((r, d), F32),
                   jax.ShapeDtypeStruct((8, r), I32), jax.ShapeDtypeStruct((8, r), F32),
                   jax.ShapeDtypeStruct((8, r), I32),
                   jax.ShapeDtypeStruct((n_tiles, N_EXPERTS, LANES), F32)],
        compiler_params=_cparams("parallel"),
        name="outproj_router",
    )(*xs, att, hf, hb, lru, of, ob, hg, hn_g, wo_bf, gate_msa, shift_mlp, scale_mlp, n2_g, rw_t, rb)


def _local_pos(idx_ref, rank_ref, toff_ref, k):
    ne = toff_ref.shape[0]
    tm = idx_ref.shape[1]
    e_io = lax.broadcasted_iota(I32, (ne, tm), 0)
    off = jnp.sum(jnp.where(e_io == idx_ref[k:k + 1, :], toff_ref[...], 0.0), axis=0, keepdims=True)
    return off + rank_ref[k:k + 1, :].astype(F32)


def _run_copies(tab_ref, tile, make_copy, sems, slot, start):
    base = tile * LANES

    def per_expert(e, carry):
        src8 = tab_ref[base + e]
        dst8 = tab_ref[base + N_EXPERTS + e]
        n8 = tab_ref[base + 2 * N_EXPERTS + e]

        for b in range(RUN_BITS):
            size = 8 << b
            off8 = (n8 >> (b + 1)) << (b + 1)

            @pl.when(((n8 >> b) & 1) == 1)
            def _():
                cp = make_copy(pl.multiple_of((src8 + off8) * 8, 8),
                               pl.multiple_of((dst8 + off8) * 8, 8), size, sems.at[slot, b])
                if start:
                    cp.start()
                else:
                    cp.wait()
        return carry

    lax.fori_loop(0, N_EXPERTS, per_expert, 0)


def _fill_copies(fill_ref, zero_scr, xs_ref, sems, start):
    def go(cp):
        if start:
            cp.start()
        else:
            cp.wait()

    def per_expert(e, carry):
        t8 = fill_ref[e]
        n8 = fill_ref[N_EXPERTS + e]
        for b in range(TAIL_BITS):
            size = 8 << b
            off8 = (n8 >> (b + 1)) << (b + 1)

            @pl.when(((n8 >> b) & 1) == 1)
            def _():
                go(pltpu.make_async_copy(
                    zero_scr.at[pl.ds(0, size), :],
                    xs_ref.at[pl.ds(pl.multiple_of((t8 + off8) * 8, 8), size), :], sems.at[0, b]))
        return carry

    lax.fori_loop(0, N_EXPERTS, per_expert, 0)

    def per_block(j, carry):
        go(pltpu.make_async_copy(zero_scr, xs_ref.at[pl.ds(pl.multiple_of(j * TB, TB), TB), :],
                                 sems.at[0, TAIL_BITS]))
        return carry

    lax.fori_loop(fill_ref[2 * N_EXPERTS], xs_ref.shape[0] // TB, per_block, 0)


def _dispatch_body(fill_ref, tab_ref, h_ref, idx_ref, rank_ref, toff_ref, xs_ref, z_scr, zero_scr, sems):
    tm = h_ref.shape[0]
    i = pl.program_id(0)
    slot = i % 2

    @pl.when(i == 0)
    def _fill():
        zero_scr[...] = jnp.zeros_like(zero_scr)
        _fill_copies(fill_ref, zero_scr, xs_ref, sems, True)
        _fill_copies(fill_ref, zero_scr, xs_ref, sems, False)

    r_io = lax.broadcasted_iota(I32, (ZR, tm), 0).astype(F32)
    hit = r_io == _local_pos(idx_ref, rank_ref, toff_ref, 0)
    for k in range(1, TOP_K):
        hit = jnp.logical_or(hit, r_io == _local_pos(idx_ref, rank_ref, toff_ref, k))
    perm = jnp.where(hit, 1.0, 0.0).astype(BF16)
    z_scr[slot] = jnp.dot(perm, h_ref[...].astype(BF16), preferred_element_type=F32)

    def copy_from(buf):
        def make_copy(s, d, size, sem):
            return pltpu.make_async_copy(z_scr.at[buf, pl.ds(s, size), :],
                                         xs_ref.at[pl.ds(d, size), :], sem)
        return make_copy

    _run_copies(tab_ref, i, copy_from(slot), sems, slot, True)

    @pl.when(i > 0)
    def _drain_previous():
        _run_copies(tab_ref, i - 1, copy_from(1 - slot), sems, 1 - slot, False)

    @pl.when(i == pl.num_programs(0) - 1)
    def _drain_last():
        _run_copies(tab_ref, i, copy_from(slot), sems, slot, False)


def moe_dispatch(h2, idx, rank, toff_col, tab, fill, p_rows):
    r, d = h2.shape
    n_tiles = r // TM
    lane_in = pl.BlockSpec((8, TM), lambda i, f, t: (0, i))
    return pl.pallas_call(
        _dispatch_body,
        grid_spec=pltpu.PrefetchScalarGridSpec(
            num_scalar_prefetch=2,
            grid=(n_tiles,),
            in_specs=[pl.BlockSpec((TM, d), lambda i, f, t: (i, 0)),
                      lane_in, lane_in,
                      pl.BlockSpec((None, N_EXPERTS, 1), lambda i, f, t: (i, 0, 0))],
            out_specs=pl.BlockSpec(memory_space=pl.ANY),
            scratch_shapes=[pltpu.VMEM((2, ZR, d), F32), pltpu.VMEM((TB, d), F32),
                            pltpu.SemaphoreType.DMA((2, N_COPY_SEMS))]),
        out_shape=jax.ShapeDtypeStruct((p_rows, d), F32),
        compiler_params=_cparams("arbitrary"),
        name="moe_dispatch",
    )(fill, tab, h2, idx, rank, toff_col)


def _expert_body(be_ref, rows_ref, next_ref, x_ref, bgu_ref, bd_ref, wgu_hbm, wd_hbm, y_ref,
                 wgu_stage, wd_stage, wgu_bf, wd_bf, sems, *, e0):
    i = pl.program_id(0)
    n_rows = rows_ref[i]

    def weight_copies(e):
        return (pltpu.make_async_copy(wgu_hbm.at[e], wgu_stage, sems.at[0]),
                pltpu.make_async_copy(wd_hbm.at[e], wd_stage, sems.at[1]))

    @pl.when(i == 0)
    def _first_fetch():
        for cp in weight_copies(be_ref[0] + e0):
            cp.start()

    @pl.when(jnp.logical_and(n_rows > 0,
                             jnp.logical_or(i == 0, be_ref[i] != be_ref[jnp.maximum(i - 1, 0)])))
    def _new_expert():
        for cp in weight_copies(be_ref[i] + e0):
            cp.wait()
        rows = 128

        def chunk(j, carry):
            sl = pl.ds(pl.multiple_of(j * rows, rows), rows)
            wgu_bf[sl, :] = wgu_stage[sl, :].astype(BF16)
            wd_bf[sl, :] = wd_stage[sl, :].astype(BF16)
            return carry

        lax.fori_loop(0, wgu_stage.shape[0] // rows, chunk, 0)

        @pl.when(next_ref[i] >= 0)
        def _prefetch():
            for cp in weight_copies(next_ref[i] + e0):
                cp.start()

    def mlp(rows):
        dff = wd_bf.shape[0]
        gu = jnp.dot(x_ref[0:rows, :].astype(BF16), wgu_bf[...], preferred_element_type=F32) + bgu_ref[...]
        gate = jnp.minimum(gu[:, :dff].astype(BF16), SWIGLU_LIMIT)
        up = jnp.clip(gu[:, dff:].astype(BF16), -SWIGLU_LIMIT, SWIGLU_LIMIT)
        act = (up + 1.0) * (gate * _sigmoid_abs(SWIGLU_ALPHA * gate))
        y_ref[0:rows, :] = jnp.dot(act, wd_bf[...], preferred_element_type=F32) + bd_ref[...]

    for rows in range(TB_STEP, TB + 1, TB_STEP):
        @pl.when(jnp.logical_and(n_rows > rows - TB_STEP, n_rows <= rows))
        def _piece(rows=rows):
            mlp(rows)
            if rows < TB:
                y_ref[rows:TB, :] = jnp.zeros((TB - rows, y_ref.shape[1]), F32)

    @pl.when(n_rows == 0)
    def _unused():
        y_ref[...] = jnp.zeros_like(y_ref)


def moe_experts(xs_sorted, blk_e, blk_rows, blk_next, wgu, bgu, wd, bd, layer):
    p_rows, d = xs_sorted.shape
    depth, ne, _, dgu = wgu.shape
    dff = wd.shape[2]
    assert dff == d

    def expert(i, be, nr, nx):
        return (layer * ne + be[i], 0, 0)

    return pl.pallas_call(
        functools.partial(_expert_body, e0=layer * ne),
        grid_spec=pltpu.PrefetchScalarGridSpec(
            num_scalar_prefetch=3,
            grid=(p_rows // TB,),
            in_specs=[pl.BlockSpec((TB, d), lambda i, be, nr, nx: (i, 0)),
                      pl.BlockSpec((None, 1, dgu), expert),
                      pl.BlockSpec((None, 1, d), expert),
                      pl.BlockSpec(memory_space=pl.ANY),
                      pl.BlockSpec(memory_space=pl.ANY)],
            out_specs=pl.BlockSpec((TB, d), lambda i, be, nr, nx: (i, 0)),
            scratch_shapes=[pltpu.VMEM((d, dgu), F32), pltpu.VMEM((dff, d), F32),
                            pltpu.VMEM((d, dgu), BF16), pltpu.VMEM((dff, d), BF16),
                            pltpu.SemaphoreType.DMA((2,))]),
        out_shape=jax.ShapeDtypeStruct((p_rows, d), F32),
        compiler_params=_cparams("arbitrary"),
        name="moe_experts",
    )(blk_e, blk_rows, blk_next, xs_sorted, bgu.reshape(depth * ne, 1, dgu), bd.reshape(depth * ne, 1, d),
      wgu.reshape(depth * ne, d, dgu), wd.reshape(depth * ne, dff, d))


def _combine_body(tab_ref, x_ref, idx_ref, rank_ref, gate_ref, toff_ref, gm_ref, fg_ref, ys_ref,
                  o_ref, zy_scr, sems, *, final):
    tm = x_ref.shape[0]
    i = pl.program_id(0)
    slot = i % 2

    def copy_into(buf):
        def make_copy(s, d, size, sem):
            return pltpu.make_async_copy(ys_ref.at[pl.ds(d, size), :],
                                         zy_scr.at[buf, pl.ds(s, size), :], sem)
        return make_copy

    def fetch(tile, buf):
        zy_scr[buf, TOP_K * tm:ZR, :] = jnp.zeros((ZR - TOP_K * tm, zy_scr.shape[2]), F32)
        _run_copies(tab_ref, tile, copy_into(buf), sems, buf, True)

    @pl.when(i == 0)
    def _first():
        fetch(0, 0)

    @pl.when(i + 1 < pl.num_programs(0))
    def _prefetch_next():
        fetch(i + 1, 1 - slot)

    pos = [_local_pos(idx_ref, rank_ref, toff_ref, k) for k in range(TOP_K)]
    packed = jnp.concatenate(pos + [gate_ref[0:TOP_K, :], jnp.zeros((LANES - 2 * TOP_K, tm), F32)],
                             axis=0)
    cols = packed.T
    c_io = lax.broadcasted_iota(I32, (tm, ZR), 1).astype(F32)
    weights = jnp.zeros((tm, ZR), F32)
    for k in range(TOP_K):
        weights = jnp.where(c_io == cols[:, k:k + 1], cols[:, TOP_K + k:TOP_K + k + 1], weights)
    _run_copies(tab_ref, i, copy_into(slot), sems, slot, False)
    acc = jnp.dot(weights.astype(BF16), zy_scr[slot].astype(BF16), preferred_element_type=F32)
    x = x_ref[...] + gm_ref[...] * acc
    if final:
        x = _rms(x) * fg_ref[...]
    o_ref[...] = x


def moe_combine(xs, idx, rank, gates, toff_col, tab, ys, gate_mlp, final_g, *, tiles_per_batch,
                n_seg, n_tiles, final):
    d = xs.shape[1]

    def seg(i):
        return (jnp.minimum(i // tiles_per_batch, n_seg - 1), 0, 0)

    lane_in = pl.BlockSpec((8, TM), lambda i, t: (0, i))
    return pl.pallas_call(
        functools.partial(_combine_body, final=final),
        grid_spec=pltpu.PrefetchScalarGridSpec(
            num_scalar_prefetch=1,
            grid=(n_tiles,),
            in_specs=[pl.BlockSpec((TM, d), lambda i, t: (i, 0)),
                      lane_in, lane_in, lane_in,
                      pl.BlockSpec((None, N_EXPERTS, 1), lambda i, t: (i, 0, 0)),
                      pl.BlockSpec((None, 1, d), lambda i, t: seg(i)),
                      pl.BlockSpec((1, d), lambda i, t: (0, 0)),
                      pl.BlockSpec(memory_space=pl.ANY)],
            out_specs=pl.BlockSpec((TM, d), lambda i, t: (i, 0)),
            scratch_shapes=[pltpu.VMEM((2, ZR, d), F32), pltpu.SemaphoreType.DMA((2, N_COPY_SEMS))]),
        out_shape=jax.ShapeDtypeStruct((n_tiles * TM, d), F32),
        compiler_params=_cparams("arbitrary"),
        name="moe_combine",
    )(tab, xs, idx, rank, gates, toff_col, gate_mlp, final_g, ys)


def _rope_tables(seq):
    nf = HEAD_DIM // 4
    pos = jnp.arange(seq)
    rows = (pos // GRID_W).astype(F32)
    cols = (pos % GRID_W).astype(F32)
    inv_freq = ROPE_BASE ** (-jnp.arange(nf, dtype=F32) / nf)
    d = jnp.arange(LANES) % HEAD_DIM
    axis = d // (2 * nf)
    half = (d // nf) % 2
    f = d % nf
    ang = jnp.where(axis[None, :] == 0, rows[:, None], cols[:, None]) * inv_freq[f][None, :]
    cs, sn = jnp.cos(ang), jnp.sin(ang)
    ca = jnp.where(half[None, :] == 0, -sn, 0.0)
    cb = jnp.where(half[None, :] == 1, sn, 0.0)
    pad1 = jnp.ones((TM, LANES), F32)
    pad0 = jnp.zeros((TM, LANES), F32)
    return (jnp.concatenate([cs, pad1]), jnp.concatenate([ca, pad0]), jnp.concatenate([cb, pad0]))


def _block_diag(w):
    n, c, _ = w.shape
    eye = jnp.eye(n, dtype=w.dtype)
    return (eye[:, None, :, None] * w[:, :, None, :]).reshape(n * c, n * c)


def _moe_layout(tile_counts, n_blocks):
    cnt8 = (tile_counts.astype(I32) + 7) // 8 * 8
    toff = jnp.cumsum(cnt8, axis=1) - cnt8
    goff = jnp.cumsum(cnt8, axis=0) - cnt8
    padded = (jnp.sum(cnt8, axis=0) + TB - 1) // TB * TB
    pends = jnp.cumsum(padded)
    dst = (pends - padded)[None, :] + goff
    n_tiles = cnt8.shape[0]
    tab = jnp.concatenate([toff, dst, cnt8, jnp.zeros((n_tiles, LANES - 3 * N_EXPERTS), I32)], axis=1) // 8
    blk_start = jnp.arange(n_blocks, dtype=I32) * TB
    blk_e = jnp.minimum(jnp.sum((pends[None, :] <= blk_start[:, None]).astype(I32), axis=1), N_EXPERTS - 1)
    n_used = (pends[-1:] // TB).astype(I32)
    total = jnp.sum(cnt8, axis=0)
    region_end = pends - padded + total
    of_blk = blk_e[:, None] == jnp.arange(N_EXPERTS, dtype=I32)[None, :]
    blk_rows = jnp.clip(jnp.sum(jnp.where(of_blk, region_end[None, :], 0), axis=1) - blk_start, 0, TB)
    e_ids = jnp.arange(N_EXPERTS, dtype=I32)
    later = (e_ids[None, :] > e_ids[:, None]) & (padded[None, :] > 0)
    next_e = jnp.min(jnp.where(later, e_ids[None, :], N_EXPERTS), axis=1)
    next_e = jnp.where(next_e == N_EXPERTS, -1, next_e)
    blk_next = jnp.sum(jnp.where(of_blk, next_e[None, :], 0), axis=1)
    fill = jnp.concatenate([(pends - padded + total) // 8, (padded - total) // 8, n_used,
                            jnp.zeros((LANES - 2 * N_EXPERTS - 1,), I32)]).astype(I32)
    return (tab.astype(I32).reshape(n_tiles * LANES), toff.astype(F32).reshape(n_tiles, N_EXPERTS, 1),
            fill, blk_e.astype(I32), blk_rows.astype(I32), blk_next.astype(I32))


def kernel(x, c, ctx, c_ctx, ada_w, ada_b, norm1_g, w_in, attn_sink, conv_w, conv_b, lru_wr, lru_br,
           lru_wi, lru_bi, lru_lambda, hgrn_lb_logits, hgrn_norm_g, w_out, norm2_g, router_w,
           router_b, moe_w_gu, moe_b_gu, moe_w_down, moe_b_down, final_g):
    n_batch, seq, d = x.shape
    ctx_len = ctx.shape[1]
    depth = ada_w.shape[0]
    assert n_batch * ctx_len == TM and seq % TM == 0 and ctx_len == TSCAN
    tiles_per_batch = seq // TM
    n_seg = n_batch + 1
    n_lat = n_batch * tiles_per_batch
    r = n_batch * seq + n_batch * ctx_len
    n_blocks = -(-(r * TOP_K + (r // TM) * N_EXPERTS * 7) // TB) + N_EXPERTS
    p_rows = n_blocks * TB

    xs = (x.reshape(n_batch * seq, d), ctx.reshape(n_batch * ctx_len, d))
    cond8 = jnp.zeros((8, d), F32).at[:n_batch].set(c).at[n_batch].set(c_ctx)
    mods = adaln(cond8, ada_w, ada_b)
    rope_c, rope_a, rope_b = _rope_tables(seq)
    lb_p = jax.nn.softmax(hgrn_lb_logits.astype(F32), axis=0)
    lower_bounds = jnp.cumsum(lb_p, axis=0) - lb_p[0]

    out = None
    for layer in range(depth):
        mod = [mods[layer, :, j * d:(j + 1) * d].reshape(8, 1, d) for j in range(6)]
        q, kv, lru, hg = inproj(xs, norm1_g[layer].reshape(1, d), mod[0], mod[1],
                                w_in[layer].astype(BF16), rope_c, rope_a, rope_b,
                                tiles_per_batch=tiles_per_batch, n_seg=n_seg)
        final = layer == depth - 1
        att = attention(q, kv, attn_sink[layer], tiles_per_batch=tiles_per_batch,
                        n_batch=n_batch, ctx_len=ctx_len, with_ctx=not final)
        w_gates = jnp.stack([jnp.concatenate([_block_diag(lru_wr[layer, dd]),
                                              _block_diag(lru_wi[layer, dd])], axis=1)
                             for dd in range(2)]).astype(BF16)
        cw = lru_br.shape[-1]
        hf, hb = lru_scan(lru, conv_w[layer], conv_b[layer].reshape(1, cw), w_gates,
                          lru_br[layer].reshape(2, 1, cw), lru_bi[layer].reshape(2, 1, cw),
                          lru_lambda[layer].reshape(2, 1, cw),
                          n_batch=n_batch, seq=seq, ctx_len=ctx_len)
        of, ob = hgrn_scan(hg, lower_bounds[layer], n_batch=n_batch, seq=seq, ctx_len=ctx_len)
        x_mid, h2, idx, gates, rank, counts = outproj_router(
            xs, att, hf, hb, lru, of, ob, hg, hgrn_norm_g[layer].reshape(1, -1),
            w_out[layer].astype(BF16), mod[2], mod[3], mod[4], norm2_g[layer].reshape(1, d),
            router_w[layer].T, router_b[layer].reshape(N_EXPERTS, 1),
            tiles_per_batch=tiles_per_batch, n_seg=n_seg, with_ctx=not final)
        tab, toff_col, fill, blk_e, blk_rows, blk_next = _moe_layout(counts[:, :, 0], n_blocks)
        xs_sorted = moe_dispatch(h2, idx, rank, toff_col, tab, fill, p_rows)
        ys = moe_experts(xs_sorted, blk_e, blk_rows, blk_next, moe_w_gu, moe_b_gu, moe_w_down, moe_b_down, layer)
        res = moe_combine(x_mid, idx, rank, gates, toff_col, tab, ys, mod[5], final_g.reshape(1, d),
                          tiles_per_batch=tiles_per_batch, n_seg=n_seg,
                          n_tiles=n_lat if final else n_lat + 1, final=final)
        if final:
            out = res.reshape(n_batch, seq, d)
        else:
            xs = (res,)
    return out
```

```python
import functools

import jax
import jax.numpy as jnp
from jax import lax
from jax.experimental import pallas as pl
from jax.experimental.pallas import tpu as pltpu

F32 = jnp.float32
BF16 = jnp.bfloat16
I32 = jnp.int32

EPS = 1e-6
NEG_INF = -1e30
GRID_W = 64
ROPE_BASE = 10000.0
HEAD_DIM = 64
N_HEADS = 8
WINDOW = 128
LRU_C = 8.0
SQRT_FLOOR = 1e-12
HG_CHUNK = 32
N_EXPERTS = 32
TOP_K = 4
SWIGLU_LIMIT = 7.0
SWIGLU_ALPHA = 1.702

TM = 512
TSCAN = 256
TB = 1024
TB_STEP = 256
RUN_BITS = 7
TAIL_BITS = 7
N_COPY_SEMS = max(RUN_BITS, TAIL_BITS + 1)
ZR = TOP_K * 512 + N_EXPERTS * 8
LANES = 128
VMEM_LIMIT = 56 * 1024 * 1024

NT_DIMS = (((1,), (1,)), ((), ()))


def _cparams(*sem):
    return pltpu.CompilerParams(dimension_semantics=sem, vmem_limit_bytes=VMEM_LIMIT)


def _rms(x):
    return x * lax.rsqrt(jnp.mean(x * x, axis=-1, keepdims=True) + EPS)


def _sigmoid(x):
    return 1.0 / (1.0 + jnp.exp(-x))


def _sigmoid_abs(x):
    return 0.5 * jnp.tanh(0.5 * x) + 0.5


def _alternate(stages):
    stages = list(stages)
    while stages:
        for g in list(stages):
            if next(g, "done") == "done":
                stages.remove(g)


def _split_bf16(x):
    hi = x.astype(BF16)
    return hi, (x - hi.astype(F32)).astype(BF16)


def _adaln_body(c_ref, w_ref, b_ref, o_ref):
    c = c_ref[...]
    cond = c * _sigmoid(c)
    o_ref[...] = jnp.dot(cond, w_ref[...], preferred_element_type=F32,
                         precision=lax.Precision.HIGHEST) + b_ref[...]


def adaln(cond8, ada_w, ada_b):
    depth, d, n = ada_w.shape
    tn = 1536
    return pl.pallas_call(
        _adaln_body,
        grid=(depth, n // tn),
        in_specs=[pl.BlockSpec((8, d), lambda l, j: (0, 0)),
                  pl.BlockSpec((None, d, tn), lambda l, j: (l, 0, j)),
                  pl.BlockSpec((None, 1, tn), lambda l, j: (l, 0, j))],
        out_specs=pl.BlockSpec((None, 8, tn), lambda l, j: (l, 0, j)),
        out_shape=jax.ShapeDtypeStruct((depth, 8, n), F32),
        compiler_params=_cparams("parallel", "parallel"),
        name="adaln",
    )(cond8, ada_w, ada_b.reshape(depth, 1, n))


def _tile_rows(refs, n_lat):
    if len(refs) == 1:
        return refs[0][...]
    return jnp.where(pl.program_id(0) < n_lat, refs[0][...], refs[1][...])


def _inproj_body(*refs, n_x, n_lat):
    (g_ref, sh_ref, sc_ref, w_ref, cs_ref, sa_ref, sb_ref, q_ref, kv_ref, lru_ref, hg_ref) = refs[n_x:]
    h = _rms(_tile_rows(refs[:n_x], n_lat)) * g_ref[...]
    h = h * (1.0 + sc_ref[...]) + sh_ref[...]
    p = jnp.dot(h.astype(BF16), w_ref[...], preferred_element_type=F32)
    cs, sa, sb = cs_ref[...], sa_ref[...], sb_ref[...]

    def rope(t):
        return t * cs + pltpu.roll(t, LANES - 16, 1) * sa + pltpu.roll(t, 16, 1) * sb

    for j in range(4):
        q_ref[:, j * LANES:(j + 1) * LANES] = (
            rope(p[:, j * LANES:(j + 1) * LANES]) * (HEAD_DIM ** -0.5)).astype(BF16)
    k = rope(p[:, 512:640])
    v = p[:, 640:768]
    lo = lax.broadcasted_iota(I32, k.shape, 1) < HEAD_DIM
    kr = pltpu.roll(k, HEAD_DIM, 1)
    vr = pltpu.roll(v, HEAD_DIM, 1)
    kv_ref[:, 0:128] = jnp.where(lo, k, kr).astype(BF16)
    kv_ref[:, 128:256] = jnp.where(lo, kr, k).astype(BF16)
    kv_ref[:, 256:384] = jnp.where(lo, v, vr).astype(BF16)
    kv_ref[:, 384:512] = jnp.where(lo, vr, v).astype(BF16)
    lru_ref[...] = p[:, 768:1280]
    hg_ref[...] = p[:, 1280:2560]


def _stream_specs(xs, n_lat, d):
    if len(xs) == 1:
        return [pl.BlockSpec((TM, d), lambda i: (i, 0))]
    return [pl.BlockSpec((TM, d), lambda i: (jnp.minimum(i, n_lat - 1), 0)),
            pl.BlockSpec((TM, d), lambda i: (0, 0))]


def inproj(xs, g, shift, scale, w_bf, rope_c, rope_a, rope_b, *, tiles_per_batch, n_seg):
    d = xs[0].shape[1]
    n_lat = tiles_per_batch * (n_seg - 1)
    n_tiles = n_lat + 1
    r = n_tiles * TM

    def seg(i):
        return (jnp.minimum(i // tiles_per_batch, n_seg - 1), 0, 0)

    def rope_idx(i):
        return (jnp.where(i < n_lat, i % tiles_per_batch, tiles_per_batch), 0)

    row = lambda w: pl.BlockSpec((TM, w), lambda i: (i, 0))
    return pl.pallas_call(
        functools.partial(_inproj_body, n_x=len(xs), n_lat=n_lat),
        grid=(n_tiles,),
        in_specs=_stream_specs(xs, n_lat, d) + [
                  pl.BlockSpec((1, d), lambda i: (0, 0)),
                  pl.BlockSpec((None, 1, d), seg),
                  pl.BlockSpec((None, 1, d), seg),
                  pl.BlockSpec(w_bf.shape, lambda i: (0, 0)),
                  pl.BlockSpec((TM, LANES), rope_idx),
                  pl.BlockSpec((TM, LANES), rope_idx),
                  pl.BlockSpec((TM, LANES), rope_idx)],
        out_specs=[row(512), row(512), row(512), row(1280)],
        out_shape=[jax.ShapeDtypeStruct((r, 512), BF16),
                   jax.ShapeDtypeStruct((r, 512), BF16),
                   jax.ShapeDtypeStruct((r, 512), F32),
                   jax.ShapeDtypeStruct((r, 1280), F32)],
        compiler_params=_cparams("parallel"),
        name="inproj",
    )(*xs, g, shift, scale, w_bf, rope_c, rope_a, rope_b)


def _softmax_pv(s_parts, v_parts, sink_col):
    def lane_blocks(a):
        return [a[:, j:j + LANES] for j in range(0, a.shape[1], LANES)]

    folded = None
    for s in s_parts:
        for blk in lane_blocks(s):
            folded = blk if folded is None else jnp.maximum(folded, blk)
    m = jnp.maximum(sink_col, jnp.max(folded, axis=-1, keepdims=True))
    psum = None
    o = None
    for s, v in zip(s_parts, v_parts):
        p = jnp.exp(s - m)
        for blk in lane_blocks(p):
            psum = blk if psum is None else psum + blk
        pv = jnp.dot(p.astype(BF16), v, preferred_element_type=F32)
        o = pv if o is None else o + pv
    l = jnp.exp(sink_col - m) + jnp.sum(psum, axis=-1, keepdims=True)
    return o / l


def _attn_body(sink_ref, q_ref, kvm_ref, kvp_ref, kvn_ref, kvc_ref, o_ref, *,
               tiles_per_batch, n_lat, ctx_len, n_batch):
    i = pl.program_id(0)
    blk = WINDOW
    nsub = TM // blk

    def stacked_q(q_sub, g):
        rows = q_sub.shape[0]
        lo = lax.broadcasted_iota(I32, (rows, LANES), 1) < HEAD_DIM
        parts = []
        for jq in (2 * g, 2 * g + 1):
            qg = q_sub[:, jq * LANES:(jq + 1) * LANES]
            parts.append(jnp.where(lo, qg, jnp.zeros_like(qg)))
            parts.append(jnp.where(lo, jnp.zeros_like(qg), qg))
        return jnp.concatenate(parts, axis=0)

    def sink_column(g, rows):
        blk_id = lax.broadcasted_iota(I32, (4 * rows, 1), 0) // rows
        col = jnp.full((4 * rows, 1), sink_ref[4 * g + 3], F32)
        for b in range(3):
            col = jnp.where(blk_id == b, sink_ref[4 * g + b], col)
        return col

    def write_heads(o, g, rows, row0):
        lo = lax.broadcasted_iota(I32, (rows, LANES), 1) < HEAD_DIM
        for t, jq in enumerate((2 * g, 2 * g + 1)):
            o_e0 = o[(2 * t) * rows:(2 * t + 1) * rows]
            o_e1 = o[(2 * t + 1) * rows:(2 * t + 2) * rows]
            o_ref[row0:row0 + rows, jq * LANES:(jq + 1) * LANES] = (
                jnp.where(lo, o_e0, o_e1).astype(BF16))

    @pl.when(i < n_lat)
    def _latent():
        j = i % tiles_per_batch
        r_io = lax.broadcasted_iota(I32, (4 * blk, blk), 0) % blk
        c_io = lax.broadcasted_iota(I32, (4 * blk, blk), 1)
        def kv_block(b, col):
            if b < 0:
                return kvp_ref[:, col:col + LANES]
            if b >= nsub:
                return kvn_ref[:, col:col + LANES]
            return kvm_ref[b * blk:(b + 1) * blk, col:col + LANES]

        def unit(sb, g):
            kc, vc = g * LANES, (2 + g) * LANES
            q_sub = q_ref[sb * blk:(sb + 1) * blk, :]
            has_prev = jnp.logical_or(j > 0, sb > 0)
            has_next = jnp.logical_or(j < tiles_per_batch - 1, sb < nsub - 1)
            qs = stacked_q(q_sub, g)
            s_prev = lax.dot_general(qs, kv_block(sb - 1, kc), NT_DIMS, preferred_element_type=F32)
            s_own = lax.dot_general(qs, kv_block(sb, kc), NT_DIMS, preferred_element_type=F32)
            s_next = lax.dot_general(qs, kv_block(sb + 1, kc), NT_DIMS, preferred_element_type=F32)
            s_ctx = lax.dot_general(qs, kvc_ref[:, kc:kc + LANES], NT_DIMS, preferred_element_type=F32)
            yield
            s_prev = jnp.where(jnp.logical_and(c_io >= r_io, has_prev), s_prev, NEG_INF)
            s_next = jnp.where(jnp.logical_and(c_io <= r_io, has_next), s_next, NEG_INF)
            o = _softmax_pv(
                [s_prev, s_own, s_next, s_ctx],
                [kv_block(sb - 1, vc), kv_block(sb, vc), kv_block(sb + 1, vc),
                 kvc_ref[:, vc:vc + LANES]],
                sink_column(g, blk))
            yield
            write_heads(o, g, blk, sb * blk)

        for sb in range(0, nsub, 2):
            _alternate([unit(sb, 0), unit(sb, 1), unit(sb + 1, 0), unit(sb + 1, 1)])

    @pl.when(i >= n_lat)
    def _context():
        for bb in range(n_batch):
            r0 = bb * ctx_len
            q_sub = q_ref[r0:r0 + ctx_len, :]
            for g in range(2):
                kc, vc = g * LANES, (2 + g) * LANES
                qs = stacked_q(q_sub, g)
                s = lax.dot_general(qs, kvm_ref[r0:r0 + ctx_len, kc:kc + LANES], NT_DIMS,
                                    preferred_element_type=F32)
                o = _softmax_pv([s], [kvm_ref[r0:r0 + ctx_len, vc:vc + LANES]],
                                sink_column(g, ctx_len))
                write_heads(o, g, ctx_len, r0)


def attention(q, kv, sink, *, tiles_per_batch, n_batch, ctx_len, with_ctx):
    r = q.shape[0]
    n_lat = tiles_per_batch * n_batch
    assert n_batch * ctx_len == TM and r // TM == n_lat + 1
    n_tiles = n_lat + (1 if with_ctx else 0)
    sub = TM // WINDOW
    n_blk128 = r // WINDOW
    ctx_blk0 = (n_lat * TM) // ctx_len

    def prev_idx(i, s):
        return (jnp.maximum(i * sub - 1, 0), 0)

    def next_idx(i, s):
        return (jnp.minimum(i * sub + sub, n_blk128 - 1), 0)

    def ctx_idx(i, s):
        return (ctx_blk0 + jnp.minimum(i // tiles_per_batch, n_batch - 1), 0)

    body = functools.partial(_attn_body, tiles_per_batch=tiles_per_batch, n_lat=n_lat,
                             ctx_len=ctx_len, n_batch=n_batch)
    return pl.pallas_call(
        body,
        grid_spec=pltpu.PrefetchScalarGridSpec(
            num_scalar_prefetch=1,
            grid=(n_tiles,),
            in_specs=[pl.BlockSpec((TM, 512), lambda i, s: (i, 0)),
                      pl.BlockSpec((TM, 512), lambda i, s: (i, 0)),
                      pl.BlockSpec((WINDOW, 512), prev_idx),
                      pl.BlockSpec((WINDOW, 512), next_idx),
                      pl.BlockSpec((ctx_len, 512), ctx_idx)],
            out_specs=pl.BlockSpec((TM, 512), lambda i, s: (i, 0))),
        out_shape=jax.ShapeDtypeStruct((n_tiles * TM, 512), BF16),
        compiler_params=_cparams("parallel"),
        name="attention",
    )(sink, q, kv, kv, kv, kv)


def _scan_maps(n_batch, seq, ctx_len):
    assert ctx_len == TSCAN
    nl = seq // TSCAN
    ctx0 = (n_batch * seq) // TSCAN

    def fwd(b, c):
        return jnp.where(c == 0, ctx0 + b, b * nl + c - 1)

    def bwd(b, c):
        return jnp.where(c == 0, ctx0 + b, b * nl + nl - c)

    return nl, fwd, bwd


def _lru_dir(x_ref, xp_ref, xn_ref, has_prev, has_next, cw_ref, cb_ref, w_ref, br_ref, bi_ref,
             lam_ref, a_scr, b_scr, reverse):
    t = TSCAN
    xp = jnp.where(has_prev, xp_ref[...], 0.0)
    xn = jnp.where(has_next, xn_ref[...], 0.0)
    xpad = jnp.concatenate([xp, x_ref[...], xn], axis=0)
    n = t + 16
    cw = cw_ref[...]
    u = cb_ref[...] + jnp.zeros((t, xpad.shape[1]), F32)
    for j in range(4):
        sh = (2 - j) % n
        rolled = xpad if sh == 0 else pltpu.roll(xpad, sh, 0)
        u = u + rolled[8:8 + t] * cw[j:j + 1, :]
    gates = jnp.dot(u.astype(BF16), w_ref[...], preferred_element_type=F32)
    yield
    c = u.shape[1]
    r = _sigmoid_abs(gates[:, :c] + br_ref[...])
    ig = _sigmoid_abs(gates[:, c:] + bi_ref[...])
    nl = -lam_ref[...]
    softplus = jnp.maximum(nl, 0.0) + jnp.log1p(jnp.exp(-jnp.abs(nl)))
    log_a = (-LRU_C) * r * softplus
    a = jnp.exp(log_a)
    mult = jnp.sqrt(jnp.maximum(1.0 - a * a, SQRT_FLOOR))
    bv = mult * (ig * u)
    yield

    g8 = t // 8
    a3 = a.reshape(g8, 8, c)
    b3 = bv.reshape(g8, 8, c)
    r8 = lax.broadcasted_iota(I32, (g8, 8, c), 1)
    for s in (1, 2, 4):
        if reverse:
            a_sh, b_sh, ok = pltpu.roll(a3, 8 - s, 1), pltpu.roll(b3, 8 - s, 1), r8 < 8 - s
        else:
            a_sh, b_sh, ok = pltpu.roll(a3, s, 1), pltpu.roll(b3, s, 1), r8 >= s
        b3 = jnp.where(ok, a3 * b_sh + b3, b3)
        a3 = jnp.where(ok, a3 * a_sh, a3)
    a_scr[...] = a3.reshape(t, c)
    b_scr[...] = b3.reshape(t, c)


def _lru_body(xf_ref, xfp_ref, xfn_ref, xb_ref, xbp_ref, xbn_ref, cw_ref, cb_ref, w_ref, br_ref,
              bi_ref, lam_ref, hf_ref, hb_ref, st_ref, a_scr, b_scr, *, nl):
    c = pl.program_id(1)

    @pl.when(c == 0)
    def _init():
        st_ref[...] = jnp.zeros_like(st_ref)

    lat = c > 0
    _alternate([
        _lru_dir(xf_ref, xfp_ref, xfn_ref, jnp.logical_and(lat, c > 1), jnp.logical_and(lat, c < nl),
                 cw_ref, cb_ref, w_ref.at[0], br_ref.at[0], bi_ref.at[0], lam_ref.at[0],
                 a_scr.at[0], b_scr.at[0], False),
        _lru_dir(xb_ref, xbp_ref, xbn_ref, jnp.logical_and(lat, c < nl), jnp.logical_and(lat, c > 1),
                 cw_ref, cb_ref, w_ref.at[1], br_ref.at[1], bi_ref.at[1], lam_ref.at[1],
                 a_scr.at[1], b_scr.at[1], True)])

    g8 = TSCAN // 8

    def group(gi, carry):
        cf, cb = carry
        rf = pl.ds(pl.multiple_of(gi * 8, 8), 8)
        rb = pl.ds(pl.multiple_of((g8 - 1 - gi) * 8, 8), 8)
        hf = b_scr[0, rf, :] + a_scr[0, rf, :] * cf
        hb = b_scr[1, rb, :] + a_scr[1, rb, :] * cb
        hf_ref[rf, :] = hf
        hb_ref[rb, :] = hb
        return (jnp.broadcast_to(hf[7:8, :], hf.shape), jnp.broadcast_to(hb[0:1, :], hb.shape))

    cf, cb = lax.fori_loop(0, g8, group, (st_ref[0], st_ref[1]), unroll=4)
    st_ref[0] = cf
    st_ref[1] = cb


def lru_scan(lru, conv_w, conv_b, w_gates, br, bi, lam, *, n_batch, seq, ctx_len):
    r = lru.shape[0]
    c = lru.shape[1] // 2
    nl, fwd, bwd = _scan_maps(n_batch, seq, ctx_len)
    per = TSCAN // 8
    n8 = r // 8

    def main(m):
        return pl.BlockSpec((TSCAN, c), lambda b, s: (m(b, s), 0))

    def prev(m):
        return pl.BlockSpec((8, c), lambda b, s: (jnp.maximum(m(b, s) * per - 1, 0), 0))

    def nxt(m):
        return pl.BlockSpec((8, c), lambda b, s: (jnp.minimum(m(b, s) * per + per, n8 - 1), 0))

    full = lambda a: pl.BlockSpec(a.shape, lambda b, s: (0,) * a.ndim)
    return pl.pallas_call(
        functools.partial(_lru_body, nl=nl),
        grid=(n_batch, nl + 1),
        in_specs=[main(fwd), prev(fwd), nxt(fwd), main(bwd), prev(bwd), nxt(bwd),
                  full(conv_w), full(conv_b), full(w_gates), full(br), full(bi), full(lam)],
        out_specs=[main(fwd), main(bwd)],
        out_shape=[jax.ShapeDtypeStruct((r, c), F32)] * 2,
        scratch_shapes=[pltpu.VMEM((2, 8, c), F32), pltpu.VMEM((2, TSCAN, c), F32),
                        pltpu.VMEM((2, TSCAN, c), F32)],
        compiler_params=_cparams("parallel", "arbitrary"),
        name="lru_scan",
    )(lru, lru, lru, lru, lru, lru, conv_w, conv_b, w_gates, br, bi, lam)


def _hgrn_dir(q_ref, z_ref, v_ref, lb_ref, st_ref, out_ref, reverse):
    t = TSCAN
    nch = t // HG_CHUNK
    q = q_ref[...]
    z = z_ref[...]
    v = v_ref[...]
    lb = lb_ref[...]
    w = q.shape[1]
    log_f = jnp.log(lb + (1.0 - lb) * _sigmoid(z))
    k = (1.0 - lb) * _sigmoid(-z)

    ri = lax.broadcasted_iota(I32, (t, t), 0)
    ci = lax.broadcasted_iota(I32, (t, t), 1)
    same = (ri // HG_CHUNK) == (ci // HG_CHUNK)
    causal = jnp.logical_and(same, (ci >= ri) if reverse else (ci <= ri))

    yield
    rc = lax.broadcasted_iota(I32, (t, w), 0) % HG_CHUNK
    b = log_f
    s = 1
    while s < HG_CHUNK:
        if reverse:
            b = b + jnp.where(rc < HG_CHUNK - s, pltpu.roll(b, t - s, 0), 0.0)
        else:
            b = b + jnp.where(rc >= s, pltpu.roll(b, s, 0), 0.0)
        s *= 2
    yield
    tot = jnp.sum(log_f.reshape(nch, HG_CHUNK, w), axis=1, keepdims=True)
    b_last = jnp.broadcast_to(tot, (nch, HG_CHUNK, w)).reshape(t, w)
    b_half = 0.5 * b_last
    q_i = (q * jnp.exp(b - b_half)).astype(BF16)
    k_i = (k * jnp.exp(b_half - b)).astype(BF16)
    q_s = (q * jnp.exp(b)).astype(BF16)
    k_e = (k * jnp.exp(b_last - b)).astype(BF16)
    v_bf = v.astype(BF16)
    lo = lax.broadcasted_iota(I32, (t, LANES), 1) < HEAD_DIM
    chunk_of_row = lax.broadcasted_iota(I32, (t, LANES), 0) // HG_CHUNK
    bd_r = lax.broadcasted_iota(I32, (LANES, LANES), 0) // HEAD_DIM
    bd_c = lax.broadcasted_iota(I32, (LANES, LANES), 1) // HEAD_DIM
    block_diag = bd_r == bd_c
    yield

    def head_pair(p):
        cols = slice(p * LANES, (p + 1) * LANES)
        qi_p, ki_p, qs_p, ke_p, v_p = q_i[:, cols], k_i[:, cols], q_s[:, cols], k_e[:, cols], v_bf[:, cols]
        o_half = []
        for e in range(2):
            qm = jnp.where(lo if e == 0 else jnp.logical_not(lo), qi_p, jnp.zeros_like(qi_p))
            att = lax.dot_general(qm, ki_p, NT_DIMS, preferred_element_type=F32)
            att = jnp.where(causal, att, 0.0).astype(BF16)
            o_half.append(jnp.dot(att, v_p, preferred_element_type=F32))
            yield
        v_t = v[:, cols].T.astype(BF16)
        st = st_ref[p]
        o_inter = [None] * nch
        order = range(nch - 1, -1, -1) if reverse else range(nch)
        for n in order:
            rows = slice(n * HG_CHUNK, (n + 1) * HG_CHUNK)
            o_inter[n] = lax.dot_general(qs_p[rows], st.astype(BF16), NT_DIMS,
                                         preferred_element_type=F32)
            ke_n = jnp.where(chunk_of_row == n, ke_p, jnp.zeros_like(ke_p))
            kv_t = jnp.dot(v_t, ke_n, preferred_element_type=F32)
            decay = jnp.exp(b_last[n * HG_CHUNK:n * HG_CHUNK + 1, cols])
            st = decay * st + jnp.where(block_diag, kv_t, 0.0)
            yield
        st_ref[p] = st
        out_ref[:, cols] = jnp.where(lo, o_half[0], o_half[1]) + jnp.concatenate(o_inter, axis=0)

    pairs = [head_pair(p) for p in range(w // LANES)]
    while pairs:
        for g in list(pairs):
            if next(g, "done") == "done":
                pairs.remove(g)
        yield


def _hgrn_body(qf_ref, zf_ref, vf_ref, qb_ref, zb_ref, vb_ref, lb_ref, of_ref, ob_ref, st_ref):
    c = pl.program_id(1)

    @pl.when(c == 0)
    def _init():
        st_ref[...] = jnp.zeros_like(st_ref)

    _alternate([_hgrn_dir(qf_ref, zf_ref, vf_ref, lb_ref.at[0], st_ref.at[0], of_ref, False),
                _hgrn_dir(qb_ref, zb_ref, vb_ref, lb_ref.at[1], st_ref.at[1], ob_ref, True)])


def hgrn_scan(hg, lower_bounds, *, n_batch, seq, ctx_len):
    r = hg.shape[0]
    w = hg.shape[1] // 5
    nl, fwd, bwd = _scan_maps(n_batch, seq, ctx_len)

    def col(m, j):
        return pl.BlockSpec((TSCAN, w), lambda b, s: (m(b, s), j))

    lb3 = lower_bounds.reshape(2, 1, w)
    return pl.pallas_call(
        _hgrn_body,
        grid=(n_batch, nl + 1),
        in_specs=[col(fwd, 0), col(fwd, 1), col(fwd, 3), col(bwd, 0), col(bwd, 2), col(bwd, 3),
                  pl.BlockSpec(lb3.shape, lambda b, s: (0, 0, 0))],
        out_specs=[col(fwd, 0), col(bwd, 0)],
        out_shape=[jax.ShapeDtypeStruct((r, w), F32)] * 2,
        scratch_shapes=[pltpu.VMEM((2, w // LANES, LANES, LANES), F32)],
        compiler_params=_cparams("parallel", "arbitrary"),
        name="hgrn_scan",
    )(hg, hg, hg, hg, hg, hg, lb3)


def _outproj_router_body(*refs, n_x, n_lat):
    (att_ref, hf_ref, hb_ref, lg_ref, of_ref, ob_ref, gg_ref, hn_ref,
     wo_ref, gm_ref, sh_ref, sc_ref, n2_ref, rw_ref, rb_ref,
     xo_ref, h2_ref, idx_ref, gate_ref, rank_ref, cnt_ref) = refs[n_x:]
    tm = att_ref.shape[0]
    hm = tm // 2
    w = of_ref.shape[1]
    gr = lax.broadcasted_iota(I32, (w, w), 0) // HEAD_DIM
    gc = lax.broadcasted_iota(I32, (w, w), 1) // HEAD_DIM
    head_mean = jnp.where(gr == gc, 1.0 / HEAD_DIM, 0.0).astype(BF16)
    x_tile = _tile_rows(refs[:n_x], n_lat)
    ne = rw_ref.shape[0]
    e_io = lax.broadcasted_iota(I32, (ne, hm), 0).astype(F32)
    routed = [None, None]

    def half(hh):
        rows = slice(hh * hm, (hh + 1) * hm)
        lru_y = (hf_ref[rows, :] + hb_ref[rows, :]) * jax.nn.gelu(lg_ref[rows, :], approximate=True)
        o = of_ref[rows, :] + ob_ref[rows, :]
        sq_hi, sq_lo = _split_bf16(o * o)
        ms = (jnp.dot(sq_hi, head_mean, preferred_element_type=F32)
              + jnp.dot(sq_lo, head_mean, preferred_element_type=F32))
        gg = gg_ref[rows, :]
        hg_y = o * lax.rsqrt(ms + EPS) * hn_ref[...] * (gg * _sigmoid_abs(gg))
        yield
        y = jnp.dot(att_ref[rows, :], wo_ref[0:512, :], preferred_element_type=F32)
        y = y + jnp.dot(lru_y.astype(BF16), wo_ref[512:768, :], preferred_element_type=F32)
        y = y + jnp.dot(hg_y.astype(BF16), wo_ref[768:1024, :], preferred_element_type=F32)
        x = x_tile[rows] + gm_ref[...] * y
        xo_ref[rows, :] = x
        h2 = _rms(x) * n2_ref[...]
        h2 = h2 * (1.0 + sc_ref[...]) + sh_ref[...]
        h2_ref[rows, :] = h2
        yield
        logits = lax.dot_general(rw_ref[...], h2, NT_DIMS, preferred_element_type=F32,
                                 precision=lax.Precision.HIGHEST) + rb_ref[...]
        work = logits
        vals, hots = [], []
        for k in range(TOP_K):
            m = jnp.max(work, axis=0, keepdims=True)
            idx = jnp.min(jnp.where(work == m, e_io, float(ne)), axis=0, keepdims=True)
            hot = e_io == idx
            vals.append(m)
            hots.append(hot)
            idx_ref[k:k + 1, rows] = idx.astype(I32)
            work = jnp.where(hot, -jnp.inf, work)
        exps = [jnp.exp(v - vals[0]) for v in vals]
        denom = exps[0] + exps[1] + exps[2] + exps[3]
        for k in range(TOP_K):
            gate_ref[k:k + 1, rows] = exps[k] / denom
        chosen = jnp.logical_or(jnp.logical_or(hots[0], hots[1]), jnp.logical_or(hots[2], hots[3]))
        routed[hh] = (hots, jnp.where(chosen, 1.0, 0.0))

    _alternate([half(0), half(1)])
    z4 = jnp.zeros((8 - TOP_K, tm), F32)
    gate_ref[TOP_K:8, :] = z4
    idx_ref[TOP_K:8, :] = z4.astype(I32)
    rank_ref[TOP_K:8, :] = z4.astype(I32)

    sr = lax.broadcasted_iota(I32, (hm, hm), 0)
    sc = lax.broadcasted_iota(I32, (hm, hm), 1)
    before = jnp.where(sr < sc, 1.0, 0.0).astype(BF16)
    seen = jnp.zeros((ne, 1), F32)
    for hh in range(2):
        hots, chosen_f = routed[hh]
        prefix = jnp.dot(chosen_f.astype(BF16), before, preferred_element_type=F32) + seen
        for k in range(TOP_K):
            rk = jnp.sum(jnp.where(hots[k], prefix, 0.0), axis=0, keepdims=True)
            rank_ref[k:k + 1, hh * hm:(hh + 1) * hm] = rk.astype(I32)
        seen = seen + jnp.sum(chosen_f, axis=1, keepdims=True)
    cnt_ref[...] = jnp.broadcast_to(seen, cnt_ref.shape)


def outproj_router(xs, att, hf, hb, lru, of, ob, hg, hn_g, wo_bf, gate_msa, shift_mlp, scale_mlp,
                   n2_g, rw_t, rb, *, tiles_per_batch, n_seg, with_ctx):
    d = xs[0].shape[1]
    n_lat = tiles_per_batch * (n_seg - 1)
    n_tiles = n_lat + (1 if with_ctx else 0)
    r = n_tiles * TM

    def seg(i):
        return (jnp.minimum(i // tiles_per_batch, n_seg - 1), 0, 0)

    row = lambda w, j=0: pl.BlockSpec((TM, w), lambda i: (i, j))
    full = lambda a: pl.BlockSpec(a.shape, lambda i: (0,) * a.ndim)
    modspec = pl.BlockSpec((None, 1, d), seg)
    lane_out = pl.BlockSpec((8, TM), lambda i: (0, i))
    return pl.pallas_call(
        functools.partial(_outproj_router_body, n_x=len(xs), n_lat=n_lat),
        grid=(n_tiles,),
        in_specs=_stream_specs(xs, n_lat, d) + [
                  row(512), row(256), row(256), row(256, 1), row(256), row(256),
                  row(256, 4), full(hn_g), full(wo_bf), modspec, modspec, modspec, full(n2_g),
                  full(rw_t), full(rb)],
        out_specs=[row(d), row(d), lane_out, lane_out, lane_out,
                   pl.BlockSpec((None, N_EXPERTS, LANES), lambda i: (i, 0, 0))],
        out_shape=[jax.ShapeDtypeStruct((r, d), F32), jax.ShapeDtypeStruct((r, d), F32),
                   jax.ShapeDtypeStruct((8, r), I32), jax.ShapeDtypeStruct((8, r), F32),
                   jax.ShapeDtypeStruct((8, r), I32),
                   jax.ShapeDtypeStruct((n_tiles, N_EXPERTS, LANES), F32)],
        compiler_params=_cparams("parallel"),
        name="outproj_router",
    )(*xs, att, hf, hb, lru, of, ob, hg, hn_g, wo_bf, gate_msa, shift_mlp, scale_mlp, n2_g, rw_t, rb)


def _local_pos(idx_ref, rank_ref, toff_ref, k):
    ne = toff_ref.shape[0]
    tm = idx_ref.shape[1]
    e_io = lax.broadcasted_iota(I32, (ne, tm), 0)
    off = jnp.sum(jnp.where(e_io == idx_ref[k:k + 1, :], toff_ref[...], 0.0), axis=0, keepdims=True)
    return off + rank_ref[k:k + 1, :].astype(F32)


def _run_copies(tab_ref, tile, make_copy, sems, slot, start):
    base = tile * LANES

    def per_expert(e, carry):
        src8 = tab_ref[base + e]
        dst8 = tab_ref[base + N_EXPERTS + e]
        n8 = tab_ref[base + 2 * N_EXPERTS + e]

        for b in range(RUN_BITS):
            size = 8 << b
            off8 = (n8 >> (b + 1)) << (b + 1)

            @pl.when(((n8 >> b) & 1) == 1)
            def _():
                cp = make_copy(pl.multiple_of((src8 + off8) * 8, 8),
                               pl.multiple_of((dst8 + off8) * 8, 8), size, sems.at[slot, b])
                if start:
                    cp.start()
                else:
                    cp.wait()
        return carry

    lax.fori_loop(0, N_EXPERTS, per_expert, 0)


def _fill_copies(fill_ref, zero_scr, xs_ref, sems, start):
    def go(cp):
        if start:
            cp.start()
        else:
            cp.wait()

    def per_expert(e, carry):
        t8 = fill_ref[e]
        n8 = fill_ref[N_EXPERTS + e]
        for b in range(TAIL_BITS):
            size = 8 << b
            off8 = (n8 >> (b + 1)) << (b + 1)

            @pl.when(((n8 >> b) & 1) == 1)
            def _():
                go(pltpu.make_async_copy(
                    zero_scr.at[pl.ds(0, size), :],
                    xs_ref.at[pl.ds(pl.multiple_of((t8 + off8) * 8, 8), size), :], sems.at[0, b]))
        return carry

    lax.fori_loop(0, N_EXPERTS, per_expert, 0)

    def per_block(j, carry):
        go(pltpu.make_async_copy(zero_scr, xs_ref.at[pl.ds(pl.multiple_of(j * TB, TB), TB), :],
                                 sems.at[0, TAIL_BITS]))
        return carry

    lax.fori_loop(fill_ref[2 * N_EXPERTS], xs_ref.shape[0] // TB, per_block, 0)


def _dispatch_body(fill_ref, tab_ref, h_ref, idx_ref, rank_ref, toff_ref, xs_ref, z_scr, zero_scr, sems):
    tm = h_ref.shape[0]
    i = pl.program_id(0)
    slot = i % 2

    @pl.when(i == 0)
    def _fill():
        zero_scr[...] = jnp.zeros_like(zero_scr)
        _fill_copies(fill_ref, zero_scr, xs_ref, sems, True)
        _fill_copies(fill_ref, zero_scr, xs_ref, sems, False)

    r_io = lax.broadcasted_iota(I32, (ZR, tm), 0).astype(F32)
    hit = r_io == _local_pos(idx_ref, rank_ref, toff_ref, 0)
    for k in range(1, TOP_K):
        hit = jnp.logical_or(hit, r_io == _local_pos(idx_ref, rank_ref, toff_ref, k))
    perm = jnp.where(hit, 1.0, 0.0).astype(BF16)
    z_scr[slot] = jnp.dot(perm, h_ref[...].astype(BF16), preferred_element_type=F32)

    def copy_from(buf):
        def make_copy(s, d, size, sem):
            return pltpu.make_async_copy(z_scr.at[buf, pl.ds(s, size), :],
                                         xs_ref.at[pl.ds(d, size), :], sem)
        return make_copy

    _run_copies(tab_ref, i, copy_from(slot), sems, slot, True)

    @pl.when(i > 0)
    def _drain_previous():
        _run_copies(tab_ref, i - 1, copy_from(1 - slot), sems, 1 - slot, False)

    @pl.when(i == pl.num_programs(0) - 1)
    def _drain_last():
        _run_copies(tab_ref, i, copy_from(slot), sems, slot, False)


def moe_dispatch(h2, idx, rank, toff_col, tab, fill, p_rows):
    r, d = h2.shape
    n_tiles = r // TM
    lane_in = pl.BlockSpec((8, TM), lambda i, f, t: (0, i))
    return pl.pallas_call(
        _dispatch_body,
        grid_spec=pltpu.PrefetchScalarGridSpec(
            num_scalar_prefetch=2,
            grid=(n_tiles,),
            in_specs=[pl.BlockSpec((TM, d), lambda i, f, t: (i, 0)),
                      lane_in, lane_in,
                      pl.BlockSpec((None, N_EXPERTS, 1), lambda i, f, t: (i, 0, 0))],
            out_specs=pl.BlockSpec(memory_space=pl.ANY),
            scratch_shapes=[pltpu.VMEM((2, ZR, d), F32), pltpu.VMEM((TB, d), F32),
                            pltpu.SemaphoreType.DMA((2, N_COPY_SEMS))]),
        out_shape=jax.ShapeDtypeStruct((p_rows, d), F32),
        compiler_params=_cparams("arbitrary"),
        name="moe_dispatch",
    )(fill, tab, h2, idx, rank, toff_col)


def _expert_body(be_ref, rows_ref, next_ref, x_ref, bgu_ref, bd_ref, wgu_hbm, wd_hbm, y_ref,
                 wgu_stage, wd_stage, wgu_bf, wd_bf, sems, *, e0):
    i = pl.program_id(0)
    n_rows = rows_ref[i]

    def weight_copies(e):
        return (pltpu.make_async_copy(wgu_hbm.at[e], wgu_stage, sems.at[0]),
                pltpu.make_async_copy(wd_hbm.at[e], wd_stage, sems.at[1]))

    @pl.when(i == 0)
    def _first_fetch():
        for cp in weight_copies(be_ref[0] + e0):
            cp.start()

    @pl.when(jnp.logical_and(n_rows > 0,
                             jnp.logical_or(i == 0, be_ref[i] != be_ref[jnp.maximum(i - 1, 0)])))
    def _new_expert():
        for cp in weight_copies(be_ref[i] + e0):
            cp.wait()
        rows = 128

        def chunk(j, carry):
            sl = pl.ds(pl.multiple_of(j * rows, rows), rows)
            wgu_bf[sl, :] = wgu_stage[sl, :].astype(BF16)
            wd_bf[sl, :] = wd_stage[sl, :].astype(BF16)
            return carry

        lax.fori_loop(0, wgu_stage.shape[0] // rows, chunk, 0)

        @pl.when(next_ref[i] >= 0)
        def _prefetch():
            for cp in weight_copies(next_ref[i] + e0):
                cp.start()

    def mlp(rows):
        dff = wd_bf.shape[0]
        gu = jnp.dot(x_ref[0:rows, :].astype(BF16), wgu_bf[...], preferred_element_type=F32) + bgu_ref[...]
        gate = jnp.minimum(gu[:, :dff].astype(BF16), SWIGLU_LIMIT)
        up = jnp.clip(gu[:, dff:].astype(BF16), -SWIGLU_LIMIT, SWIGLU_LIMIT)
        act = (up + 1.0) * (gate * _sigmoid_abs(SWIGLU_ALPHA * gate))
        y_ref[0:rows, :] = jnp.dot(act, wd_bf[...], preferred_element_type=F32) + bd_ref[...]

    for rows in range(TB_STEP, TB + 1, TB_STEP):
        @pl.when(jnp.logical_and(n_rows > rows - TB_STEP, n_rows <= rows))
        def _piece(rows=rows):
            mlp(rows)
            if rows < TB:
                y_ref[rows:TB, :] = jnp.zeros((TB - rows, y_ref.shape[1]), F32)

    @pl.when(n_rows == 0)
    def _unused():
        y_ref[...] = jnp.zeros_like(y_ref)


def moe_experts(xs_sorted, blk_e, blk_rows, blk_next, wgu, bgu, wd, bd, layer):
    p_rows, d = xs_sorted.shape
    depth, ne, _, dgu = wgu.shape
    dff = wd.shape[2]
    assert dff == d

    def expert(i, be, nr, nx):
        return (layer * ne + be[i], 0, 0)

    return pl.pallas_call(
        functools.partial(_expert_body, e0=layer * ne),
        grid_spec=pltpu.PrefetchScalarGridSpec(
            num_scalar_prefetch=3,
            grid=(p_rows // TB,),
            in_specs=[pl.BlockSpec((TB, d), lambda i, be, nr, nx: (i, 0)),
                      pl.BlockSpec((None, 1, dgu), expert),
                      pl.BlockSpec((None, 1, d), expert),
                      pl.BlockSpec(memory_space=pl.ANY),
                      pl.BlockSpec(memory_space=pl.ANY)],
            out_specs=pl.BlockSpec((TB, d), lambda i, be, nr, nx: (i, 0)),
            scratch_shapes=[pltpu.VMEM((d, dgu), F32), pltpu.VMEM((dff, d), F32),
                            pltpu.VMEM((d, dgu), BF16), pltpu.VMEM((dff, d), BF16),
                            pltpu.SemaphoreType.DMA((2,))]),
        out_shape=jax.ShapeDtypeStruct((p_rows, d), F32),
        compiler_params=_cparams("arbitrary"),
        name="moe_experts",
    )(blk_e, blk_rows, blk_next, xs_sorted, bgu.reshape(depth * ne, 1, dgu), bd.reshape(depth * ne, 1, d),
      wgu.reshape(depth * ne, d, dgu), wd.reshape(depth * ne, dff, d))


def _combine_body(tab_ref, x_ref, idx_ref, rank_ref, gate_ref, toff_ref, gm_ref, fg_ref, ys_ref,
                  o_ref, zy_scr, sems, *, final):
    tm = x_ref.shape[0]
    i = pl.program_id(0)
    slot = i % 2

    def copy_into(buf):
        def make_copy(s, d, size, sem):
            return pltpu.make_async_copy(ys_ref.at[pl.ds(d, size), :],
                                         zy_scr.at[buf, pl.ds(s, size), :], sem)
        return make_copy

    def fetch(tile, buf):
        zy_scr[buf, TOP_K * tm:ZR, :] = jnp.zeros((ZR - TOP_K * tm, zy_scr.shape[2]), F32)
        _run_copies(tab_ref, tile, copy_into(buf), sems, buf, True)

    @pl.when(i == 0)
    def _first():
        fetch(0, 0)

    @pl.when(i + 1 < pl.num_programs(0))
    def _prefetch_next():
        fetch(i + 1, 1 - slot)

    pos = [_local_pos(idx_ref, rank_ref, toff_ref, k) for k in range(TOP_K)]
    packed = jnp.concatenate(pos + [gate_ref[0:TOP_K, :], jnp.zeros((LANES - 2 * TOP_K, tm), F32)],
                             axis=0)
    cols = packed.T
    c_io = lax.broadcasted_iota(I32, (tm, ZR), 1).astype(F32)
    weights = jnp.zeros((tm, ZR), F32)
    for k in range(TOP_K):
        weights = jnp.where(c_io == cols[:, k:k + 1], cols[:, TOP_K + k:TOP_K + k + 1], weights)
    _run_copies(tab_ref, i, copy_into(slot), sems, slot, False)
    acc = jnp.dot(weights.astype(BF16), zy_scr[slot].astype(BF16), preferred_element_type=F32)
    x = x_ref[...] + gm_ref[...] * acc
    if final:
        x = _rms(x) * fg_ref[...]
    o_ref[...] = x


def moe_combine(xs, idx, rank, gates, toff_col, tab, ys, gate_mlp, final_g, *, tiles_per_batch,
                n_seg, n_tiles, final):
    d = xs.shape[1]

    def seg(i):
        return (jnp.minimum(i // tiles_per_batch, n_seg - 1), 0, 0)

    lane_in = pl.BlockSpec((8, TM), lambda i, t: (0, i))
    return pl.pallas_call(
        functools.partial(_combine_body, final=final),
        grid_spec=pltpu.PrefetchScalarGridSpec(
            num_scalar_prefetch=1,
            grid=(n_tiles,),
            in_specs=[pl.BlockSpec((TM, d), lambda i, t: (i, 0)),
                      lane_in, lane_in, lane_in,
                      pl.BlockSpec((None, N_EXPERTS, 1), lambda i, t: (i, 0, 0)),
                      pl.BlockSpec((None, 1, d), lambda i, t: seg(i)),
                      pl.BlockSpec((1, d), lambda i, t: (0, 0)),
                      pl.BlockSpec(memory_space=pl.ANY)],
            out_specs=pl.BlockSpec((TM, d), lambda i, t: (i, 0)),
            scratch_shapes=[pltpu.VMEM((2, ZR, d), F32), pltpu.SemaphoreType.DMA((2, N_COPY_SEMS))]),
        out_shape=jax.ShapeDtypeStruct((n_tiles * TM, d), F32),
        compiler_params=_cparams("arbitrary"),
        name="moe_combine",
    )(tab, xs, idx, rank, gates, toff_col, gate_mlp, final_g, ys)


def _rope_tables(seq):
    nf = HEAD_DIM // 4
    pos = jnp.arange(seq)
    rows = (pos // GRID_W).astype(F32)
    cols = (pos % GRID_W).astype(F32)
    inv_freq = ROPE_BASE ** (-jnp.arange(nf, dtype=F32) / nf)
    d = jnp.arange(LANES) % HEAD_DIM
    axis = d // (2 * nf)
    half = (d // nf) % 2
    f = d % nf
    ang = jnp.where(axis[None, :] == 0, rows[:, None], cols[:, None]) * inv_freq[f][None, :]
    cs, sn = jnp.cos(ang), jnp.sin(ang)
    ca = jnp.where(half[None, :] == 0, -sn, 0.0)
    cb = jnp.where(half[None, :] == 1, sn, 0.0)
    pad1 = jnp.ones((TM, LANES), F32)
    pad0 = jnp.zeros((TM, LANES), F32)
    return (jnp.concatenate([cs, pad1]), jnp.concatenate([ca, pad0]), jnp.concatenate([cb, pad0]))


def _block_diag(w):
    n, c, _ = w.shape
    eye = jnp.eye(n, dtype=w.dtype)
    return (eye[:, None, :, None] * w[:, :, None, :]).reshape(n * c, n * c)


def _moe_layout(tile_counts, n_blocks):
    cnt8 = (tile_counts.astype(I32) + 7) // 8 * 8
    toff = jnp.cumsum(cnt8, axis=1) - cnt8
    goff = jnp.cumsum(cnt8, axis=0) - cnt8
    padded = (jnp.sum(cnt8, axis=0) + TB - 1) // TB * TB
    pends = jnp.cumsum(padded)
    dst = (pends - padded)[None, :] + goff
    n_tiles = cnt8.shape[0]
    tab = jnp.concatenate([toff, dst, cnt8, jnp.zeros((n_tiles, LANES - 3 * N_EXPERTS), I32)], axis=1) // 8
    blk_start = jnp.arange(n_blocks, dtype=I32) * TB
    blk_e = jnp.minimum(jnp.sum((pends[None, :] <= blk_start[:, None]).astype(I32), axis=1), N_EXPERTS - 1)
    n_used = (pends[-1:] // TB).astype(I32)
    total = jnp.sum(cnt8, axis=0)
    region_end = pends - padded + total
    of_blk = blk_e[:, None] == jnp.arange(N_EXPERTS, dtype=I32)[None, :]
    blk_rows = jnp.clip(jnp.sum(jnp.where(of_blk, region_end[None, :], 0), axis=1) - blk_start, 0, TB)
    e_ids = jnp.arange(N_EXPERTS, dtype=I32)
    later = (e_ids[None, :] > e_ids[:, None]) & (padded[None, :] > 0)
    next_e = jnp.min(jnp.where(later, e_ids[None, :], N_EXPERTS), axis=1)
    next_e = jnp.where(next_e == N_EXPERTS, -1, next_e)
    blk_next = jnp.sum(jnp.where(of_blk, next_e[None, :], 0), axis=1)
    fill = jnp.concatenate([(pends - padded + total) // 8, (padded - total) // 8, n_used,
                            jnp.zeros((LANES - 2 * N_EXPERTS - 1,), I32)]).astype(I32)
    return (tab.astype(I32).reshape(n_tiles * LANES), toff.astype(F32).reshape(n_tiles, N_EXPERTS, 1),
            fill, blk_e.astype(I32), blk_rows.astype(I32), blk_next.astype(I32))


def kernel(x, c, ctx, c_ctx, ada_w, ada_b, norm1_g, w_in, attn_sink, conv_w, conv_b, lru_wr, lru_br,
           lru_wi, lru_bi, lru_lambda, hgrn_lb_logits, hgrn_norm_g, w_out, norm2_g, router_w,
           router_b, moe_w_gu, moe_b_gu, moe_w_down, moe_b_down, final_g):
    n_batch, seq, d = x.shape
    ctx_len = ctx.shape[1]
    depth = ada_w.shape[0]
    assert n_batch * ctx_len == TM and seq % TM == 0 and ctx_len == TSCAN
    tiles_per_batch = seq // TM
    n_seg = n_batch + 1
    n_lat = n_batch * tiles_per_batch
    r = n_batch * seq + n_batch * ctx_len
    n_blocks = -(-(r * TOP_K + (r // TM) * N_EXPERTS * 7) // TB) + N_EXPERTS
    p_rows = n_blocks * TB

    xs = (x.reshape(n_batch * seq, d), ctx.reshape(n_batch * ctx_len, d))
    cond8 = jnp.zeros((8, d), F32).at[:n_batch].set(c).at[n_batch].set(c_ctx)
    mods = adaln(cond8, ada_w, ada_b)
    rope_c, rope_a, rope_b = _rope_tables(seq)
    lb_p = jax.nn.softmax(hgrn_lb_logits.astype(F32), axis=0)
    lower_bounds = jnp.cumsum(lb_p, axis=0) - lb_p[0]

    out = None
    for layer in range(depth):
        mod = [mods[layer, :, j * d:(j + 1) * d].reshape(8, 1, d) for j in range(6)]
        q, kv, lru, hg = inproj(xs, norm1_g[layer].reshape(1, d), mod[0], mod[1],
                                w_in[layer].astype(BF16), rope_c, rope_a, rope_b,
                                tiles_per_batch=tiles_per_batch, n_seg=n_seg)
        final = layer == depth - 1
        att = attention(q, kv, attn_sink[layer], tiles_per_batch=tiles_per_batch,
                        n_batch=n_batch, ctx_len=ctx_len, with_ctx=not final)
        w_gates = jnp.stack([jnp.concatenate([_block_diag(lru_wr[layer, dd]),
                                              _block_diag(lru_wi[layer, dd])], axis=1)
                             for dd in range(2)]).astype(BF16)
        cw = lru_br.shape[-1]
        hf, hb = lru_scan(lru, conv_w[layer], conv_b[layer].reshape(1, cw), w_gates,
                          lru_br[layer].reshape(2, 1, cw), lru_bi[layer].reshape(2, 1, cw),
                          lru_lambda[layer].reshape(2, 1, cw),
                          n_batch=n_batch, seq=seq, ctx_len=ctx_len)
        of, ob = hgrn_scan(hg, lower_bounds[layer], n_batch=n_batch, seq=seq, ctx_len=ctx_len)
        x_mid, h2, idx, gates, rank, counts = outproj_router(
            xs, att, hf, hb, lru, of, ob, hg, hgrn_norm_g[layer].reshape(1, -1),
            w_out[layer].astype(BF16), mod[2], mod[3], mod[4], norm2_g[layer].reshape(1, d),
            router_w[layer].T, router_b[layer].reshape(N_EXPERTS, 1),
            tiles_per_batch=tiles_per_batch, n_seg=n_seg, with_ctx=not final)
        tab, toff_col, fill, blk_e, blk_rows, blk_next = _moe_layout(counts[:, :, 0], n_blocks)
        xs_sorted = moe_dispatch(h2, idx, rank, toff_col, tab, fill, p_rows)
        ys = moe_experts(xs_sorted, blk_e, blk_rows, blk_next, moe_w_gu, moe_b_gu, moe_w_down, moe_b_down, layer)
        res = moe_combine(x_mid, idx, rank, gates, toff_col, tab, ys, mod[5], final_g.reshape(1, d),
                          tiles_per_batch=tiles_per_batch, n_seg=n_seg,
                          n_tiles=n_lat if final else n_lat + 1, final=final)
        if final:
            out = res.reshape(n_batch, seq, d)
        else:
            xs = (res,)
    return out
```

```python
import functools

import jax
import jax.numpy as jnp
from jax import lax
from jax.experimental import pallas as pl
from jax.experimental.pallas import tpu as pltpu

F32 = jnp.float32
BF16 = jnp.bfloat16
I32 = jnp.int32

EPS = 1e-6
NEG_INF = -1e30
GRID_W = 64
ROPE_BASE = 10000.0
HEAD_DIM = 64
N_HEADS = 8
WINDOW = 128
LRU_C = 8.0
SQRT_FLOOR = 1e-12
HG_CHUNK = 32
N_EXPERTS = 32
TOP_K = 4
SWIGLU_LIMIT = 7.0
SWIGLU_ALPHA = 1.702

TM = 512
TSCAN = 256
TB = 1024
TB_STEP = 256
RUN_BITS = 7
TAIL_BITS = 7
N_COPY_SEMS = max(RUN_BITS, TAIL_BITS + 1)
ZR = TOP_K * 512 + N_EXPERTS * 8
LANES = 128
VMEM_LIMIT = 56 * 1024 * 1024

NT_DIMS = (((1,), (1,)), ((), ()))


def _cparams(*sem):
    return pltpu.CompilerParams(dimension_semantics=sem, vmem_limit_bytes=VMEM_LIMIT)


def _rms(x):
    return x * lax.rsqrt(jnp.mean(x * x, axis=-1, keepdims=True) + EPS)


def _sigmoid(x):
    return 1.0 / (1.0 + jnp.exp(-x))


def _sigmoid_abs(x):
    return 0.5 * jnp.tanh(0.5 * x) + 0.5


def _alternate(stages):
    stages = list(stages)
    while stages:
        for g in list(stages):
            if next(g, "done") == "done":
                stages.remove(g)


def _split_bf16(x):
    hi = x.astype(BF16)
    return hi, (x - hi.astype(F32)).astype(BF16)


def _adaln_body(c_ref, w_ref, b_ref, o_ref):
    c = c_ref[...]
    cond = c * _sigmoid(c)
    o_ref[...] = jnp.dot(cond, w_ref[...], preferred_element_type=F32,
                         precision=lax.Precision.HIGHEST) + b_ref[...]


def adaln(cond8, ada_w, ada_b):
    depth, d, n = ada_w.shape
    tn = 1536
    return pl.pallas_call(
        _adaln_body,
        grid=(depth, n // tn),
        in_specs=[pl.BlockSpec((8, d), lambda l, j: (0, 0)),
                  pl.BlockSpec((None, d, tn), lambda l, j: (l, 0, j)),
                  pl.BlockSpec((None, 1, tn), lambda l, j: (l, 0, j))],
        out_specs=pl.BlockSpec((None, 8, tn), lambda l, j: (l, 0, j)),
        out_shape=jax.ShapeDtypeStruct((depth, 8, n), F32),
        compiler_params=_cparams("parallel", "parallel"),
        name="adaln",
    )(cond8, ada_w, ada_b.reshape(depth, 1, n))


def _tile_rows(refs, n_lat):
    if len(refs) == 1:
        return refs[0][...]
    return jnp.where(pl.program_id(0) < n_lat, refs[0][...], refs[1][...])


def _inproj_body(*refs, n_x, n_lat):
    (g_ref, sh_ref, sc_ref, w_ref, cs_ref, sa_ref, sb_ref, q_ref, kv_ref, lru_ref, hg_ref) = refs[n_x:]
    h = _rms(_tile_rows(refs[:n_x], n_lat)) * g_ref[...]
    h = h * (1.0 + sc_ref[...]) + sh_ref[...]
    p = jnp.dot(h.astype(BF16), w_ref[...], preferred_element_type=F32)
    cs, sa, sb = cs_ref[...], sa_ref[...], sb_ref[...]

    def rope(t):
        return t * cs + pltpu.roll(t, LANES - 16, 1) * sa + pltpu.roll(t, 16, 1) * sb

    for j in range(4):
        q_ref[:, j * LANES:(j + 1) * LANES] = (
            rope(p[:, j * LANES:(j + 1) * LANES]) * (HEAD_DIM ** -0.5)).astype(BF16)
    k = rope(p[:, 512:640])
    v = p[:, 640:768]
    lo = lax.broadcasted_iota(I32, k.shape, 1) < HEAD_DIM
    kr = pltpu.roll(k, HEAD_DIM, 1)
    vr = pltpu.roll(v, HEAD_DIM, 1)
    kv_ref[:, 0:128] = jnp.where(lo, k, kr).astype(BF16)
    kv_ref[:, 128:256] = jnp.where(lo, kr, k).astype(BF16)
    kv_ref[:, 256:384] = jnp.where(lo, v, vr).astype(BF16)
    kv_ref[:, 384:512] = jnp.where(lo, vr, v).astype(BF16)
    lru_ref[...] = p[:, 768:1280]
    hg_ref[...] = p[:, 1280:2560]


def _stream_specs(xs, n_lat, d):
    if len(xs) == 1:
        return [pl.BlockSpec((TM, d), lambda i: (i, 0))]
    return [pl.BlockSpec((TM, d), lambda i: (jnp.minimum(i, n_lat - 1), 0)),
            pl.BlockSpec((TM, d), lambda i: (0, 0))]


def inproj(xs, g, shift, scale, w_bf, rope_c, rope_a, rope_b, *, tiles_per_batch, n_seg):
    d = xs[0].shape[1]
    n_lat = tiles_per_batch * (n_seg - 1)
    n_tiles = n_lat + 1
    r = n_tiles * TM

    def seg(i):
        return (jnp.minimum(i // tiles_per_batch, n_seg - 1), 0, 0)

    def rope_idx(i):
        return (jnp.where(i < n_lat, i % tiles_per_batch, tiles_per_batch), 0)

    row = lambda w: pl.BlockSpec((TM, w), lambda i: (i, 0))
    return pl.pallas_call(
        functools.partial(_inproj_body, n_x=len(xs), n_lat=n_lat),
        grid=(n_tiles,),
        in_specs=_stream_specs(xs, n_lat, d) + [
                  pl.BlockSpec((1, d), lambda i: (0, 0)),
                  pl.BlockSpec((None, 1, d), seg),
                  pl.BlockSpec((None, 1, d), seg),
                  pl.BlockSpec(w_bf.shape, lambda i: (0, 0)),
                  pl.BlockSpec((TM, LANES), rope_idx),
                  pl.BlockSpec((TM, LANES), rope_idx),
                  pl.BlockSpec((TM, LANES), rope_idx)],
        out_specs=[row(512), row(512), row(512), row(1280)],
        out_shape=[jax.ShapeDtypeStruct((r, 512), BF16),
                   jax.ShapeDtypeStruct((r, 512), BF16),
                   jax.ShapeDtypeStruct((r, 512), F32),
                   jax.ShapeDtypeStruct((r, 1280), F32)],
        compiler_params=_cparams("parallel"),
        name="inproj",
    )(*xs, g, shift, scale, w_bf, rope_c, rope_a, rope_b)


def _softmax_pv(s_parts, v_parts, sink_col):
    def lane_blocks(a):
        return [a[:, j:j + LANES] for j in range(0, a.shape[1], LANES)]

    folded = None
    for s in s_parts:
        for blk in lane_blocks(s):
            folded = blk if folded is None else jnp.maximum(folded, blk)
    m = jnp.maximum(sink_col, jnp.max(folded, axis=-1, keepdims=True))
    psum = None
    o = None
    for s, v in zip(s_parts, v_parts):
        p = jnp.exp(s - m)
        for blk in lane_blocks(p):
            psum = blk if psum is None else psum + blk
        pv = jnp.dot(p.astype(BF16), v, preferred_element_type=F32)
        o = pv if o is None else o + pv
    l = jnp.exp(sink_col - m) + jnp.sum(psum, axis=-1, keepdims=True)
    return o / l


def _attn_body(sink_ref, q_ref, kvm_ref, kvp_ref, kvn_ref, kvc_ref, o_ref, *,
               tiles_per_batch, n_lat, ctx_len, n_batch):
    i = pl.program_id(0)
    blk = WINDOW
    nsub = TM // blk

    def stacked_q(q_sub, g):
        rows = q_sub.shape[0]
        lo = lax.broadcasted_iota(I32, (rows, LANES), 1) < HEAD_DIM
        parts = []
        for jq in (2 * g, 2 * g + 1):
            qg = q_sub[:, jq * LANES:(jq + 1) * LANES]
            parts.append(jnp.where(lo, qg, jnp.zeros_like(qg)))
            parts.append(jnp.where(lo, jnp.zeros_like(qg), qg))
        return jnp.concatenate(parts, axis=0)

    def sink_column(g, rows):
        blk_id = lax.broadcasted_iota(I32, (4 * rows, 1), 0) // rows
        col = jnp.full((4 * rows, 1), sink_ref[4 * g + 3], F32)
        for b in range(3):
            col = jnp.where(blk_id == b, sink_ref[4 * g + b], col)
        return col

    def write_heads(o, g, rows, row0):
        lo = lax.broadcasted_iota(I32, (rows, LANES), 1) < HEAD_DIM
        for t, jq in enumerate((2 * g, 2 * g + 1)):
            o_e0 = o[(2 * t) * rows:(2 * t + 1) * rows]
            o_e1 = o[(2 * t + 1) * rows:(2 * t + 2) * rows]
            o_ref[row0:row0 + rows, jq * LANES:(jq + 1) * LANES] = (
                jnp.where(lo, o_e0, o_e1).astype(BF16))

    @pl.when(i < n_lat)
    def _latent():
        j = i % tiles_per_batch
        r_io = lax.broadcasted_iota(I32, (4 * blk, blk), 0) % blk
        c_io = lax.broadcasted_iota(I32, (4 * blk, blk), 1)
        def kv_block(b, col):
            if b < 0:
                return kvp_ref[:, col:col + LANES]
            if b >= nsub:
                return kvn_ref[:, col:col + LANES]
            return kvm_ref[b * blk:(b + 1) * blk, col:col + LANES]

        def unit(sb, g):
            kc, vc = g * LANES, (2 + g) * LANES
            q_sub = q_ref[sb * blk:(sb + 1) * blk, :]
            has_prev = jnp.logical_or(j > 0, sb > 0)
            has_next = jnp.logical_or(j < tiles_per_batch - 1, sb < nsub - 1)
            qs = stacked_q(q_sub, g)
            s_prev = lax.dot_general(qs, kv_block(sb - 1, kc), NT_DIMS, preferred_element_type=F32)
            s_own = lax.dot_general(qs, kv_block(sb, kc), NT_DIMS, preferred_element_type=F32)
            s_next = lax.dot_general(qs, kv_block(sb + 1, kc), NT_DIMS, preferred_element_type=F32)
            s_ctx = lax.dot_general(qs, kvc_ref[:, kc:kc + LANES], NT_DIMS, preferred_element_type=F32)
            yield
            s_prev = jnp.where(jnp.logical_and(c_io >= r_io, has_prev), s_prev, NEG_INF)
            s_next = jnp.where(jnp.logical_and(c_io <= r_io, has_next), s_next, NEG_INF)
            o = _softmax_pv(
                [s_prev, s_own, s_next, s_ctx],
                [kv_block(sb - 1, vc), kv_block(sb, vc), kv_block(sb + 1, vc),
                 kvc_ref[:, vc:vc + LANES]],
                sink_column(g, blk))
            yield
            write_heads(o, g, blk, sb * blk)

        for sb in range(0, nsub, 2):
            _alternate([unit(sb, 0), unit(sb, 1), unit(sb + 1, 0), unit(sb + 1, 1)])

    @pl.when(i >= n_lat)
    def _context():
        for bb in range(n_batch):
            r0 = bb * ctx_len
            q_sub = q_ref[r0:r0 + ctx_len, :]
            for g in range(2):
                kc, vc = g * LANES, (2 + g) * LANES
                qs = stacked_q(q_sub, g)
                s = lax.dot_general(qs, kvm_ref[r0:r0 + ctx_len, kc:kc + LANES], NT_DIMS,
                                    preferred_element_type=F32)
                o = _softmax_pv([s], [kvm_ref[r0:r0 + ctx_len, vc:vc + LANES]],
                                sink_column(g, ctx_len))
                write_heads(o, g, ctx_len, r0)


def attention(q, kv, sink, *, tiles_per_batch, n_batch, ctx_len, with_ctx):
    r = q.shape[0]
    n_lat = tiles_per_batch * n_batch
    assert n_batch * ctx_len == TM and r // TM == n_lat + 1
    n_tiles = n_lat + (1 if with_ctx else 0)
    sub = TM // WINDOW
    n_blk128 = r // WINDOW
    ctx_blk0 = (n_lat * TM) // ctx_len

    def prev_idx(i, s):
        return (jnp.maximum(i * sub - 1, 0), 0)

    def next_idx(i, s):
        return (jnp.minimum(i * sub + sub, n_blk128 - 1), 0)

    def ctx_idx(i, s):
        return (ctx_blk0 + jnp.minimum(i // tiles_per_batch, n_batch - 1), 0)

    body = functools.partial(_attn_body, tiles_per_batch=tiles_per_batch, n_lat=n_lat,
                             ctx_len=ctx_len, n_batch=n_batch)
    return pl.pallas_call(
        body,
        grid_spec=pltpu.PrefetchScalarGridSpec(
            num_scalar_prefetch=1,
            grid=(n_tiles,),
            in_specs=[pl.BlockSpec((TM, 512), lambda i, s: (i, 0)),
                      pl.BlockSpec((TM, 512), lambda i, s: (i, 0)),
                      pl.BlockSpec((WINDOW, 512), prev_idx),
                      pl.BlockSpec((WINDOW, 512), next_idx),
                      pl.BlockSpec((ctx_len, 512), ctx_idx)],
            out_specs=pl.BlockSpec((TM, 512), lambda i, s: (i, 0))),
        out_shape=jax.ShapeDtypeStruct((n_tiles * TM, 512), BF16),
        compiler_params=_cparams("parallel"),
        name="attention",
    )(sink, q, kv, kv, kv, kv)


def _scan_maps(n_batch, seq, ctx_len):
    assert ctx_len == TSCAN
    nl = seq // TSCAN
    ctx0 = (n_batch * seq) // TSCAN

    def fwd(b, c):
        return jnp.where(c == 0, ctx0 + b, b * nl + c - 1)

    def bwd(b, c):
        return jnp.where(c == 0, ctx0 + b, b * nl + nl - c)

    return nl, fwd, bwd


def _lru_dir(x_ref, xp_ref, xn_ref, has_prev, has_next, cw_ref, cb_ref, w_ref, br_ref, bi_ref,
             lam_ref, a_scr, b_scr, reverse):
    t = TSCAN
    xp = jnp.where(has_prev, xp_ref[...], 0.0)
    xn = jnp.where(has_next, xn_ref[...], 0.0)
    xpad = jnp.concatenate([xp, x_ref[...], xn], axis=0)
    n = t + 16
    cw = cw_ref[...]
    u = cb_ref[...] + jnp.zeros((t, xpad.shape[1]), F32)
    for j in range(4):
        sh = (2 - j) % n
        rolled = xpad if sh == 0 else pltpu.roll(xpad, sh, 0)
        u = u + rolled[8:8 + t] * cw[j:j + 1, :]
    gates = jnp.dot(u.astype(BF16), w_ref[...], preferred_element_type=F32)
    yield
    c = u.shape[1]
    r = _sigmoid_abs(gates[:, :c] + br_ref[...])
    ig = _sigmoid_abs(gates[:, c:] + bi_ref[...])
    nl = -lam_ref[...]
    softplus = jnp.maximum(nl, 0.0) + jnp.log1p(jnp.exp(-jnp.abs(nl)))
    log_a = (-LRU_C) * r * softplus
    a = jnp.exp(log_a)
    mult = jnp.sqrt(jnp.maximum(1.0 - a * a, SQRT_FLOOR))
    bv = mult * (ig * u)
    yield

    g8 = t // 8
    a3 = a.reshape(g8, 8, c)
    b3 = bv.reshape(g8, 8, c)
    r8 = lax.broadcasted_iota(I32, (g8, 8, c), 1)
    for s in (1, 2, 4):
        if reverse:
            a_sh, b_sh, ok = pltpu.roll(a3, 8 - s, 1), pltpu.roll(b3, 8 - s, 1), r8 < 8 - s
        else:
            a_sh, b_sh, ok = pltpu.roll(a3, s, 1), pltpu.roll(b3, s, 1), r8 >= s
        b3 = jnp.where(ok, a3 * b_sh + b3, b3)
        a3 = jnp.where(ok, a3 * a_sh, a3)
    a_scr[...] = a3.reshape(t, c)
    b_scr[...] = b3.reshape(t, c)


def _lru_body(xf_ref, xfp_ref, xfn_ref, xb_ref, xbp_ref, xbn_ref, cw_ref, cb_ref, w_ref, br_ref,
              bi_ref, lam_ref, hf_ref, hb_ref, st_ref, a_scr, b_scr, *, nl):
    c = pl.program_id(1)

    @pl.when(c == 0)
    def _init():
        st_ref[...] = jnp.zeros_like(st_ref)

    lat = c > 0
    _alternate([
        _lru_dir(xf_ref, xfp_ref, xfn_ref, jnp.logical_and(lat, c > 1), jnp.logical_and(lat, c < nl),
                 cw_ref, cb_ref, w_ref.at[0], br_ref.at[0], bi_ref.at[0], lam_ref.at[0],
                 a_scr.at[0], b_scr.at[0], False),
        _lru_dir(xb_ref, xbp_ref, xbn_ref, jnp.logical_and(lat, c < nl), jnp.logical_and(lat, c > 1),
                 cw_ref, cb_ref, w_ref.at[1], br_ref.at[1], bi_ref.at[1], lam_ref.at[1],
                 a_scr.at[1], b_scr.at[1], True)])

    g8 = TSCAN // 8

    def group(gi, carry):
        cf, cb = carry
        rf = pl.ds(pl.multiple_of(gi * 8, 8), 8)
        rb = pl.ds(pl.multiple_of((g8 - 1 - gi) * 8, 8), 8)
        hf = b_scr[0, rf, :] + a_scr[0, rf, :] * cf
        hb = b_scr[1, rb, :] + a_scr[1, rb, :] * cb
        hf_ref[rf, :] = hf
        hb_ref[rb, :] = hb
        return (jnp.broadcast_to(hf[7:8, :], hf.shape), jnp.broadcast_to(hb[0:1, :], hb.shape))

    cf, cb = lax.fori_loop(0, g8, group, (st_ref[0], st_ref[1]), unroll=4)
    st_ref[0] = cf
    st_ref[1] = cb


def lru_scan(lru, conv_w, conv_b, w_gates, br, bi, lam, *, n_batch, seq, ctx_len):
    r = lru.shape[0]
    c = lru.shape[1] // 2
    nl, fwd, bwd = _scan_maps(n_batch, seq, ctx_len)
    per = TSCAN // 8
    n8 = r // 8

    def main(m):
        return pl.BlockSpec((TSCAN, c), lambda b, s: (m(b, s), 0))

    def prev(m):
        return pl.BlockSpec((8, c), lambda b, s: (jnp.maximum(m(b, s) * per - 1, 0), 0))

    def nxt(m):
        return pl.BlockSpec((8, c), lambda b, s: (jnp.minimum(m(b, s) * per + per, n8 - 1), 0))

    full = lambda a: pl.BlockSpec(a.shape, lambda b, s: (0,) * a.ndim)
    return pl.pallas_call(
        functools.partial(_lru_body, nl=nl),
        grid=(n_batch, nl + 1),
        in_specs=[main(fwd), prev(fwd), nxt(fwd), main(bwd), prev(bwd), nxt(bwd),
                  full(conv_w), full(conv_b), full(w_gates), full(br), full(bi), full(lam)],
        out_specs=[main(fwd), main(bwd)],
        out_shape=[jax.ShapeDtypeStruct((r, c), F32)] * 2,
        scratch_shapes=[pltpu.VMEM((2, 8, c), F32), pltpu.VMEM((2, TSCAN, c), F32),
                        pltpu.VMEM((2, TSCAN, c), F32)],
        compiler_params=_cparams("parallel", "arbitrary"),
        name="lru_scan",
    )(lru, lru, lru, lru, lru, lru, conv_w, conv_b, w_gates, br, bi, lam)


def _hgrn_dir(q_ref, z_ref, v_ref, lb_ref, st_ref, out_ref, reverse):
    t = TSCAN
    nch = t // HG_CHUNK
    q = q_ref[...]
    z = z_ref[...]
    v = v_ref[...]
    lb = lb_ref[...]
    w = q.shape[1]
    log_f = jnp.log(lb + (1.0 - lb) * _sigmoid(z))
    k = (1.0 - lb) * _sigmoid(-z)

    ri = lax.broadcasted_iota(I32, (t, t), 0)
    ci = lax.broadcasted_iota(I32, (t, t), 1)
    same = (ri // HG_CHUNK) == (ci // HG_CHUNK)
    causal = jnp.logical_and(same, (ci >= ri) if reverse else (ci <= ri))

    yield
    rc = lax.broadcasted_iota(I32, (t, w), 0) % HG_CHUNK
    b = log_f
    s = 1
    while s < HG_CHUNK:
        if reverse:
            b = b + jnp.where(rc < HG_CHUNK - s, pltpu.roll(b, t - s, 0), 0.0)
        else:
            b = b + jnp.where(rc >= s, pltpu.roll(b, s, 0), 0.0)
        s *= 2
    yield
    tot = jnp.sum(log_f.reshape(nch, HG_CHUNK, w), axis=1, keepdims=True)
    b_last = jnp.broadcast_to(tot, (nch, HG_CHUNK, w)).reshape(t, w)
    b_half = 0.5 * b_last
    q_i = (q * jnp.exp(b - b_half)).astype(BF16)
    k_i = (k * jnp.exp(b_half - b)).astype(BF16)
    q_s = (q * jnp.exp(b)).astype(BF16)
    k_e = (k * jnp.exp(b_last - b)).astype(BF16)
    v_bf = v.astype(BF16)
    lo = lax.broadcasted_iota(I32, (t, LANES), 1) < HEAD_DIM
    chunk_of_row = lax.broadcasted_iota(I32, (t, LANES), 0) // HG_CHUNK
    bd_r = lax.broadcasted_iota(I32, (LANES, LANES), 0) // HEAD_DIM
    bd_c = lax.broadcasted_iota(I32, (LANES, LANES), 1) // HEAD_DIM
    block_diag = bd_r == bd_c
    yield

    def head_pair(p):
        cols = slice(p * LANES, (p + 1) * LANES)
        qi_p, ki_p, qs_p, ke_p, v_p = q_i[:, cols], k_i[:, cols], q_s[:, cols], k_e[:, cols], v_bf[:, cols]
        o_half = []
        for e in range(2):
            qm = jnp.where(lo if e == 0 else jnp.logical_not(lo), qi_p, jnp.zeros_like(qi_p))
            att = lax.dot_general(qm, ki_p, NT_DIMS, preferred_element_type=F32)
            att = jnp.where(causal, att, 0.0).astype(BF16)
            o_half.append(jnp.dot(att, v_p, preferred_element_type=F32))
            yield
        v_t = v[:, cols].T.astype(BF16)
        st = st_ref[p]
        o_inter = [None] * nch
        order = range(nch - 1, -1, -1) if reverse else range(nch)
        for n in order:
            rows = slice(n * HG_CHUNK, (n + 1) * HG_CHUNK)
            o_inter[n] = lax.dot_general(qs_p[rows], st.astype(BF16), NT_DIMS,
                                         preferred_element_type=F32)
            ke_n = jnp.where(chunk_of_row == n, ke_p, jnp.zeros_like(ke_p))
            kv_t = jnp.dot(v_t, ke_n, preferred_element_type=F32)
            decay = jnp.exp(b_last[n * HG_CHUNK:n * HG_CHUNK + 1, cols])
            st = decay * st + jnp.where(block_diag, kv_t, 0.0)
            yield
        st_ref[p] = st
        out_ref[:, cols] = jnp.where(lo, o_half[0], o_half[1]) + jnp.concatenate(o_inter, axis=0)

    pairs = [head_pair(p) for p in range(w // LANES)]
    while pairs:
        for g in list(pairs):
            if next(g, "done") == "done":
                pairs.remove(g)
        yield


def _hgrn_body(qf_ref, zf_ref, vf_ref, qb_ref, zb_ref, vb_ref, lb_ref, of_ref, ob_ref, st_ref):
    c = pl.program_id(1)

    @pl.when(c == 0)
    def _init():
        st_ref[...] = jnp.zeros_like(st_ref)

    _alternate([_hgrn_dir(qf_ref, zf_ref, vf_ref, lb_ref.at[0], st_ref.at[0], of_ref, False),
                _hgrn_dir(qb_ref, zb_ref, vb_ref, lb_ref.at[1], st_ref.at[1], ob_ref, True)])


def hgrn_scan(hg, lower_bounds, *, n_batch, seq, ctx_len):
    r = hg.shape[0]
    w = hg.shape[1] // 5
    nl, fwd, bwd = _scan_maps(n_batch, seq, ctx_len)

    def col(m, j):
        return pl.BlockSpec((TSCAN, w), lambda b, s: (m(b, s), j))

    lb3 = lower_bounds.reshape(2, 1, w)
    return pl.pallas_call(
        _hgrn_body,
        grid=(n_batch, nl + 1),
        in_specs=[col(fwd, 0), col(fwd, 1), col(fwd, 3), col(bwd, 0), col(bwd, 2), col(bwd, 3),
                  pl.BlockSpec(lb3.shape, lambda b, s: (0, 0, 0))],
        out_specs=[col(fwd, 0), col(bwd, 0)],
        out_shape=[jax.ShapeDtypeStruct((r, w), F32)] * 2,
        scratch_shapes=[pltpu.VMEM((2, w // LANES, LANES, LANES), F32)],
        compiler_params=_cparams("parallel", "arbitrary"),
        name="hgrn_scan",
    )(hg, hg, hg, hg, hg, hg, lb3)


def _outproj_router_body(*refs, n_x, n_lat):
    (att_ref, hf_ref, hb_ref, lg_ref, of_ref, ob_ref, gg_ref, hn_ref,
     wo_ref, gm_ref, sh_ref, sc_ref, n2_ref, rw_ref, rb_ref,
     xo_ref, h2_ref, idx_ref, gate_ref, rank_ref, cnt_ref) = refs[n_x:]
    lru_y = (hf_ref[...] + hb_ref[...]) * jax.nn.gelu(lg_ref[...], approximate=True)
    o = of_ref[...] + ob_ref[...]
    w = o.shape[1]
    gr = lax.broadcasted_iota(I32, (w, w), 0) // HEAD_DIM
    gc = lax.broadcasted_iota(I32, (w, w), 1) // HEAD_DIM
    head_mean = jnp.where(gr == gc, 1.0 / HEAD_DIM, 0.0).astype(BF16)
    sq_hi, sq_lo = _split_bf16(o * o)
    ms = (jnp.dot(sq_hi, head_mean, preferred_element_type=F32)
          + jnp.dot(sq_lo, head_mean, preferred_element_type=F32))
    gg = gg_ref[...]
    hg_y = o * lax.rsqrt(ms + EPS) * hn_ref[...] * (gg * _sigmoid_abs(gg))
    y = jnp.dot(att_ref[...], wo_ref[0:512, :], preferred_element_type=F32)
    y = y + jnp.dot(lru_y.astype(BF16), wo_ref[512:768, :], preferred_element_type=F32)
    y = y + jnp.dot(hg_y.astype(BF16), wo_ref[768:1024, :], preferred_element_type=F32)
    x = _tile_rows(refs[:n_x], n_lat) + gm_ref[...] * y
    xo_ref[...] = x
    h2 = _rms(x) * n2_ref[...]
    h2 = h2 * (1.0 + sc_ref[...]) + sh_ref[...]
    h2_ref[...] = h2

    logits = lax.dot_general(rw_ref[...], h2, NT_DIMS, preferred_element_type=F32,
                             precision=lax.Precision.HIGHEST) + rb_ref[...]
    ne, tm = logits.shape
    e_io = lax.broadcasted_iota(I32, (ne, tm), 0).astype(F32)
    work = logits
    vals, hots = [], []
    for k in range(TOP_K):
        m = jnp.max(work, axis=0, keepdims=True)
        idx = jnp.min(jnp.where(work == m, e_io, float(ne)), axis=0, keepdims=True)
        hot = e_io == idx
        vals.append(m)
        hots.append(hot)
        idx_ref[k:k + 1, :] = idx.astype(I32)
        work = jnp.where(hot, -jnp.inf, work)
    exps = [jnp.exp(v - vals[0]) for v in vals]
    denom = exps[0] + exps[1] + exps[2] + exps[3]
    for k in range(TOP_K):
        gate_ref[k:k + 1, :] = exps[k] / denom
    z4 = jnp.zeros((8 - TOP_K, tm), F32)
    gate_ref[TOP_K:8, :] = z4
    idx_ref[TOP_K:8, :] = z4.astype(I32)
    rank_ref[TOP_K:8, :] = z4.astype(I32)

    chosen = jnp.logical_or(jnp.logical_or(hots[0], hots[1]), jnp.logical_or(hots[2], hots[3]))
    sr = lax.broadcasted_iota(I32, (tm, tm), 0)
    sc = lax.broadcasted_iota(I32, (tm, tm), 1)
    before = jnp.where(sr < sc, 1.0, 0.0).astype(BF16)
    chosen_f = jnp.where(chosen, 1.0, 0.0)
    prefix = jnp.dot(chosen_f.astype(BF16), before, preferred_element_type=F32)
    for k in range(TOP_K):
        rk = jnp.sum(jnp.where(hots[k], prefix, 0.0), axis=0, keepdims=True)
        rank_ref[k:k + 1, :] = rk.astype(I32)
    cnt_ref[...] = jnp.broadcast_to(jnp.sum(chosen_f, axis=1, keepdims=True), cnt_ref.shape)


def outproj_router(xs, att, hf, hb, lru, of, ob, hg, hn_g, wo_bf, gate_msa, shift_mlp, scale_mlp,
                   n2_g, rw_t, rb, *, tiles_per_batch, n_seg, with_ctx):
    d = xs[0].shape[1]
    n_lat = tiles_per_batch * (n_seg - 1)
    n_tiles = n_lat + (1 if with_ctx else 0)
    r = n_tiles * TM

    def seg(i):
        return (jnp.minimum(i // tiles_per_batch, n_seg - 1), 0, 0)

    row = lambda w, j=0: pl.BlockSpec((TM, w), lambda i: (i, j))
    full = lambda a: pl.BlockSpec(a.shape, lambda i: (0,) * a.ndim)
    modspec = pl.BlockSpec((None, 1, d), seg)
    lane_out = pl.BlockSpec((8, TM), lambda i: (0, i))
    return pl.pallas_call(
        functools.partial(_outproj_router_body, n_x=len(xs), n_lat=n_lat),
        grid=(n_tiles,),
        in_specs=_stream_specs(xs, n_lat, d) + [
                  row(512), row(256), row(256), row(256, 1), row(256), row(256),
                  row(256, 4), full(hn_g), full(wo_bf), modspec, modspec, modspec, full(n2_g),
                  full(rw_t), full(rb)],
        out_specs=[row(d), row(d), lane_out, lane_out, lane_out,
                   pl.BlockSpec((None, N_EXPERTS, LANES), lambda i: (i, 0, 0))],
        out_shape=[jax.ShapeDtypeStruct((r, d), F32), jax.ShapeDtypeStruct((r, d), F32),
                   jax.ShapeDtypeStruct((8, r), I32), jax.ShapeDtypeStruct((8, r), F32),
                   jax.ShapeDtypeStruct((8, r), I32),
                   jax.ShapeDtypeStruct((n_tiles, N_EXPERTS, LANES), F32)],
        compiler_params=_cparams("parallel"),
        name="outproj_router",
    )(*xs, att, hf, hb, lru, of, ob, hg, hn_g, wo_bf, gate_msa, shift_mlp, scale_mlp, n2_g, rw_t, rb)


def _local_pos(idx_ref, rank_ref, toff_ref, k):
    ne = toff_ref.shape[0]
    tm = idx_ref.shape[1]
    e_io = lax.broadcasted_iota(I32, (ne, tm), 0)
    off = jnp.sum(jnp.where(e_io == idx_ref[k:k + 1, :], toff_ref[...], 0.0), axis=0, keepdims=True)
    return off + rank_ref[k:k + 1, :].astype(F32)


def _run_copies(tab_ref, tile, make_copy, sems, slot, start):
    base = tile * LANES

    def per_expert(e, carry):
        src8 = tab_ref[base + e]
        dst8 = tab_ref[base + N_EXPERTS + e]
        n8 = tab_ref[base + 2 * N_EXPERTS + e]

        for b in range(RUN_BITS):
            size = 8 << b
            off8 = (n8 >> (b + 1)) << (b + 1)

            @pl.when(((n8 >> b) & 1) == 1)
            def _():
                cp = make_copy(pl.multiple_of((src8 + off8) * 8, 8),
                               pl.multiple_of((dst8 + off8) * 8, 8), size, sems.at[slot, b])
                if start:
                    cp.start()
                else:
                    cp.wait()
        return carry

    lax.fori_loop(0, N_EXPERTS, per_expert, 0)


def _fill_copies(fill_ref, zero_scr, xs_ref, sems, start):
    def go(cp):
        if start:
            cp.start()
        else:
            cp.wait()

    def per_expert(e, carry):
        t8 = fill_ref[e]
        n8 = fill_ref[N_EXPERTS + e]
        for b in range(TAIL_BITS):
            size = 8 << b
            off8 = (n8 >> (b + 1)) << (b + 1)

            @pl.when(((n8 >> b) & 1) == 1)
            def _():
                go(pltpu.make_async_copy(
                    zero_scr.at[pl.ds(0, size), :],
                    xs_ref.at[pl.ds(pl.multiple_of((t8 + off8) * 8, 8), size), :], sems.at[0, b]))
        return carry

    lax.fori_loop(0, N_EXPERTS, per_expert, 0)

    def per_block(j, carry):
        go(pltpu.make_async_copy(zero_scr, xs_ref.at[pl.ds(pl.multiple_of(j * TB, TB), TB), :],
                                 sems.at[0, TAIL_BITS]))
        return carry

    lax.fori_loop(fill_ref[2 * N_EXPERTS], xs_ref.shape[0] // TB, per_block, 0)


def _dispatch_body(fill_ref, tab_ref, h_ref, idx_ref, rank_ref, toff_ref, xs_ref, z_scr, zero_scr, sems):
    tm = h_ref.shape[0]
    i = pl.program_id(0)
    slot = i % 2

    @pl.when(i == 0)
    def _fill():
        zero_scr[...] = jnp.zeros_like(zero_scr)
        _fill_copies(fill_ref, zero_scr, xs_ref, sems, True)
        _fill_copies(fill_ref, zero_scr, xs_ref, sems, False)

    r_io = lax.broadcasted_iota(I32, (ZR, tm), 0).astype(F32)
    hit = r_io == _local_pos(idx_ref, rank_ref, toff_ref, 0)
    for k in range(1, TOP_K):
        hit = jnp.logical_or(hit, r_io == _local_pos(idx_ref, rank_ref, toff_ref, k))
    perm = jnp.where(hit, 1.0, 0.0).astype(BF16)
    z_scr[slot] = jnp.dot(perm, h_ref[...].astype(BF16), preferred_element_type=F32)

    def copy_from(buf):
        def make_copy(s, d, size, sem):
            return pltpu.make_async_copy(z_scr.at[buf, pl.ds(s, size), :],
                                         xs_ref.at[pl.ds(d, size), :], sem)
        return make_copy

    _run_copies(tab_ref, i, copy_from(slot), sems, slot, True)

    @pl.when(i > 0)
    def _drain_previous():
        _run_copies(tab_ref, i - 1, copy_from(1 - slot), sems, 1 - slot, False)

    @pl.when(i == pl.num_programs(0) - 1)
    def _drain_last():
        _run_copies(tab_ref, i, copy_from(slot), sems, slot, False)


def moe_dispatch(h2, idx, rank, toff_col, tab, fill, p_rows):
    r, d = h2.shape
    n_tiles = r // TM
    lane_in = pl.BlockSpec((8, TM), lambda i, f, t: (0, i))
    return pl.pallas_call(
        _dispatch_body,
        grid_spec=pltpu.PrefetchScalarGridSpec(
            num_scalar_prefetch=2,
            grid=(n_tiles,),
            in_specs=[pl.BlockSpec((TM, d), lambda i, f, t: (i, 0)),
                      lane_in, lane_in,
                      pl.BlockSpec((None, N_EXPERTS, 1), lambda i, f, t: (i, 0, 0))],
            out_specs=pl.BlockSpec(memory_space=pl.ANY),
            scratch_shapes=[pltpu.VMEM((2, ZR, d), F32), pltpu.VMEM((TB, d), F32),
                            pltpu.SemaphoreType.DMA((2, N_COPY_SEMS))]),
        out_shape=jax.ShapeDtypeStruct((p_rows, d), F32),
        compiler_params=_cparams("arbitrary"),
        name="moe_dispatch",
    )(fill, tab, h2, idx, rank, toff_col)


def _expert_body(be_ref, rows_ref, next_ref, src_ref, x_ref, bgu_ref, bd_ref, wgu_hbm, wd_hbm, y_ref,
                 wgu_stage, wd_stage, wgu_bf, wd_bf, sems, *, e0):
    i = pl.program_id(0)
    n_rows = rows_ref[i]

    def weight_copies(e):
        return (pltpu.make_async_copy(wgu_hbm.at[e], wgu_stage, sems.at[0]),
                pltpu.make_async_copy(wd_hbm.at[e], wd_stage, sems.at[1]))

    @pl.when(i == 0)
    def _first_fetch():
        for cp in weight_copies(be_ref[0] + e0):
            cp.start()

    @pl.when(jnp.logical_and(n_rows > 0,
                             jnp.logical_or(i == 0, be_ref[i] != be_ref[jnp.maximum(i - 1, 0)])))
    def _new_expert():
        for cp in weight_copies(be_ref[i] + e0):
            cp.wait()
        rows = 128

        def chunk(j, carry):
            sl = pl.ds(pl.multiple_of(j * rows, rows), rows)
            wgu_bf[sl, :] = wgu_stage[sl, :].astype(BF16)
            wd_bf[sl, :] = wd_stage[sl, :].astype(BF16)
            return carry

        lax.fori_loop(0, wgu_stage.shape[0] // rows, chunk, 0)

        @pl.when(next_ref[i] >= 0)
        def _prefetch():
            for cp in weight_copies(next_ref[i] + e0):
                cp.start()

    def mlp(rows):
        dff = wd_bf.shape[0]
        gu = jnp.dot(x_ref[0:rows, :].astype(BF16), wgu_bf[...], preferred_element_type=F32) + bgu_ref[...]
        gate = jnp.minimum(gu[:, :dff].astype(BF16), SWIGLU_LIMIT)
        up = jnp.clip(gu[:, dff:].astype(BF16), -SWIGLU_LIMIT, SWIGLU_LIMIT)
        act = (up + 1.0) * (gate * _sigmoid_abs(SWIGLU_ALPHA * gate))
        y_ref[0:rows, :] = jnp.dot(act, wd_bf[...], preferred_element_type=F32) + bd_ref[...]

    for rows in range(TB_STEP, TB + 1, TB_STEP):
        @pl.when(jnp.logical_and(n_rows > rows - TB_STEP, n_rows <= rows))
        def _piece(rows=rows):
            mlp(rows)
            if rows < TB:
                y_ref[rows:TB, :] = jnp.zeros((TB - rows, y_ref.shape[1]), F32)

    @pl.when(n_rows == 0)
    def _unused():
        y_ref[...] = jnp.zeros_like(y_ref)


def moe_experts(xs_sorted, blk_e, blk_rows, blk_next, wgu, bgu, wd, bd, layer):
    p_rows, d = xs_sorted.shape
    depth, ne, _, dgu = wgu.shape
    dff = wd.shape[2]
    assert dff == d

    def expert(i, be, nr, nx, sr):
        return (layer * ne + be[i], 0, 0)

    n_blk = p_rows // TB
    blk_ids = jnp.arange(n_blk, dtype=I32)
    blk_src = jnp.where(blk_rows > 0, blk_ids, jnp.max(jnp.where(blk_rows > 0, blk_ids, 0)))

    return pl.pallas_call(
        functools.partial(_expert_body, e0=layer * ne),
        grid_spec=pltpu.PrefetchScalarGridSpec(
            num_scalar_prefetch=4,
            grid=(n_blk,),
            in_specs=[pl.BlockSpec((TB, d), lambda i, be, nr, nx, sr: (sr[i], 0)),
                      pl.BlockSpec((None, 1, dgu), expert),
                      pl.BlockSpec((None, 1, d), expert),
                      pl.BlockSpec(memory_space=pl.ANY),
                      pl.BlockSpec(memory_space=pl.ANY)],
            out_specs=pl.BlockSpec((TB, d), lambda i, be, nr, nx, sr: (i, 0)),
            scratch_shapes=[pltpu.VMEM((d, dgu), F32), pltpu.VMEM((dff, d), F32),
                            pltpu.VMEM((d, dgu), BF16), pltpu.VMEM((dff, d), BF16),
                            pltpu.SemaphoreType.DMA((2,))]),
        out_shape=jax.ShapeDtypeStruct((p_rows, d), F32),
        compiler_params=_cparams("arbitrary"),
        name="moe_experts",
    )(blk_e, blk_rows, blk_next, blk_src.astype(I32), xs_sorted, bgu.reshape(depth * ne, 1, dgu), bd.reshape(depth * ne, 1, d),
      wgu.reshape(depth * ne, d, dgu), wd.reshape(depth * ne, dff, d))


def _combine_body(tab_ref, x_ref, idx_ref, rank_ref, gate_ref, toff_ref, gm_ref, fg_ref, ys_ref,
                  o_ref, zy_scr, sems, *, final):
    tm = x_ref.shape[0]
    i = pl.program_id(0)
    slot = i % 2

    def copy_into(buf):
        def make_copy(s, d, size, sem):
            return pltpu.make_async_copy(ys_ref.at[pl.ds(d, size), :],
                                         zy_scr.at[buf, pl.ds(s, size), :], sem)
        return make_copy

    def fetch(tile, buf):
        zy_scr[buf, TOP_K * tm:ZR, :] = jnp.zeros((ZR - TOP_K * tm, zy_scr.shape[2]), F32)
        _run_copies(tab_ref, tile, copy_into(buf), sems, buf, True)

    @pl.when(i == 0)
    def _first():
        fetch(0, 0)

    @pl.when(i + 1 < pl.num_programs(0))
    def _prefetch_next():
        fetch(i + 1, 1 - slot)

    pos = [_local_pos(idx_ref, rank_ref, toff_ref, k) for k in range(TOP_K)]
    packed = jnp.concatenate(pos + [gate_ref[0:TOP_K, :], jnp.zeros((LANES - 2 * TOP_K, tm), F32)],
                             axis=0)
    cols = packed.T
    c_io = lax.broadcasted_iota(I32, (tm, ZR), 1).astype(F32)
    weights = jnp.zeros((tm, ZR), F32)
    for k in range(TOP_K):
        weights = jnp.where(c_io == cols[:, k:k + 1], cols[:, TOP_K + k:TOP_K + k + 1], weights)
    _run_copies(tab_ref, i, copy_into(slot), sems, slot, False)
    acc = jnp.dot(weights.astype(BF16), zy_scr[slot].astype(BF16), preferred_element_type=F32)
    x = x_ref[...] + gm_ref[...] * acc
    if final:
        x = _rms(x) * fg_ref[...]
    o_ref[...] = x


def moe_combine(xs, idx, rank, gates, toff_col, tab, ys, gate_mlp, final_g, *, tiles_per_batch,
                n_seg, n_tiles, final):
    d = xs.shape[1]

    def seg(i):
        return (jnp.minimum(i // tiles_per_batch, n_seg - 1), 0, 0)

    lane_in = pl.BlockSpec((8, TM), lambda i, t: (0, i))
    return pl.pallas_call(
        functools.partial(_combine_body, final=final),
        grid_spec=pltpu.PrefetchScalarGridSpec(
            num_scalar_prefetch=1,
            grid=(n_tiles,),
            in_specs=[pl.BlockSpec((TM, d), lambda i, t: (i, 0)),
                      lane_in, lane_in, lane_in,
                      pl.BlockSpec((None, N_EXPERTS, 1), lambda i, t: (i, 0, 0)),
                      pl.BlockSpec((None, 1, d), lambda i, t: seg(i)),
                      pl.BlockSpec((1, d), lambda i, t: (0, 0)),
                      pl.BlockSpec(memory_space=pl.ANY)],
            out_specs=pl.BlockSpec((TM, d), lambda i, t: (i, 0)),
            scratch_shapes=[pltpu.VMEM((2, ZR, d), F32), pltpu.SemaphoreType.DMA((2, N_COPY_SEMS))]),
        out_shape=jax.ShapeDtypeStruct((n_tiles * TM, d), F32),
        compiler_params=_cparams("arbitrary"),
        name="moe_combine",
    )(tab, xs, idx, rank, gates, toff_col, gate_mlp, final_g, ys)


def _rope_tables(seq):
    nf = HEAD_DIM // 4
    pos = jnp.arange(seq)
    rows = (pos // GRID_W).astype(F32)
    cols = (pos % GRID_W).astype(F32)
    inv_freq = ROPE_BASE ** (-jnp.arange(nf, dtype=F32) / nf)
    d = jnp.arange(LANES) % HEAD_DIM
    axis = d // (2 * nf)
    half = (d // nf) % 2
    f = d % nf
    ang = jnp.where(axis[None, :] == 0, rows[:, None], cols[:, None]) * inv_freq[f][None, :]
    cs, sn = jnp.cos(ang), jnp.sin(ang)
    ca = jnp.where(half[None, :] == 0, -sn, 0.0)
    cb = jnp.where(half[None, :] == 1, sn, 0.0)
    pad1 = jnp.ones((TM, LANES), F32)
    pad0 = jnp.zeros((TM, LANES), F32)
    return (jnp.concatenate([cs, pad1]), jnp.concatenate([ca, pad0]), jnp.concatenate([cb, pad0]))


def _block_diag(w):
    n, c, _ = w.shape
    eye = jnp.eye(n, dtype=w.dtype)
    return (eye[:, None, :, None] * w[:, :, None, :]).reshape(n * c, n * c)


def _moe_layout(tile_counts, n_blocks):
    cnt8 = (tile_counts.astype(I32) + 7) // 8 * 8
    toff = jnp.cumsum(cnt8, axis=1) - cnt8
    goff = jnp.cumsum(cnt8, axis=0) - cnt8
    padded = (jnp.sum(cnt8, axis=0) + TB - 1) // TB * TB
    pends = jnp.cumsum(padded)
    dst = (pends - padded)[None, :] + goff
    n_tiles = cnt8.shape[0]
    tab = jnp.concatenate([toff, dst, cnt8, jnp.zeros((n_tiles, LANES - 3 * N_EXPERTS), I32)], axis=1) // 8
    blk_start = jnp.arange(n_blocks, dtype=I32) * TB
    blk_e = jnp.minimum(jnp.sum((pends[None, :] <= blk_start[:, None]).astype(I32), axis=1), N_EXPERTS - 1)
    n_used = (pends[-1:] // TB).astype(I32)
    total = jnp.sum(cnt8, axis=0)
    region_end = pends - padded + total
    of_blk = blk_e[:, None] == jnp.arange(N_EXPERTS, dtype=I32)[None, :]
    blk_rows = jnp.clip(jnp.sum(jnp.where(of_blk, region_end[None, :], 0), axis=1) - blk_start, 0, TB)
    e_ids = jnp.arange(N_EXPERTS, dtype=I32)
    later = (e_ids[None, :] > e_ids[:, None]) & (padded[None, :] > 0)
    next_e = jnp.min(jnp.where(later, e_ids[None, :], N_EXPERTS), axis=1)
    next_e = jnp.where(next_e == N_EXPERTS, -1, next_e)
    blk_next = jnp.sum(jnp.where(of_blk, next_e[None, :], 0), axis=1)
    fill = jnp.concatenate([(pends - padded + total) // 8, (padded - total) // 8, n_used,
                            jnp.zeros((LANES - 2 * N_EXPERTS - 1,), I32)]).astype(I32)
    return (tab.astype(I32).reshape(n_tiles * LANES), toff.astype(F32).reshape(n_tiles, N_EXPERTS, 1),
            fill, blk_e.astype(I32), blk_rows.astype(I32), blk_next.astype(I32))


def kernel(x, c, ctx, c_ctx, ada_w, ada_b, norm1_g, w_in, attn_sink, conv_w, conv_b, lru_wr, lru_br,
           lru_wi, lru_bi, lru_lambda, hgrn_lb_logits, hgrn_norm_g, w_out, norm2_g, router_w,
           router_b, moe_w_gu, moe_b_gu, moe_w_down, moe_b_down, final_g):
    n_batch, seq, d = x.shape
    ctx_len = ctx.shape[1]
    depth = ada_w.shape[0]
    assert n_batch * ctx_len == TM and seq % TM == 0 and ctx_len == TSCAN
    tiles_per_batch = seq // TM
    n_seg = n_batch + 1
    n_lat = n_batch * tiles_per_batch
    r = n_batch * seq + n_batch * ctx_len
    n_blocks = -(-(r * TOP_K + (r // TM) * N_EXPERTS * 7) // TB) + N_EXPERTS
    p_rows = n_blocks * TB

    xs = (x.reshape(n_batch * seq, d), ctx.reshape(n_batch * ctx_len, d))
    cond8 = jnp.zeros((8, d), F32).at[:n_batch].set(c).at[n_batch].set(c_ctx)
    mods = adaln(cond8, ada_w, ada_b)
    rope_c, rope_a, rope_b = _rope_tables(seq)
    lb_p = jax.nn.softmax(hgrn_lb_logits.astype(F32), axis=0)
    lower_bounds = jnp.cumsum(lb_p, axis=0) - lb_p[0]

    out = None
    for layer in range(depth):
        mod = [mods[layer, :, j * d:(j + 1) * d].reshape(8, 1, d) for j in range(6)]
        q, kv, lru, hg = inproj(xs, norm1_g[layer].reshape(1, d), mod[0], mod[1],
                                w_in[layer].astype(BF16), rope_c, rope_a, rope_b,
                                tiles_per_batch=tiles_per_batch, n_seg=n_seg)
        final = layer == depth - 1
        att = attention(q, kv, attn_sink[layer], tiles_per_batch=tiles_per_batch,
                        n_batch=n_batch, ctx_len=ctx_len, with_ctx=not final)
        w_gates = jnp.stack([jnp.concatenate([_block_diag(lru_wr[layer, dd]),
                                              _block_diag(lru_wi[layer, dd])], axis=1)
                             for dd in range(2)]).astype(BF16)
        cw = lru_br.shape[-1]
        hf, hb = lru_scan(lru, conv_w[layer], conv_b[layer].reshape(1, cw), w_gates,
                          lru_br[layer].reshape(2, 1, cw), lru_bi[layer].reshape(2, 1, cw),
                          lru_lambda[layer].reshape(2, 1, cw),
                          n_batch=n_batch, seq=seq, ctx_len=ctx_len)
        of, ob = hgrn_scan(hg, lower_bounds[layer], n_batch=n_batch, seq=seq, ctx_len=ctx_len)
        x_mid, h2, idx, gates, rank, counts = outproj_router(
            xs, att, hf, hb, lru, of, ob, hg, hgrn_norm_g[layer].reshape(1, -1),
            w_out[layer].astype(BF16), mod[2], mod[3], mod[4], norm2_g[layer].reshape(1, d),
            router_w[layer].T, router_b[layer].reshape(N_EXPERTS, 1),
            tiles_per_batch=tiles_per_batch, n_seg=n_seg, with_ctx=not final)
        tab, toff_col, fill, blk_e, blk_rows, blk_next = _moe_layout(counts[:, :, 0], n_blocks)
        xs_sorted = moe_dispatch(h2, idx, rank, toff_col, tab, fill, p_rows)
        ys = moe_experts(xs_sorted, blk_e, blk_rows, blk_next, moe_w_gu, moe_b_gu, moe_w_down, moe_b_down, layer)
        res = moe_combine(x_mid, idx, rank, gates, toff_col, tab, ys, mod[5], final_g.reshape(1, d),
                          tiles_per_batch=tiles_per_batch, n_seg=n_seg,
                          n_tiles=n_lat if final else n_lat + 1, final=final)
        if final:
            out = res.reshape(n_batch, seq, d)
        else:
            xs = (res,)
    return out
```
